```python
import jax, jax.numpy as jnp
from jax import lax
import numpy as np

D_MODEL = 1024
BATCH = 8
SEQ = 2048
DEPTH = 4
DEC_BATCH = 32
DEC_SEQ = 4
PAST_LEN = 8192
PAGE_SIZE = 128

N_HEADS = 16
HEAD_DIM = D_MODEL // N_HEADS
N_KV_HEADS = 4
GROUP = N_HEADS // N_KV_HEADS
IDX_HEADS = 8
IDX_DIM = 64
TOPK_MAX = 256
Q_BLOCK = 128
N_EXPERTS = 32
TOP_E = 4
D_FF = D_MODEL
SWIGLU_LIMIT = 7.0
SWIGLU_ALPHA = 1.702
MOE_BLOCK = 128
LN_EPS = 1e-5
FORGET_BIAS_MEAN = 3.0
DEEPNORM_ALPHA = (2 * DEPTH) ** 0.25
DEEPNORM_BETA = (8 * DEPTH) ** -0.25
N_DSA_LAYERS = (DEPTH + 1) // 2
N_FOX_LAYERS = DEPTH // 2
DQ = N_HEADS * HEAD_DIM
DKV = N_KV_HEADS * HEAD_DIM
DSA_SPLITS = (DQ, DKV, DKV, IDX_HEADS * IDX_DIM, IDX_DIM, IDX_HEADS)
FOX_SPLITS = (DQ, DKV, DKV, N_HEADS)
DSA_COLS = sum(DSA_SPLITS)
FOX_COLS = sum(FOX_SPLITS)

kernel_name = 'hybrid_dsa_fox_moe_step'


def _alibi_slopes():
    return 2.0 ** (-8.0 * jnp.arange(1, N_HEADS + 1, dtype=jnp.float32) / N_HEADS)


def _split(h, sizes):
    offs = np.cumsum(sizes)[:-1].tolist()
    return jnp.split(h, offs, axis=-1)


def _rows(a, idx):
    return jax.vmap(lambda ab, ib: ab[ib])(a, idx)


def _layer_norm(x, g, b):
    xf = x.astype(jnp.float32)
    mu = jnp.mean(xf, axis=-1, keepdims=True)
    var = jnp.mean(jnp.square(xf - mu), axis=-1, keepdims=True)
    y = (xf - mu) * lax.rsqrt(var + LN_EPS) * g.astype(jnp.float32) + b.astype(jnp.float32)
    return y.astype(x.dtype)


def _blocked(fn, n_q):
    n_blk = n_q // Q_BLOCK
    out = lax.map(fn, jnp.arange(n_blk, dtype=jnp.int32))
    return out.transpose(1, 0, 2, 3).reshape(out.shape[1], n_q, out.shape[3])


def _dsa_core(q, iq, iw, q_pos, ik_all, fetch, top_k):
    B, Tq = q.shape[:2]
    L = ik_all.shape[1]
    rel = jax.nn.relu(jnp.einsum('bthd,bsd->bths', iq, ik_all, preferred_element_type=jnp.float32))
    w = iw.astype(jnp.float32) * (IDX_HEADS ** -0.5 * IDX_DIM ** -0.5)
    score = jnp.einsum('bths,bth->bts', rel, w)
    key_pos = jnp.arange(L, dtype=jnp.int32)
    admissible = key_pos[None, None, :] <= q_pos[None, :, None]
    score = jnp.where(admissible, score, -jnp.inf)
    _, sel = lax.top_k(score, top_k)
    k_sel, v_sel = fetch(sel)
    qg = q.reshape(B, Tq, N_KV_HEADS, GROUP, HEAD_DIM)
    logits = jnp.einsum('btkgd,btskd->btkgs', qg, k_sel, preferred_element_type=jnp.float32) * HEAD_DIM ** -0.5
    dist = (q_pos[None, :, None] - sel).astype(jnp.float32)
    slopes = _alibi_slopes().reshape(N_KV_HEADS, GROUP)
    logits = logits - slopes[None, None, :, :, None] * dist[:, :, None, None, :]
    valid = (sel <= q_pos[None, :, None])[:, :, None, None, :]
    p = jax.nn.softmax(jnp.where(valid, logits, -jnp.inf), axis=-1)
    out = jnp.einsum('btkgs,btskd->btkgd', p.astype(v_sel.dtype), v_sel)
    return out.reshape(B, Tq, DQ)


def _fox_core(q, k_all, v_all, cum_q, cum_k, q_pos):
    B, Tq = q.shape[:2]
    L = k_all.shape[1]
    qg = q.reshape(B, Tq, N_KV_HEADS, GROUP, HEAD_DIM)
    logits = jnp.einsum('btkgd,bskd->bkgts', qg, k_all, preferred_element_type=jnp.float32) * HEAD_DIM ** -0.5
    cq = cum_q.reshape(B, Tq, N_KV_HEADS, GROUP).transpose(0, 2, 3, 1)
    ck = cum_k.reshape(B, L, N_KV_HEADS, GROUP).transpose(0, 2, 3, 1)
    logits = logits + cq[..., :, None] - ck[..., None, :]
    causal = jnp.arange(L, dtype=jnp.int32)[None, :] <= q_pos[:, None]
    p = jax.nn.softmax(jnp.where(causal, logits, -jnp.inf), axis=-1)
    out = jnp.einsum('bkgts,bskd->btkgd', p.astype(v_all.dtype), v_all)
    return out.reshape(B, Tq, DQ)


def _dsa_project(x, w_in):
    B, T, _ = x.shape
    q, k, v, iq, ik, iw = _split(jnp.einsum('btd,dc->btc', x, w_in), DSA_SPLITS)
    return (q.reshape(B, T, N_HEADS, HEAD_DIM), k.reshape(B, T, N_KV_HEADS, HEAD_DIM),
            v.reshape(B, T, N_KV_HEADS, HEAD_DIM), iq.reshape(B, T, IDX_HEADS, IDX_DIM), ik, iw)


def _dsa_prompt(x, w_in, w_out):
    T = x.shape[1]
    q, k, v, iq, ik, iw = _dsa_project(x, w_in)
    top_k = min(TOPK_MAX, T // 4)

    def fetch(sel):
        return _rows(k, sel), _rows(v, sel)

    def block(i):
        s = i * Q_BLOCK
        sl = lambda a: lax.dynamic_slice_in_dim(a, s, Q_BLOCK, axis=1)
        q_pos = s + jnp.arange(Q_BLOCK, dtype=jnp.int32)
        return _dsa_core(sl(q), sl(iq), sl(iw), q_pos, ik, fetch, top_k)

    o = _blocked(block, T)
    return jnp.einsum('btc,cd->btd', o, w_out), k, v, ik


def _dsa_sample(x, cache_k, cache_v, cache_ik, page_table, layer, w_in, w_out):
    B, T, _ = x.shape
    past = page_table.shape[1] * PAGE_SIZE
    q, k, v, iq, ik, iw = _dsa_project(x, w_in)
    ik_past = cache_ik[layer, page_table].reshape(B, past, IDX_DIM)
    ik_all = jnp.concatenate([ik_past, ik], axis=1)
    top_k = min(TOPK_MAX, (past + T) // 4)

    def fetch(sel):
        in_past = (sel < past)[..., None, None]
        p = jnp.minimum(sel, past - 1)
        phys = _rows(page_table, p // PAGE_SIZE)
        off = p % PAGE_SIZE
        n = jnp.clip(sel - past, 0, T - 1)
        k_sel = jnp.where(in_past, cache_k[layer, phys, off], _rows(k, n))
        v_sel = jnp.where(in_past, cache_v[layer, phys, off], _rows(v, n))
        return k_sel, v_sel

    q_pos = past + jnp.arange(T, dtype=jnp.int32)
    o = _dsa_core(q, iq, iw, q_pos, ik_all, fetch, top_k)
    return jnp.einsum('btc,cd->btd', o, w_out), k, v, ik


def _fox_project(x, w_in, b_f):
    B, T, _ = x.shape
    q, k, v, fg = _split(jnp.einsum('btd,dc->btc', x, w_in), FOX_SPLITS)
    logf = jax.nn.log_sigmoid(fg.astype(jnp.float32) + b_f.astype(jnp.float32))
    return (q.reshape(B, T, N_HEADS, HEAD_DIM), k.reshape(B, T, N_KV_HEADS, HEAD_DIM),
            v.reshape(B, T, N_KV_HEADS, HEAD_DIM), logf)


def _fox_prompt(x, w_in, b_f, w_out):
    T = x.shape[1]
    q, k, v, logf = _fox_project(x, w_in, b_f)
    cum = jnp.cumsum(logf, axis=1)

    def block(i):
        s = i * Q_BLOCK
        sl = lambda a: lax.dynamic_slice_in_dim(a, s, Q_BLOCK, axis=1)
        q_pos = s + jnp.arange(Q_BLOCK, dtype=jnp.int32)
        return _fox_core(sl(q), k, v, sl(cum), cum, q_pos)

    o = _blocked(block, T)
    return jnp.einsum('btc,cd->btd', o, w_out), k, v, logf


def _fox_sample(x, cache_k, cache_v, cache_logf, page_table, layer, w_in, b_f, w_out):
    B, T, _ = x.shape
    past = page_table.shape[1] * PAGE_SIZE
    q, k, v, logf = _fox_project(x, w_in, b_f)
    k_all = jnp.concatenate([cache_k[layer, page_table].reshape(B, past, N_KV_HEADS, HEAD_DIM), k], axis=1)
    v_all = jnp.concatenate([cache_v[layer, page_table].reshape(B, past, N_KV_HEADS, HEAD_DIM), v], axis=1)
    lf_past = cache_logf[layer, page_table].reshape(B, past, N_HEADS).astype(jnp.float32)
    cum = jnp.cumsum(jnp.concatenate([lf_past, logf], axis=1), axis=1)
    q_pos = past + jnp.arange(T, dtype=jnp.int32)
    o = _fox_core(q, k_all, v_all, cum[:, past:], cum, q_pos)
    return jnp.einsum('btc,cd->btd', o, w_out), k, v, logf


def _expert(xb, e, w_gu, b_gu, w_dn, b_dn):
    h = xb @ w_gu[e] + b_gu[e]
    gate, up = jnp.split(h, 2, axis=-1)
    gate = jnp.minimum(gate, SWIGLU_LIMIT)
    up = jnp.clip(up, -SWIGLU_LIMIT, SWIGLU_LIMIT)
    glu = gate * jax.nn.sigmoid(SWIGLU_ALPHA * gate)
    return ((up + 1) * glu) @ w_dn[e] + b_dn[e]


def _moe(x, w_r, b_r, w_gu, b_gu, w_dn, b_dn):
    shape = x.shape
    xt = x.reshape(-1, D_MODEL)
    n_tok = xt.shape[0]
    n_asg = n_tok * TOP_E
    logits = jnp.einsum('td,de->te', xt, w_r, preferred_element_type=jnp.float32) + b_r.astype(jnp.float32)
    top_val, top_exp = lax.top_k(logits, TOP_E)
    gates = jax.nn.softmax(top_val, axis=-1)
    flat_e = top_exp.reshape(-1).astype(jnp.int32)
    order = jnp.argsort(flat_e).astype(jnp.int32)
    sorted_e = flat_e[order]
    counts = jnp.zeros((N_EXPERTS,), jnp.int32).at[flat_e].add(1)
    padded = (counts + MOE_BLOCK - 1) // MOE_BLOCK * MOE_BLOCK
    pad_end = jnp.cumsum(padded)
    pad_start = pad_end - padded
    start = jnp.cumsum(counts) - counts
    dest = pad_start[sorted_e] + jnp.arange(n_asg, dtype=jnp.int32) - start[sorted_e]
    n_blocks = -(-n_asg // MOE_BLOCK) + N_EXPERTS
    row_tok = jnp.full((n_blocks * MOE_BLOCK,), n_tok, jnp.int32).at[dest].set(order // TOP_E)
    x_pad = jnp.concatenate([xt, jnp.zeros((1, D_MODEL), xt.dtype)], axis=0)
    x_rows = x_pad[row_tok].reshape(n_blocks, MOE_BLOCK, D_MODEL)
    block_exp = jnp.minimum(jnp.searchsorted(pad_end, jnp.arange(n_blocks, dtype=jnp.int32) * MOE_BLOCK, side='right'), N_EXPERTS - 1)
    y_rows = lax.map(lambda a: _expert(a[0], a[1], w_gu, b_gu, w_dn, b_dn), (x_rows, block_exp)).reshape(-1, D_MODEL)
    dest_of_asg = jnp.zeros((n_asg,), jnp.int32).at[order].set(dest)
    y_asg = y_rows[dest_of_asg].reshape(n_tok, TOP_E, D_MODEL)
    out = jnp.einsum('tk,tkd->td', gates.astype(y_asg.dtype), y_asg)
    return out.reshape(shape)


def setup_inputs(seed: int = 0) -> dict:
    key = jax.random.key(seed)
    ks = jax.random.split(key, 24)
    f32 = jnp.float32
    n_pages = PAST_LEN // PAGE_SIZE
    pool = (DEC_BATCH * n_pages * 5) // 4

    def nrm(k, shape, s=1.0):
        return jax.random.normal(k, shape, f32) * s

    x_prompt = nrm(ks[0], (BATCH, SEQ, D_MODEL))
    x_sample = nrm(ks[1], (DEC_BATCH, DEC_SEQ, D_MODEL))
    cache_dsa_k = nrm(ks[2], (N_DSA_LAYERS, pool, PAGE_SIZE, N_KV_HEADS, HEAD_DIM))
    cache_dsa_v = nrm(ks[3], (N_DSA_LAYERS, pool, PAGE_SIZE, N_KV_HEADS, HEAD_DIM), DEEPNORM_BETA)
    cache_dsa_ik = nrm(ks[4], (N_DSA_LAYERS, pool, PAGE_SIZE, IDX_DIM))
    cache_fox_k = nrm(ks[5], (N_FOX_LAYERS, pool, PAGE_SIZE, N_KV_HEADS, HEAD_DIM))
    cache_fox_v = nrm(ks[6], (N_FOX_LAYERS, pool, PAGE_SIZE, N_KV_HEADS, HEAD_DIM), DEEPNORM_BETA)
    cache_fox_logf = jax.nn.log_sigmoid(FORGET_BIAS_MEAN + nrm(ks[7], (N_FOX_LAYERS, pool, PAGE_SIZE, N_HEADS)))
    page_table = jax.random.permutation(ks[8], pool)[: DEC_BATCH * n_pages].reshape(DEC_BATCH, n_pages).astype(jnp.int32)
    dsa_col_scale = jnp.ones((DSA_COLS,), f32).at[DQ + DKV: DQ + 2 * DKV].set(DEEPNORM_BETA)
    fox_col_scale = jnp.ones((FOX_COLS,), f32).at[DQ + DKV: DQ + 2 * DKV].set(DEEPNORM_BETA)
    w_in_dsa = nrm(ks[9], (N_DSA_LAYERS, D_MODEL, DSA_COLS), D_MODEL ** -0.5) * dsa_col_scale
    w_out_dsa = nrm(ks[10], (N_DSA_LAYERS, DQ, D_MODEL), DQ ** -0.5 * DEEPNORM_BETA)
    w_in_fox = nrm(ks[11], (N_FOX_LAYERS, D_MODEL, FOX_COLS), D_MODEL ** -0.5) * fox_col_scale
    b_forget = FORGET_BIAS_MEAN + nrm(ks[12], (N_FOX_LAYERS, N_HEADS), 0.5)
    w_out_fox = nrm(ks[13], (N_FOX_LAYERS, DQ, D_MODEL), DQ ** -0.5 * DEEPNORM_BETA)
    ln_gain = 1.0 + nrm(ks[14], (DEPTH, 2, D_MODEL), 0.01)
    ln_bias = nrm(ks[15], (DEPTH, 2, D_MODEL), 0.01)
    w_router = nrm(ks[16], (DEPTH, D_MODEL, N_EXPERTS), D_MODEL ** -0.5)
    b_router = nrm(ks[17], (DEPTH, N_EXPERTS), 0.01)
    w_gate_up = nrm(ks[18], (DEPTH, N_EXPERTS, D_MODEL, 2 * D_FF), D_MODEL ** -0.5 * DEEPNORM_BETA)
    b_gate_up = nrm(ks[19], (DEPTH, N_EXPERTS, 2 * D_FF), 0.01)
    w_down = nrm(ks[20], (DEPTH, N_EXPERTS, D_FF, D_MODEL), D_FF ** -0.5 * DEEPNORM_BETA)
    b_down = nrm(ks[21], (DEPTH, N_EXPERTS, D_MODEL), 0.01)
    return {'x_prompt': x_prompt, 'x_sample': x_sample,
            'cache_dsa_k': cache_dsa_k, 'cache_dsa_v': cache_dsa_v, 'cache_dsa_ik': cache_dsa_ik,
            'cache_fox_k': cache_fox_k, 'cache_fox_v': cache_fox_v, 'cache_fox_logf': cache_fox_logf,
            'page_table': page_table,
            'w_in_dsa': w_in_dsa, 'w_out_dsa': w_out_dsa,
            'w_in_fox': w_in_fox, 'b_forget': b_forget, 'w_out_fox': w_out_fox,
            'ln_gain': ln_gain, 'ln_bias': ln_bias,
            'w_router': w_router, 'b_router': b_router,
            'w_gate_up': w_gate_up, 'b_gate_up': b_gate_up, 'w_down': w_down, 'b_down': b_down}


def reference(x_prompt, x_sample, cache_dsa_k, cache_dsa_v, cache_dsa_ik, cache_fox_k, cache_fox_v,
              cache_fox_logf, page_table, w_in_dsa, w_out_dsa, w_in_fox, b_forget, w_out_fox,
              ln_gain, ln_bias, w_router, b_router, w_gate_up, b_gate_up, w_down, b_down):
    dk_p, dv_p, di_p, fk_p, fv_p, fl_p = [], [], [], [], [], []
    dk_s, dv_s, di_s, fk_s, fv_s, fl_s = [], [], [], [], [], []
    xp, xs = x_prompt, x_sample
    for i in range(DEPTH):
        j = i // 2
        if i % 2 == 0:
            mp, kp, vp, ap = _dsa_prompt(xp, w_in_dsa[j], w_out_dsa[j])
            ms, k_s, v_s, a_s = _dsa_sample(xs, cache_dsa_k, cache_dsa_v, cache_dsa_ik, page_table, j,
                                            w_in_dsa[j], w_out_dsa[j])
            dk_p.append(kp); dv_p.append(vp); di_p.append(ap)
            dk_s.append(k_s); dv_s.append(v_s); di_s.append(a_s)
        else:
            mp, kp, vp, ap = _fox_prompt(xp, w_in_fox[j], b_forget[j], w_out_fox[j])
            ms, k_s, v_s, a_s = _fox_sample(xs, cache_fox_k, cache_fox_v, cache_fox_logf, page_table, j,
                                            w_in_fox[j], b_forget[j], w_out_fox[j])
            fk_p.append(kp); fv_p.append(vp); fl_p.append(ap)
            fk_s.append(k_s); fv_s.append(v_s); fl_s.append(a_s)
        xp = _layer_norm(DEEPNORM_ALPHA * xp + mp, ln_gain[i, 0], ln_bias[i, 0])
        xs = _layer_norm(DEEPNORM_ALPHA * xs + ms, ln_gain[i, 0], ln_bias[i, 0])
        xp = _layer_norm(DEEPNORM_ALPHA * xp + _moe(xp, w_router[i], b_router[i], w_gate_up[i], b_gate_up[i], w_down[i], b_down[i]),
                         ln_gain[i, 1], ln_bias[i, 1])
        xs = _layer_norm(DEEPNORM_ALPHA * xs + _moe(xs, w_router[i], b_router[i], w_gate_up[i], b_gate_up[i], w_down[i], b_down[i]),
                         ln_gain[i, 1], ln_bias[i, 1])
    return (xp, xs,
            jnp.stack(dk_p), jnp.stack(dv_p), jnp.stack(di_p),
            jnp.stack(fk_p), jnp.stack(fv_p), jnp.stack(fl_p),
            jnp.stack(dk_s), jnp.stack(dv_s), jnp.stack(di_s),
            jnp.stack(fk_s), jnp.stack(fv_s), jnp.stack(fl_s))
```

```python
import functools

import jax
import jax.numpy as jnp
import numpy as np
from jax import lax
from jax.experimental import pallas as pl
from jax.experimental.pallas import tpu as pltpu

f32 = jnp.float32
bf16 = jnp.bfloat16
i32 = jnp.int32

D_MODEL = 1024
DEPTH = 4
PAGE = 128
N_HEADS = 16
HEAD_DIM = 64
N_KV = 4
GROUP = N_HEADS // N_KV
DQ = N_HEADS * HEAD_DIM
DKV = N_KV * HEAD_DIM
IDX_HEADS = 8
IDX_DIM = 64
TOPK = 256
N_EXPERTS = 32
TOP_E = 4
D_FF = D_MODEL
SWIGLU_LIMIT = 7.0
SWIGLU_ALPHA = 1.702
MOE_BLOCK = 128
LN_EPS = 1e-5
ALPHA = (2 * DEPTH) ** 0.25
IDX_SCALE = IDX_HEADS ** -0.5 * IDX_DIM ** -0.5
QK_SCALE = HEAD_DIM ** -0.5

LANES = 128
Q_TILE = 128
KEY_CHUNK = 256
ROW_TILE = 128
INT_MIN = -2 ** 31
NEG = -1e30
VMEM_LIMIT = 56 * 1024 * 1024

NT = (((1,), (1,)), ((), ()))


def _cparams(sem):
    return pltpu.CompilerParams(dimension_semantics=sem, vmem_limit_bytes=VMEM_LIMIT)


def _split3(x):
    hi = x.astype(bf16)
    r1 = x - hi.astype(f32)
    mid = r1.astype(bf16)
    lo = (r1 - mid.astype(f32)).astype(bf16)
    return hi, mid, lo


def _dot(a, b):
    return jnp.dot(a, b, preferred_element_type=f32)


def _dot3(x, w_bf):
    hi, mid, lo = _split3(x)
    return _dot(hi, w_bf) + _dot(mid, w_bf) + _dot(lo, w_bf)


def _layer_norm(y, g, b):
    mu = jnp.mean(y, axis=-1, keepdims=True)
    d = y - mu
    var = jnp.mean(d * d, axis=-1, keepdims=True)
    return d * lax.rsqrt(var + LN_EPS) * g + b


def _order_key(s):
    s = jnp.where(s == 0.0, 0.0, s)
    bits = pltpu.bitcast(s, i32)
    return bits ^ ((bits >> 31) & 0x7FFFFFFF)


def _upper_tri(n):
    r = lax.broadcasted_iota(i32, (n, n), 0)
    c = lax.broadcasted_iota(i32, (n, n), 1)
    return jnp.where(r <= c, 1.0, 0.0).astype(bf16)


def _inproj_dsa_kernel(x_ref, w_ref, q_ref, k_ref, v_ref, iq_ref, ik_ref, iw_ref):
    xb = x_ref[...].astype(bf16)
    q_ref[...] = _dot(xb, w_ref[:, 0:DQ])
    k_ref[...] = _dot(xb, w_ref[:, DQ:DQ + DKV])
    v_ref[...] = _dot(xb, w_ref[:, DQ + DKV:DQ + 2 * DKV])
    o = DQ + 2 * DKV
    iq_ref[...] = _dot(xb, w_ref[:, o:o + IDX_HEADS * IDX_DIM])
    o += IDX_HEADS * IDX_DIM
    tail = _dot(xb, w_ref[:, o:o + LANES])
    ik_ref[...] = tail[:, 0:IDX_DIM]
    iw_ref[...] = tail[:, IDX_DIM:IDX_DIM + IDX_HEADS]


def _inproj_dsa(x, w_pad):
    n = x.shape[0]
    cols = w_pad.shape[1]
    row = lambda c: pl.BlockSpec((ROW_TILE, c), lambda i: (i, 0))
    widths = (DQ, DKV, DKV, IDX_HEADS * IDX_DIM, IDX_DIM, IDX_HEADS)
    return pl.pallas_call(
        _inproj_dsa_kernel,
        grid=(n // ROW_TILE,),
        in_specs=[row(D_MODEL), pl.BlockSpec((D_MODEL, cols), lambda i: (0, 0))],
        out_specs=[row(c) for c in widths],
        out_shape=[jax.ShapeDtypeStruct((n, c), f32) for c in widths],
        compiler_params=_cparams(("arbitrary",)),
    )(x, w_pad)


def _inproj_fox_kernel(x_ref, w_ref, bf_ref, q_ref, k_ref, v_ref, lf_ref):
    xb = x_ref[...].astype(bf16)
    q_ref[...] = _dot(xb, w_ref[:, 0:DQ])
    k_ref[...] = _dot(xb, w_ref[:, DQ:DQ + DKV])
    v_ref[...] = _dot(xb, w_ref[:, DQ + DKV:DQ + 2 * DKV])
    o = DQ + 2 * DKV
    fg = _dot(xb, w_ref[:, o:o + LANES])[:, 0:N_HEADS] + bf_ref[...]
    lf_ref[...] = jax.nn.log_sigmoid(fg)


def _inproj_fox(x, w_pad, b_f):
    n = x.shape[0]
    cols = w_pad.shape[1]
    row = lambda c: pl.BlockSpec((ROW_TILE, c), lambda i: (i, 0))
    widths = (DQ, DKV, DKV, N_HEADS)
    return pl.pallas_call(
        _inproj_fox_kernel,
        grid=(n // ROW_TILE,),
        in_specs=[row(D_MODEL), pl.BlockSpec((D_MODEL, cols), lambda i: (0, 0)),
                  pl.BlockSpec((1, N_HEADS), lambda i: (0, 0))],
        out_specs=[row(c) for c in widths],
        out_shape=[jax.ShapeDtypeStruct((n, c), f32) for c in widths],
        compiler_params=_cparams(("arbitrary",)),
    )(x, w_pad, b_f.reshape(1, N_HEADS))


def _flash_head(qh, k_ref, v_ref, kv_head, n_chunks, bias_fn):
    def body(c, carry):
        m, l, acc = carry
        off = pl.multiple_of(c * KEY_CHUNK, KEY_CHUNK)
        kc = k_ref[kv_head, pl.ds(off, KEY_CHUNK), :]
        vc = v_ref[kv_head, pl.ds(off, KEY_CHUNK), :]
        lg = lax.dot_general(qh, kc, NT, preferred_element_type=f32) + bias_fn(c)
        m_new = jnp.maximum(m, jnp.max(lg, axis=1, keepdims=True))
        a = jnp.exp(m - m_new)
        p = jnp.exp(lg - m_new)
        l = a * l + jnp.sum(p, axis=1, keepdims=True)
        acc = a * acc + _dot(p.astype(bf16), vc)
        return m_new, l, acc

    init = (jnp.full((Q_TILE, 1), NEG, f32), jnp.zeros((Q_TILE, 1), f32),
            jnp.zeros((Q_TILE, HEAD_DIM), f32))
    _, l, acc = lax.fori_loop(0, n_chunks, body, init)
    return acc / l


def _stage_kv(k_ref, v_ref, kbf, vbf):
    for h in range(N_KV):
        kbf[h] = k_ref[:, h * HEAD_DIM:(h + 1) * HEAD_DIM].astype(bf16)
        vbf[h] = v_ref[:, h * HEAD_DIM:(h + 1) * HEAD_DIM].astype(bf16)


def _dsa_prompt_kernel(q_ref, iq_ref, iw_ref, ik_ref, k_ref, v_ref, o_ref,
                       ikbf, kbf, vbf, key3, mb3, dist3):
    i = pl.program_id(1)

    @pl.when(i == 0)
    def _():
        ikbf[...] = ik_ref[...].astype(bf16)
        _stage_kv(k_ref, v_ref, kbf, vbf)

    n_chunks = (i * Q_TILE + Q_TILE + KEY_CHUNK - 1) // KEY_CHUNK
    row = lax.broadcasted_iota(i32, (Q_TILE, KEY_CHUNK), 0)
    col = lax.broadcasted_iota(i32, (Q_TILE, KEY_CHUNK), 1)
    qpos = i * Q_TILE + row
    w = iw_ref[...] * IDX_SCALE
    iqb = iq_ref[...].astype(bf16)

    def score_body(c, _):
        ikc = ikbf[pl.ds(pl.multiple_of(c * KEY_CHUNK, KEY_CHUNK), KEY_CHUNK), :]
        s = jnp.zeros((Q_TILE, KEY_CHUNK), f32)
        for h in range(IDX_HEADS):
            rel = lax.dot_general(iqb[:, h * IDX_DIM:(h + 1) * IDX_DIM], ikc, NT,
                                  preferred_element_type=f32)
            s = s + w[:, h:h + 1] * jnp.maximum(rel, 0.0)
        d = qpos - (c * KEY_CHUNK + col)
        key3[c] = jnp.where(d >= 0, _order_key(s), INT_MIN)
        dist3[c] = d.astype(f32)
        return 0

    lax.fori_loop(0, n_chunks, score_body, 0)

    def count(pred):
        def body(c, acc):
            return acc + jnp.where(pred(key3[c]), 1.0, 0.0)
        acc = lax.fori_loop(0, n_chunks, body, jnp.zeros((Q_TILE, KEY_CHUNK), f32))
        return jnp.sum(acc, axis=1, keepdims=True)

    def bit_body(it, t):
        cand = t ^ lax.shift_left(jnp.int32(1), 31 - it)
        return jnp.where(count(lambda kk: kk >= cand) >= TOPK, cand, t)

    thr = lax.fori_loop(0, 32, bit_body, jnp.full((Q_TILE, 1), INT_MIN, i32))
    thr = jnp.maximum(thr, INT_MIN + 1)
    need = TOPK - count(lambda kk: kk > thr)
    n_tie = count(lambda kk: kk == thr)
    has_excess = jnp.max(n_tie - need) > 0.0

    @pl.when(jnp.logical_not(has_excess))
    def _():
        def body(c, _):
            mb3[c] = jnp.where(key3[c] >= thr, 0.0, NEG)
            return 0
        lax.fori_loop(0, n_chunks, body, 0)

    @pl.when(has_excess)
    def _():
        tri = _upper_tri(KEY_CHUNK)

        def body(c, before):
            kk = key3[c]
            tie = jnp.where(kk == thr, 1.0, 0.0)
            rank = before + _dot(tie.astype(bf16), tri)
            sel = (kk > thr) | ((kk == thr) & (rank <= need))
            mb3[c] = jnp.where(sel, 0.0, NEG)
            return before + jnp.sum(tie, axis=1, keepdims=True)
        lax.fori_loop(0, n_chunks, body, jnp.zeros((Q_TILE, 1), f32))

    for h in range(N_HEADS):
        slope = float(2.0 ** (-8.0 * (h + 1) / N_HEADS))
        qh = (q_ref[:, h * HEAD_DIM:(h + 1) * HEAD_DIM] * QK_SCALE).astype(bf16)
        out = _flash_head(qh, kbf, vbf, h // GROUP, n_chunks,
                          lambda c: mb3[c] - slope * dist3[c])
        o_ref[:, h * HEAD_DIM:(h + 1) * HEAD_DIM] = out


def _dsa_prompt(q, iq, iw, ik, k, v, batch, seq):
    nq = seq // Q_TILE
    nc = seq // KEY_CHUNK
    qspec = lambda c: pl.BlockSpec((Q_TILE, c), lambda b, i: (b * nq + i, 0))
    bspec = lambda c: pl.BlockSpec((seq, c), lambda b, i: (b, 0))
    return pl.pallas_call(
        _dsa_prompt_kernel,
        grid=(batch, nq),
        in_specs=[qspec(DQ), qspec(IDX_HEADS * IDX_DIM), qspec(IDX_HEADS),
                  bspec(IDX_DIM), bspec(DKV), bspec(DKV)],
        out_specs=qspec(DQ),
        out_shape=jax.ShapeDtypeStruct((batch * seq, DQ), f32),
        scratch_shapes=[pltpu.VMEM((seq, IDX_DIM), bf16),
                        pltpu.VMEM((N_KV, seq, HEAD_DIM), bf16),
                        pltpu.VMEM((N_KV, seq, HEAD_DIM), bf16),
                        pltpu.VMEM((nc, Q_TILE, KEY_CHUNK), i32),
                        pltpu.VMEM((nc, Q_TILE, KEY_CHUNK), f32),
                        pltpu.VMEM((nc, Q_TILE, KEY_CHUNK), f32)],
        compiler_params=_cparams(("arbitrary", "arbitrary")),
    )(q, iq, iw, ik, k, v)


def _cum_kernel(lf_ref, o_ref):
    tri = _upper_tri(KEY_CHUNK)
    carry = jnp.zeros((N_HEADS, 1), f32)
    for c in range(lf_ref.shape[1] // KEY_CHUNK):
        cum = _dot3(lf_ref[:, c * KEY_CHUNK:(c + 1) * KEY_CHUNK], tri) + carry
        o_ref[0, c] = cum
        carry = cum[:, KEY_CHUNK - 1:KEY_CHUNK]


def _fox_cum(lf_t, batch, seq):
    nc = seq // KEY_CHUNK
    return pl.pallas_call(
        _cum_kernel,
        grid=(batch,),
        in_specs=[pl.BlockSpec((N_HEADS, seq), lambda b: (0, b))],
        out_specs=pl.BlockSpec((1, nc, N_HEADS, KEY_CHUNK), lambda b: (b, 0, 0, 0)),
        out_shape=jax.ShapeDtypeStruct((batch, nc, N_HEADS, KEY_CHUNK), f32),
        compiler_params=_cparams(("arbitrary",)),
    )(lf_t)


def _fox_prompt_kernel(q_ref, cum_ref, k_ref, v_ref, o_ref, kbf, vbf):
    i = pl.program_id(1)

    @pl.when(i == 0)
    def _():
        _stage_kv(k_ref, v_ref, kbf, vbf)

    n_chunks = (i * Q_TILE + Q_TILE + KEY_CHUNK - 1) // KEY_CHUNK
    row = lax.broadcasted_iota(i32, (Q_TILE, KEY_CHUNK), 0)
    col = lax.broadcasted_iota(i32, (Q_TILE, KEY_CHUNK), 1)
    rel = i * Q_TILE + row - col

    for h in range(N_HEADS):
        qh = (q_ref[:, h * HEAD_DIM:(h + 1) * HEAD_DIM] * QK_SCALE).astype(bf16)

        def bias(c, h=h):
            ck = cum_ref[0, c][h:h + 1, :]
            return jnp.where(rel - c * KEY_CHUNK >= 0, -ck, NEG)

        o_ref[:, h * HEAD_DIM:(h + 1) * HEAD_DIM] = _flash_head(qh, kbf, vbf, h // GROUP, n_chunks, bias)


def _fox_prompt(q, cum, k, v, batch, seq):
    nq = seq // Q_TILE
    nc = seq // KEY_CHUNK
    qspec = lambda c: pl.BlockSpec((Q_TILE, c), lambda b, i: (b * nq + i, 0))
    bspec = lambda c: pl.BlockSpec((seq, c), lambda b, i: (b, 0))
    return pl.pallas_call(
        _fox_prompt_kernel,
        grid=(batch, nq),
        in_specs=[qspec(DQ), pl.BlockSpec((1, nc, N_HEADS, KEY_CHUNK), lambda b, i: (b, 0, 0, 0)),
                  bspec(DKV), bspec(DKV)],
        out_specs=qspec(DQ),
        out_shape=jax.ShapeDtypeStruct((batch * seq, DQ), f32),
        scratch_shapes=[pltpu.VMEM((N_KV, seq, HEAD_DIM), bf16),
                        pltpu.VMEM((N_KV, seq, HEAD_DIM), bf16)],
        compiler_params=_cparams(("arbitrary", "arbitrary")),
    )(q, cum, k, v)


SROWS = 64
SQ = 8


def _dsa_sample_score_kernel(pt_ref, iq_ref, w_ref, ikc_ref, ikn_ref, key_ref, *, n_pages, dec_seq):
    p = pl.program_id(1)
    ik = jnp.where(p == n_pages, ikn_ref[0], ikc_ref[0, 0]).astype(bf16)
    rel = lax.dot_general(iq_ref[0].astype(bf16), ik, NT, preferred_element_type=f32)
    s = jnp.sum((jnp.maximum(rel, 0.0) * (w_ref[0] * IDX_SCALE)).reshape(SQ, IDX_HEADS, PAGE), axis=1)
    t = lax.broadcasted_iota(i32, (SQ, PAGE), 0)
    j = lax.broadcasted_iota(i32, (SQ, PAGE), 1)
    ok = (t < dec_seq) & ((p < n_pages) | (j <= t))
    key_ref[0, 0] = jnp.where(ok, _order_key(s), INT_MIN)


def _dsa_sample_scores(pt, iq_rows, w_rows, cache_ik, ik_new, layer, n_pages, dec_seq):
    nb = iq_rows.shape[0]
    kern = functools.partial(_dsa_sample_score_kernel, n_pages=n_pages, dec_seq=dec_seq)
    page = lambda b, p, pt: (layer, pt[b * n_pages + jnp.minimum(p, n_pages - 1)], 0, 0)
    return pl.pallas_call(
        kern,
        grid_spec=pltpu.PrefetchScalarGridSpec(
            num_scalar_prefetch=1,
            grid=(nb, n_pages + 1),
            in_specs=[pl.BlockSpec((1, SQ * IDX_HEADS, IDX_DIM), lambda b, p, pt: (b, 0, 0)),
                      pl.BlockSpec((1, SQ * IDX_HEADS, 1), lambda b, p, pt: (b, 0, 0)),
                      pl.BlockSpec((1, 1, PAGE, IDX_DIM), page),
                      pl.BlockSpec((1, PAGE, IDX_DIM), lambda b, p, pt: (b, 0, 0))],
            out_specs=pl.BlockSpec((1, 1, SQ, PAGE), lambda b, p, pt: (b, p, 0, 0)),
        ),
        out_shape=jax.ShapeDtypeStruct((nb, n_pages + 1, SQ, PAGE), i32),
        compiler_params=_cparams(("arbitrary", "arbitrary")),
    )(pt, iq_rows, w_rows, cache_ik, ik_new)


def _sample_attn_kernel(pt_ref, q_ref, kc_ref, vc_ref, kn_ref, vn_ref, aux_c_ref, aux_n_ref, slope_ref,
                        o_ref, m_sc, l_sc, acc_sc, carry_sc, thr_sc, need_sc,
                        *, mode, n_pages, dec_seq):
    p = pl.program_id(1)
    last = p == n_pages
    row = lax.broadcasted_iota(i32, (SROWS, PAGE), 0)
    col = lax.broadcasted_iota(i32, (SROWS, PAGE), 1)
    tok = row // N_HEADS

    @pl.when(p == 0)
    def _():
        m_sc[...] = jnp.full(m_sc.shape, NEG, f32)
        l_sc[...] = jnp.zeros(l_sc.shape, f32)
        acc_sc[...] = jnp.zeros(acc_sc.shape, f32)
        carry_sc[...] = jnp.zeros(carry_sc.shape, f32)

    if mode == "dsa":
        keys_all = aux_c_ref

        @pl.when(p == 0)
        def _():
            kk = keys_all[0]

            def count(pred):
                return jnp.sum(jnp.sum(jnp.where(pred(kk), 1.0, 0.0), axis=0), axis=1, keepdims=True)

            def bit_body(it, t):
                cand = t ^ lax.shift_left(jnp.int32(1), 31 - it)
                return jnp.where(count(lambda a: a >= cand[None]) >= TOPK, cand, t)

            thr = lax.fori_loop(0, 32, bit_body, jnp.full((SQ, 1), INT_MIN, i32))
            thr = jnp.maximum(thr, INT_MIN + 1)
            thr_sc[...] = thr
            need_sc[...] = TOPK - count(lambda a: a > thr[None])

        kp = keys_all[0, p]
        thr = thr_sc[...]
        tie = jnp.where(kp == thr, 1.0, 0.0)
        rank = carry_sc[0:SQ, :] + _dot(tie.astype(bf16), _upper_tri(PAGE))
        sel = jnp.where((kp > thr) | ((kp == thr) & (rank <= need_sc[...])), 1.0, 0.0)
        carry_sc[0:SQ, :] = carry_sc[0:SQ, :] + jnp.sum(tie, axis=1, keepdims=True)
        sel_rows = jnp.concatenate(
            [jnp.broadcast_to(sel[t:t + 1, :], (N_HEADS, PAGE)) for t in range(dec_seq)], axis=0)
        dist = (n_pages * PAGE + tok - (p * PAGE + col)).astype(f32)
        bias = jnp.where(sel_rows > 0.5, -slope_ref[...] * dist, NEG)
    else:
        lf_t = jnp.where(last, aux_n_ref[0], aux_c_ref[0, 0])
        cum = carry_sc[...] + _dot3(lf_t, _upper_tri(PAGE))
        carry_sc[...] = cum[:, PAGE - 1:PAGE]
        ck = jnp.concatenate([cum] * dec_seq, axis=0)
        bias = jnp.where(jnp.logical_not(last) | (col <= tok), -ck, NEG)

    k = jnp.where(last, kn_ref[0], kc_ref[0, 0]).astype(bf16)
    v = jnp.where(last, vn_ref[0], vc_ref[0, 0]).astype(bf16)
    qb = (q_ref[0] * QK_SCALE).astype(bf16)
    lg = lax.dot_general(qb, k, NT, preferred_element_type=f32) + bias
    m = m_sc[...]
    m_new = jnp.maximum(m, jnp.max(lg, axis=1, keepdims=True))
    a = jnp.exp(m - m_new)
    pr = jnp.exp(lg - m_new)
    l_sc[...] = a * l_sc[...] + jnp.sum(pr, axis=1, keepdims=True)
    acc_sc[...] = a * acc_sc[...] + _dot(pr.astype(bf16), v)
    m_sc[...] = m_new

    @pl.when(last)
    def _():
        o_ref[0] = acc_sc[...] / l_sc[...]


def _sample_attn(mode, pt, q_rows, cache_k, cache_v, k_new, v_new, aux_c, aux_n, slopes,
                 layer, n_pages, dec_seq):
    nb = q_rows.shape[0]
    kern = functools.partial(_sample_attn_kernel, mode=mode, n_pages=n_pages, dec_seq=dec_seq)
    page = lambda b, p, pt: (layer, pt[b * n_pages + jnp.minimum(p, n_pages - 1)], 0, 0)
    per_b = lambda b, p, pt: (b, 0, 0)
    if mode == "dsa":
        aux_c_spec = pl.BlockSpec((1, n_pages + 1, SQ, PAGE), lambda b, p, pt: (b, 0, 0, 0))
        aux_n_spec = pl.BlockSpec((1, SQ, PAGE), per_b)
    else:
        aux_c_spec = pl.BlockSpec((1, 1, N_HEADS, PAGE), page)
        aux_n_spec = pl.BlockSpec((1, N_HEADS, PAGE), per_b)
    return pl.pallas_call(
        kern,
        grid_spec=pltpu.PrefetchScalarGridSpec(
            num_scalar_prefetch=1,
            grid=(nb, n_pages + 1),
            in_specs=[pl.BlockSpec((1, SROWS, DKV), per_b),
                      pl.BlockSpec((1, 1, PAGE, DKV), page),
                      pl.BlockSpec((1, 1, PAGE, DKV), page),
                      pl.BlockSpec((1, PAGE, DKV), per_b),
                      pl.BlockSpec((1, PAGE, DKV), per_b),
                      aux_c_spec, aux_n_spec,
                      pl.BlockSpec((SROWS, 1), lambda b, p, pt: (0, 0))],
            out_specs=pl.BlockSpec((1, SROWS, DKV), per_b),
            scratch_shapes=[pltpu.VMEM((SROWS, 1), f32), pltpu.VMEM((SROWS, 1), f32),
                            pltpu.VMEM((SROWS, DKV), f32), pltpu.VMEM((N_HEADS, 1), f32),
                            pltpu.VMEM((SQ, 1), i32), pltpu.VMEM((SQ, 1), f32)],
        ),
        out_shape=jax.ShapeDtypeStruct((nb, SROWS, DKV), f32),
        compiler_params=_cparams(("arbitrary", "arbitrary")),
    )(pt, q_rows, cache_k, cache_v, k_new, v_new, aux_c, aux_n, slopes)


def _outproj_kernel(x_ref, op_ref, os_ref, w_ref, g_ref, b_ref, wr_hi_ref, wr_lo_ref, br_ref,
                    x1_ref, x1b_ref, exp_ref, gate_ref, *, n_prompt_tiles):
    i = pl.program_id(0)
    o = jnp.where(i < n_prompt_tiles, op_ref[...], os_ref[...]).astype(bf16)
    x1 = _layer_norm(ALPHA * x_ref[...] + _dot(o, w_ref[...]), g_ref[...], b_ref[...])
    x1_ref[...] = x1
    x1b_ref[...] = x1.astype(bf16)

    hi = x1.astype(bf16)
    lo = (x1 - hi.astype(f32)).astype(bf16)
    lg = _dot(hi, wr_hi_ref[...]) + (_dot(lo, wr_hi_ref[...]) + _dot(hi, wr_lo_ref[...])) + br_ref[...]
    lane = lax.broadcasted_iota(i32, lg.shape, 1)
    out_lane = lax.broadcasted_iota(i32, (ROW_TILE, LANES), 1)
    vals = []
    experts = jnp.zeros((ROW_TILE, LANES), i32)
    for k in range(TOP_E):
        m = jnp.max(lg, axis=1, keepdims=True)
        idx = jnp.min(jnp.where(lg == m, lane, N_EXPERTS), axis=1, keepdims=True)
        vals.append(m)
        experts = jnp.where(out_lane == k, idx, experts)
        lg = jnp.where(lane == idx, -jnp.inf, lg)
    es = [jnp.exp(v - vals[0]) for v in vals]
    tot = es[0] + es[1] + es[2] + es[3]
    gates = jnp.zeros((ROW_TILE, LANES), f32)
    for k in range(TOP_E):
        gates = jnp.where(out_lane == k, es[k] / tot, gates)
    exp_ref[...] = experts
    gate_ref[...] = gates


def _outproj(x, o_prompt, o_sample, w_bf, g, b, wr_hi, wr_lo, br):
    n = x.shape[0]
    npt = o_prompt.shape[0] // ROW_TILE
    nst = o_sample.shape[0] // ROW_TILE
    row = lambda c: pl.BlockSpec((ROW_TILE, c), lambda i: (i, 0))
    full = lambda r, c: pl.BlockSpec((r, c), lambda i: (0, 0))
    kern = functools.partial(_outproj_kernel, n_prompt_tiles=npt)
    return pl.pallas_call(
        kern,
        grid=(n // ROW_TILE,),
        in_specs=[row(D_MODEL),
                  pl.BlockSpec((ROW_TILE, DQ), lambda i: (jnp.minimum(i, npt - 1), 0)),
                  pl.BlockSpec((ROW_TILE, DQ), lambda i: (jnp.clip(i - npt, 0, nst - 1), 0)),
                  full(DQ, D_MODEL), full(1, D_MODEL), full(1, D_MODEL),
                  full(D_MODEL, N_EXPERTS), full(D_MODEL, N_EXPERTS), full(1, N_EXPERTS)],
        out_specs=[row(D_MODEL), row(D_MODEL), row(LANES), row(LANES)],
        out_shape=[jax.ShapeDtypeStruct((n, D_MODEL), f32), jax.ShapeDtypeStruct((n, D_MODEL), bf16),
                   jax.ShapeDtypeStruct((n, LANES), i32), jax.ShapeDtypeStruct((n, LANES), f32)],
        compiler_params=_cparams(("arbitrary",)),
    )(x, o_prompt, o_sample, w_bf, g, b, wr_hi, wr_lo, br)


def _moe_kernel(bexp_ref, x_ref, wgu_ref, bgu_ref, wdn_ref, bdn_ref, y_ref, wgu_bf, wdn_bf):
    i = pl.program_id(0)
    e = bexp_ref[i]
    prev = bexp_ref[jnp.maximum(i - 1, 0)]

    @pl.when((i == 0) | (e != prev))
    def _():
        wgu_bf[...] = wgu_ref[0, 0].astype(bf16)
        wdn_bf[...] = wdn_ref[0, 0].astype(bf16)

    h = _dot(x_ref[...], wgu_bf[...]) + bgu_ref[0, 0]
    gate = jnp.minimum(h[:, 0:D_FF], SWIGLU_LIMIT)
    up = jnp.clip(h[:, D_FF:2 * D_FF], -SWIGLU_LIMIT, SWIGLU_LIMIT)
    glu = gate * jax.nn.sigmoid(SWIGLU_ALPHA * gate)
    act = ((up + 1.0) * glu).astype(bf16)
    y_ref[...] = _dot(act, wdn_bf[...]) + bdn_ref[0, 0]


def _moe(block_exp, x_rows, w_gu, b_gu, w_dn, b_dn, layer):
    n_blocks = block_exp.shape[0]
    wspec = lambda r, c: pl.BlockSpec((1, 1, r, c), lambda i, be: (layer, be[i], 0, 0))
    return pl.pallas_call(
        _moe_kernel,
        grid_spec=pltpu.PrefetchScalarGridSpec(
            num_scalar_prefetch=1,
            grid=(n_blocks,),
            in_specs=[pl.BlockSpec((MOE_BLOCK, D_MODEL), lambda i, be: (i, 0)),
                      wspec(D_MODEL, 2 * D_FF), wspec(1, 2 * D_FF),
                      wspec(D_FF, D_MODEL), wspec(1, D_MODEL)],
            out_specs=pl.BlockSpec((MOE_BLOCK, D_MODEL), lambda i, be: (i, 0)),
            scratch_shapes=[pltpu.VMEM((D_MODEL, 2 * D_FF), bf16), pltpu.VMEM((D_FF, D_MODEL), bf16)],
        ),
        out_shape=jax.ShapeDtypeStruct((n_blocks * MOE_BLOCK, D_MODEL), f32),
        compiler_params=_cparams(("arbitrary",)),
    )(block_exp, x_rows, w_gu, b_gu, w_dn, b_dn)


def _combine_kernel(x_ref, y_ref, gate_ref, g_ref, b_ref, o_ref):
    gates = gate_ref[...]
    mix = gates[:, 0:1] * y_ref[:, 0:D_MODEL]
    for k in range(1, TOP_E):
        mix = mix + gates[:, k:k + 1] * y_ref[:, k * D_MODEL:(k + 1) * D_MODEL]
    o_ref[...] = _layer_norm(ALPHA * x_ref[...] + mix, g_ref[...], b_ref[...])


def _combine(x1, y_asg, gates, g, b):
    n = x1.shape[0]
    row = lambda c: pl.BlockSpec((ROW_TILE, c), lambda i: (i, 0))
    full = lambda r, c: pl.BlockSpec((r, c), lambda i: (0, 0))
    return pl.pallas_call(
        _combine_kernel,
        grid=(n // ROW_TILE,),
        in_specs=[row(D_MODEL), row(TOP_E * D_MODEL), row(LANES), full(1, D_MODEL), full(1, D_MODEL)],
        out_specs=row(D_MODEL),
        out_shape=jax.ShapeDtypeStruct((n, D_MODEL), f32),
        compiler_params=_cparams(("arbitrary",)),
    )(x1, y_asg, gates, g, b)


def _route(top_exp, n_tok):
    n_asg = n_tok * TOP_E
    flat_e = top_exp.reshape(-1)
    order = jnp.argsort(flat_e).astype(i32)
    sorted_e = flat_e[order]
    counts = jnp.zeros((N_EXPERTS,), i32).at[flat_e].add(1)
    padded = (counts + MOE_BLOCK - 1) // MOE_BLOCK * MOE_BLOCK
    pad_end = jnp.cumsum(padded)
    pad_start = pad_end - padded
    start = jnp.cumsum(counts) - counts
    dest = pad_start[sorted_e] + jnp.arange(n_asg, dtype=i32) - start[sorted_e]
    n_blocks = -(-n_asg // MOE_BLOCK) + N_EXPERTS
    row_tok = jnp.zeros((n_blocks * MOE_BLOCK,), i32).at[dest].set(order // TOP_E)
    block_exp = jnp.minimum(
        jnp.searchsorted(pad_end, jnp.arange(n_blocks, dtype=i32) * MOE_BLOCK, side='right'),
        N_EXPERTS - 1).astype(i32)
    dest_of_asg = jnp.zeros((n_asg,), i32).at[order].set(dest)
    return row_tok, block_exp, dest_of_asg


def _pad_cols(w):
    cols = w.shape[1]
    padded = -(-cols // LANES) * LANES
    return jnp.pad(w, ((0, 0), (0, padded - cols))).astype(bf16)


def _diag_heads(o_full, nb, dec_seq):
    o6 = o_full.reshape(nb, dec_seq, N_KV, GROUP, N_KV, HEAD_DIM)
    return jnp.einsum('btkgkd->btkgd', o6).reshape(nb * dec_seq, DQ)


def _block_diag_q(q_s, nb, dec_seq):
    q5 = q_s.reshape(nb, dec_seq, N_KV, GROUP, 1, HEAD_DIM)
    eye = jnp.eye(N_KV, dtype=q_s.dtype).reshape(1, 1, N_KV, 1, N_KV, 1)
    return (q5 * eye).reshape(nb, SROWS, DKV)


def _pad_page(a, nb, dec_seq):
    a3 = a.reshape(nb, dec_seq, a.shape[-1])
    return jnp.pad(a3, ((0, 0), (0, PAGE - dec_seq), (0, 0)))


def kernel(x_prompt, x_sample, cache_dsa_k, cache_dsa_v, cache_dsa_ik, cache_fox_k, cache_fox_v,
           cache_fox_logf, page_table, w_in_dsa, w_out_dsa, w_in_fox, b_forget, w_out_fox,
           ln_gain, ln_bias, w_router, b_router, w_gate_up, b_gate_up, w_down, b_down):
    batch, seq, _ = x_prompt.shape
    nb, dec_seq, _ = x_sample.shape
    n_pages = page_table.shape[1]
    pool = cache_dsa_k.shape[1]
    n_p = batch * seq
    n_s = nb * dec_seq
    n_tok = n_p + n_s
    assert dec_seq * N_HEADS == SROWS and n_s == ROW_TILE and n_p % ROW_TILE == 0

    x = jnp.concatenate([x_prompt.reshape(n_p, D_MODEL), x_sample.reshape(n_s, D_MODEL)], axis=0)
    pt = page_table.reshape(-1).astype(i32)
    slopes = 2.0 ** (-8.0 * jnp.arange(1, N_HEADS + 1, dtype=f32) / N_HEADS)
    slope_rows = jnp.tile(slopes, dec_seq).reshape(SROWS, 1)

    dsa_k = cache_dsa_k.reshape(-1, pool, PAGE, DKV)
    dsa_v = cache_dsa_v.reshape(-1, pool, PAGE, DKV)
    fox_k = cache_fox_k.reshape(-1, pool, PAGE, DKV)
    fox_v = cache_fox_v.reshape(-1, pool, PAGE, DKV)
    fox_lf_t = jnp.swapaxes(cache_fox_logf, 2, 3)
    b_gu4 = b_gate_up.reshape(DEPTH, N_EXPERTS, 1, 2 * D_FF)
    b_dn4 = b_down.reshape(DEPTH, N_EXPERTS, 1, D_MODEL)

    outs = {name: [] for name in ("dk", "dv", "di", "fk", "fv", "fl")}
    for layer in range(DEPTH):
        j = layer // 2
        if layer % 2 == 0:
            q, k, v, iq, ik, iw = _inproj_dsa(x, _pad_cols(w_in_dsa[j]))
            o_p = _dsa_prompt(q, iq, iw, ik, k, v, batch, seq)
            iq_rows = jnp.pad(iq[n_p:].reshape(nb, dec_seq * IDX_HEADS, IDX_DIM),
                              ((0, 0), (0, (SQ - dec_seq) * IDX_HEADS), (0, 0)))
            w_rows = jnp.pad(iw[n_p:].reshape(nb, dec_seq * IDX_HEADS, 1),
                             ((0, 0), (0, (SQ - dec_seq) * IDX_HEADS), (0, 0)))
            keys = _dsa_sample_scores(pt, iq_rows, w_rows, cache_dsa_ik, _pad_page(ik[n_p:], nb, dec_seq),
                                      j, n_pages, dec_seq)
            o_s = _sample_attn("dsa", pt, _block_diag_q(q[n_p:], nb, dec_seq), dsa_k, dsa_v,
                               _pad_page(k[n_p:], nb, dec_seq), _pad_page(v[n_p:], nb, dec_seq),
                               keys, jnp.zeros((nb, SQ, PAGE), f32), slope_rows, j, n_pages, dec_seq)
            w_out = w_out_dsa[j]
            outs["dk"].append(k); outs["dv"].append(v); outs["di"].append(ik)
        else:
            q, k, v, lf = _inproj_fox(x, _pad_cols(w_in_fox[j]), b_forget[j])
            cum = _fox_cum(lf[:n_p].T, batch, seq)
            o_p = _fox_prompt(q, cum, k, v, batch, seq)
            lf_new_t = jnp.swapaxes(_pad_page(lf[n_p:], nb, dec_seq), 1, 2)
            o_s = _sample_attn("fox", pt, _block_diag_q(q[n_p:], nb, dec_seq), fox_k, fox_v,
                               _pad_page(k[n_p:], nb, dec_seq), _pad_page(v[n_p:], nb, dec_seq),
                               fox_lf_t, lf_new_t, slope_rows, j, n_pages, dec_seq)
            w_out = w_out_fox[j]
            outs["fk"].append(k); outs["fv"].append(v); outs["fl"].append(lf)

        wr = w_router[layer]
        wr_hi = wr.astype(bf16)
        wr_lo = (wr - wr_hi.astype(f32)).astype(bf16)
        x1, x1b, top_exp, gates = _outproj(
            x, o_p, _diag_heads(o_s, nb, dec_seq), w_out.astype(bf16),
            ln_gain[layer, 0].reshape(1, D_MODEL), ln_bias[layer, 0].reshape(1, D_MODEL),
            wr_hi, wr_lo, b_router[layer].reshape(1, N_EXPERTS))

        row_tok, block_exp, dest_of_asg = _route(top_exp[:, :TOP_E], n_tok)
        y_rows = _moe(block_exp, x1b[row_tok], w_gate_up, b_gu4, w_down, b_dn4, layer)
        y_asg = y_rows[dest_of_asg].reshape(n_tok, TOP_E * D_MODEL)
        x = _combine(x1, y_asg, gates, ln_gain[layer, 1].reshape(1, D_MODEL),
                     ln_bias[layer, 1].reshape(1, D_MODEL))

    def stack(name, shape_p, shape_s):
        a = jnp.stack(outs[name])
        return a[:, :n_p].reshape((-1,) + shape_p), a[:, n_p:].reshape((-1,) + shape_s)

    kv_p, kv_s = (batch, seq, N_KV, HEAD_DIM), (nb, dec_seq, N_KV, HEAD_DIM)
    dk_p, dk_s = stack("dk", kv_p, kv_s)
    dv_p, dv_s = stack("dv", kv_p, kv_s)
    di_p, di_s = stack("di", (batch, seq, IDX_DIM), (nb, dec_seq, IDX_DIM))
    fk_p, fk_s = stack("fk", kv_p, kv_s)
    fv_p, fv_s = stack("fv", kv_p, kv_s)
    fl_p, fl_s = stack("fl", (batch, seq, N_HEADS), (nb, dec_seq, N_HEADS))
    return (x[:n_p].reshape(batch, seq, D_MODEL), x[n_p:].reshape(nb, dec_seq, D_MODEL),
            dk_p, dv_p, di_p, fk_p, fv_p, fl_p, dk_s, dv_s, di_s, fk_s, fv_s, fl_s)
```

```python
import functools

import jax
import jax.numpy as jnp
from jax import lax
from jax.experimental import pallas as pl
from jax.experimental.pallas import tpu as pltpu

f32 = jnp.float32
bf16 = jnp.bfloat16
i32 = jnp.int32

D_MODEL = 1024
DEPTH = 4
PAGE = 128
N_HEADS = 16
HEAD_DIM = 64
N_KV = 4
GROUP = N_HEADS // N_KV
DQ = N_HEADS * HEAD_DIM
DKV = N_KV * HEAD_DIM
IDX_HEADS = 8
IDX_DIM = 64
TOPK = 256
N_EXPERTS = 32
TOP_E = 4
D_FF = D_MODEL
SWIGLU_LIMIT = 7.0
SWIGLU_ALPHA = 1.702
MOE_BLOCK = 128
LN_EPS = 1e-5
ALPHA = (2 * DEPTH) ** 0.25
IDX_SCALE = IDX_HEADS ** -0.5 * IDX_DIM ** -0.5
QK_SCALE = HEAD_DIM ** -0.5

LANES = 128
SUBLANES = 8
Q_TILE = 128
KEY_CHUNK = 256
ROW_TILE = 128
PAGES_PER_STEP = 16
INT_MIN = -2 ** 31
NEG = -1e30
VMEM_LIMIT = 56 * 1024 * 1024

NT = (((1,), (1,)), ((), ()))


def _cparams(sem):
    return pltpu.CompilerParams(dimension_semantics=sem, vmem_limit_bytes=VMEM_LIMIT)


def _split3(x):
    hi = x.astype(bf16)
    r1 = x - hi.astype(f32)
    mid = r1.astype(bf16)
    lo = (r1 - mid.astype(f32)).astype(bf16)
    return hi, mid, lo


def _dot(a, b):
    return jnp.dot(a, b, preferred_element_type=f32)


def _dot_nt(a, b):
    return lax.dot_general(a, b, NT, preferred_element_type=f32)


def _dot3(x, w_bf):
    hi, mid, lo = _split3(x)
    return _dot(hi, w_bf) + _dot(mid, w_bf) + _dot(lo, w_bf)


def _layer_norm(y, g, b):
    mu = jnp.mean(y, axis=-1, keepdims=True)
    d = y - mu
    var = jnp.mean(d * d, axis=-1, keepdims=True)
    return d * lax.rsqrt(var + LN_EPS) * g + b


def _order_key(s):
    s = jnp.where(s == 0.0, 0.0, s)
    bits = pltpu.bitcast(s, i32)
    return bits ^ ((bits >> 31) & 0x7FFFFFFF)


def _upper_tri(n, strict=False):
    r = lax.broadcasted_iota(i32, (n, n), 0)
    c = lax.broadcasted_iota(i32, (n, n), 1)
    return jnp.where((r < c) if strict else (r <= c), 1.0, 0.0).astype(bf16)


def _inproj_dsa_kernel(x_ref, w_ref, q_ref, k_ref, v_ref, iq_ref, ik_ref, iw_ref):
    xb = x_ref[...].astype(bf16)
    q_ref[...] = _dot(xb, w_ref[:, 0:DQ])
    k_ref[...] = _dot(xb, w_ref[:, DQ:DQ + DKV])
    v_ref[...] = _dot(xb, w_ref[:, DQ + DKV:DQ + 2 * DKV])
    o = DQ + 2 * DKV
    iq_ref[...] = _dot(xb, w_ref[:, o:o + IDX_HEADS * IDX_DIM])
    o += IDX_HEADS * IDX_DIM
    tail = _dot(xb, w_ref[:, o:o + LANES])
    ik_ref[...] = tail[:, 0:IDX_DIM]
    iw_ref[...] = tail[:, IDX_DIM:IDX_DIM + IDX_HEADS]


def _inproj_dsa(x, w_pad):
    n = x.shape[0]
    cols = w_pad.shape[1]
    row = lambda c: pl.BlockSpec((ROW_TILE, c), lambda i: (i, 0))
    widths = (DQ, DKV, DKV, IDX_HEADS * IDX_DIM, IDX_DIM, IDX_HEADS)
    return pl.pallas_call(
        _inproj_dsa_kernel,
        grid=(n // ROW_TILE,),
        in_specs=[row(D_MODEL), pl.BlockSpec((D_MODEL, cols), lambda i: (0, 0))],
        out_specs=[row(c) for c in widths],
        out_shape=[jax.ShapeDtypeStruct((n, c), f32) for c in widths],
        compiler_params=_cparams(("arbitrary",)),
        name="inproj_dsa",
    )(x, w_pad)


def _inproj_fox_kernel(x_ref, w_ref, bf_ref, q_ref, k_ref, v_ref, lf_ref):
    xb = x_ref[...].astype(bf16)
    q_ref[...] = _dot(xb, w_ref[:, 0:DQ])
    k_ref[...] = _dot(xb, w_ref[:, DQ:DQ + DKV])
    v_ref[...] = _dot(xb, w_ref[:, DQ + DKV:DQ + 2 * DKV])
    o = DQ + 2 * DKV
    fg = _dot(xb, w_ref[:, o:o + LANES])[:, 0:N_HEADS] + bf_ref[...]
    lf_ref[...] = jax.nn.log_sigmoid(fg)


def _inproj_fox(x, w_pad, b_f):
    n = x.shape[0]
    cols = w_pad.shape[1]
    row = lambda c: pl.BlockSpec((ROW_TILE, c), lambda i: (i, 0))
    widths = (DQ, DKV, DKV, N_HEADS)
    return pl.pallas_call(
        _inproj_fox_kernel,
        grid=(n // ROW_TILE,),
        in_specs=[row(D_MODEL), pl.BlockSpec((D_MODEL, cols), lambda i: (0, 0)),
                  pl.BlockSpec((1, N_HEADS), lambda i: (0, 0))],
        out_specs=[row(c) for c in widths],
        out_shape=[jax.ShapeDtypeStruct((n, c), f32) for c in widths],
        compiler_params=_cparams(("arbitrary",)),
        name="inproj_fox",
    )(x, w_pad, b_f.reshape(1, N_HEADS))


QROWS = GROUP * Q_TILE


def _flash_block(q_ref, kbf, vbf, o_ref, qs, m_sc, acc_sc, n_chunks, bias_fn):
    for h in range(N_HEADS):
        g = h % GROUP
        qs[h // GROUP, g * Q_TILE:(g + 1) * Q_TILE, :] = (
            q_ref[:, h * HEAD_DIM:(h + 1) * HEAD_DIM] * QK_SCALE).astype(bf16)
    m_sc[...] = jnp.full(m_sc.shape, NEG, f32)
    acc_sc[...] = jnp.zeros(acc_sc.shape, f32)

    def body(c, _):
        off = pl.multiple_of(c * KEY_CHUNK, KEY_CHUNK)
        for kh in range(N_KV):
            kc = kbf[kh, pl.ds(off, KEY_CHUNK), :]
            vc = vbf[kh, pl.ds(off, KEY_CHUNK), :]
            lg = _dot_nt(qs[kh], kc) + bias_fn(c, kh)
            m = m_sc[kh]
            part = lg[:, 0:LANES]
            for j in range(1, KEY_CHUNK // LANES):
                part = jnp.maximum(part, lg[:, j * LANES:(j + 1) * LANES])
            m_new = jnp.maximum(m, jnp.max(part, axis=1, keepdims=True))
            p = jnp.exp(lg - m_new)
            acc_sc[kh] = jnp.exp(m - m_new) * acc_sc[kh] + _dot(p.astype(bf16), vc)
            m_sc[kh] = m_new
        return 0

    lax.fori_loop(0, n_chunks, body, 0)
    for h in range(N_HEADS):
        kh, g = h // GROUP, h % GROUP
        acc = acc_sc[kh, g * Q_TILE:(g + 1) * Q_TILE, :]
        o_ref[:, h * HEAD_DIM:(h + 1) * HEAD_DIM] = acc[:, 0:HEAD_DIM] / acc[:, HEAD_DIM:2 * HEAD_DIM]


_FLASH_SCRATCH = [pltpu.VMEM((N_KV, QROWS, HEAD_DIM), bf16),
                  pltpu.VMEM((N_KV, QROWS, 1), f32),
                  pltpu.VMEM((N_KV, QROWS, 2 * HEAD_DIM), f32)]


def _stage_kv(k_ref, v_ref, kbf, vbf):
    for h in range(N_KV):
        kbf[h] = k_ref[:, h * HEAD_DIM:(h + 1) * HEAD_DIM].astype(bf16)
        vbf[h, :, 0:HEAD_DIM] = v_ref[:, h * HEAD_DIM:(h + 1) * HEAD_DIM].astype(bf16)
        vbf[h, :, HEAD_DIM:2 * HEAD_DIM] = jnp.ones((v_ref.shape[0], HEAD_DIM), bf16)


def _dsa_prompt_kernel(q_ref, iq_ref, iw_ref, ik_ref, k_ref, v_ref, o_ref,
                       ikbf, kbf, vbf, key3, mb3, dist3, qs, m_sc, acc_sc):
    i = pl.program_id(1)

    @pl.when(i == 0)
    def _():
        ikbf[...] = ik_ref[...].astype(bf16)
        _stage_kv(k_ref, v_ref, kbf, vbf)

    n_chunks = (i * Q_TILE + Q_TILE + KEY_CHUNK - 1) // KEY_CHUNK
    row = lax.broadcasted_iota(i32, (Q_TILE, KEY_CHUNK), 0)
    col = lax.broadcasted_iota(i32, (Q_TILE, KEY_CHUNK), 1)
    qpos = i * Q_TILE + row
    w = iw_ref[...] * IDX_SCALE
    iqb = iq_ref[...].astype(bf16)

    def score_body(c, _):
        ikc = ikbf[pl.ds(pl.multiple_of(c * KEY_CHUNK, KEY_CHUNK), KEY_CHUNK), :]
        s = jnp.zeros((Q_TILE, KEY_CHUNK), f32)
        for h in range(IDX_HEADS):
            rel = _dot_nt(iqb[:, h * IDX_DIM:(h + 1) * IDX_DIM], ikc)
            s = s + w[:, h:h + 1] * jnp.maximum(rel, 0.0)
        d = qpos - (c * KEY_CHUNK + col)
        key3[c] = jnp.where(d >= 0, _order_key(s), INT_MIN)
        dist3[c] = d.astype(f32)
        return 0

    lax.fori_loop(0, n_chunks, score_body, 0)

    def count(pred):
        def body(c, acc):
            return acc + jnp.where(pred(key3[c]), 1.0, 0.0)
        acc = lax.fori_loop(0, n_chunks, body, jnp.zeros((Q_TILE, KEY_CHUNK), f32))
        return jnp.sum(acc, axis=1, keepdims=True)

    def bit_body(it, t):
        cand = t ^ lax.shift_left(jnp.int32(1), 31 - it)
        return jnp.where(count(lambda kk: kk >= cand) >= TOPK, cand, t)

    thr = lax.fori_loop(0, 32, bit_body, jnp.full((Q_TILE, 1), INT_MIN, i32))
    thr = jnp.maximum(thr, INT_MIN + 1)
    need = TOPK - count(lambda kk: kk > thr)
    n_tie = count(lambda kk: kk == thr)
    has_excess = jnp.max(n_tie - need) > 0.0

    @pl.when(jnp.logical_not(has_excess))
    def _():
        def body(c, _):
            mb3[c] = jnp.where(key3[c] >= thr, 0.0, NEG)
            return 0
        lax.fori_loop(0, n_chunks, body, 0)

    @pl.when(has_excess)
    def _():
        tri = _upper_tri(KEY_CHUNK)

        def body(c, before):
            kk = key3[c]
            tie = jnp.where(kk == thr, 1.0, 0.0)
            rank = before + _dot(tie.astype(bf16), tri)
            sel = (kk > thr) | ((kk == thr) & (rank <= need))
            mb3[c] = jnp.where(sel, 0.0, NEG)
            return before + jnp.sum(tie, axis=1, keepdims=True)
        lax.fori_loop(0, n_chunks, body, jnp.zeros((Q_TILE, 1), f32))

    def bias(c, kh):
        mb = mb3[c]
        d = dist3[c]
        slabs = [mb - float(2.0 ** (-8.0 * (kh * GROUP + g + 1) / N_HEADS)) * d for g in range(GROUP)]
        return jnp.concatenate(slabs, axis=0)

    _flash_block(q_ref, kbf, vbf, o_ref, qs, m_sc, acc_sc, n_chunks, bias)


def _dsa_prompt(q, iq, iw, ik, k, v, batch, seq):
    nq = seq // Q_TILE
    nc = seq // KEY_CHUNK
    qspec = lambda c: pl.BlockSpec((Q_TILE, c), lambda b, i: (b * nq + i, 0))
    bspec = lambda c: pl.BlockSpec((seq, c), lambda b, i: (b, 0))
    return pl.pallas_call(
        _dsa_prompt_kernel,
        grid=(batch, nq),
        in_specs=[qspec(DQ), qspec(IDX_HEADS * IDX_DIM), qspec(IDX_HEADS),
                  bspec(IDX_DIM), bspec(DKV), bspec(DKV)],
        out_specs=qspec(DQ),
        out_shape=jax.ShapeDtypeStruct((batch * seq, DQ), f32),
        scratch_shapes=[pltpu.VMEM((seq, IDX_DIM), bf16),
                        pltpu.VMEM((N_KV, seq, HEAD_DIM), bf16),
                        pltpu.VMEM((N_KV, seq, 2 * HEAD_DIM), bf16),
                        pltpu.VMEM((nc, Q_TILE, KEY_CHUNK), i32),
                        pltpu.VMEM((nc, Q_TILE, KEY_CHUNK), f32),
                        pltpu.VMEM((nc, Q_TILE, KEY_CHUNK), f32)] + _FLASH_SCRATCH,
        compiler_params=_cparams(("arbitrary", "arbitrary")),
        name="dsa_prompt",
    )(q, iq, iw, ik, k, v)


def _cum_kernel(lf_ref, o_ref):
    tri = _upper_tri(KEY_CHUNK)
    carry = jnp.zeros((N_HEADS, 1), f32)
    for c in range(lf_ref.shape[1] // KEY_CHUNK):
        cum = _dot3(lf_ref[:, c * KEY_CHUNK:(c + 1) * KEY_CHUNK], tri) + carry
        o_ref[0, c] = cum
        carry = cum[:, KEY_CHUNK - 1:KEY_CHUNK]


def _fox_cum(lf_t, batch, seq):
    nc = seq // KEY_CHUNK
    return pl.pallas_call(
        _cum_kernel,
        grid=(batch,),
        in_specs=[pl.BlockSpec((N_HEADS, seq), lambda b: (0, b))],
        out_specs=pl.BlockSpec((1, nc, N_HEADS, KEY_CHUNK), lambda b: (b, 0, 0, 0)),
        out_shape=jax.ShapeDtypeStruct((batch, nc, N_HEADS, KEY_CHUNK), f32),
        compiler_params=_cparams(("arbitrary",)),
        name="fox_cum",
    )(lf_t)


def _fox_prompt_kernel(q_ref, cum_ref, k_ref, v_ref, o_ref, kbf, vbf, qs, m_sc, acc_sc):
    i = pl.program_id(1)

    @pl.when(i == 0)
    def _():
        _stage_kv(k_ref, v_ref, kbf, vbf)

    n_chunks = (i * Q_TILE + Q_TILE + KEY_CHUNK - 1) // KEY_CHUNK
    row = lax.broadcasted_iota(i32, (Q_TILE, KEY_CHUNK), 0)
    col = lax.broadcasted_iota(i32, (Q_TILE, KEY_CHUNK), 1)
    rel = i * Q_TILE + row - col

    def bias(c, kh):
        ck = cum_ref[0, c]
        causal = rel - c * KEY_CHUNK >= 0
        slabs = [jnp.where(causal, -ck[kh * GROUP + g:kh * GROUP + g + 1, :], NEG) for g in range(GROUP)]
        return jnp.concatenate(slabs, axis=0)

    _flash_block(q_ref, kbf, vbf, o_ref, qs, m_sc, acc_sc, n_chunks, bias)


def _fox_prompt(q, cum, k, v, batch, seq):
    nq = seq // Q_TILE
    nc = seq // KEY_CHUNK
    qspec = lambda c: pl.BlockSpec((Q_TILE, c), lambda b, i: (b * nq + i, 0))
    bspec = lambda c: pl.BlockSpec((seq, c), lambda b, i: (b, 0))
    return pl.pallas_call(
        _fox_prompt_kernel,
        grid=(batch, nq),
        in_specs=[qspec(DQ), pl.BlockSpec((1, nc, N_HEADS, KEY_CHUNK), lambda b, i: (b, 0, 0, 0)),
                  bspec(DKV), bspec(DKV)],
        out_specs=qspec(DQ),
        out_shape=jax.ShapeDtypeStruct((batch * seq, DQ), f32),
        scratch_shapes=[pltpu.VMEM((N_KV, seq, HEAD_DIM), bf16),
                        pltpu.VMEM((N_KV, seq, 2 * HEAD_DIM), bf16)] + _FLASH_SCRATCH,
        compiler_params=_cparams(("arbitrary", "arbitrary")),
        name="fox_prompt",
    )(q, cum, k, v)


SROWS = 64
SQ = SUBLANES


def _page_specs(shape, layer, n_pages, n_steps, group):
    def spec(j):
        def index(b, s, pt):
            return (layer, pt[b * n_pages + jnp.minimum(s, n_steps - 1) * group + j], 0, 0)
        return pl.BlockSpec((1, 1) + shape, index)
    return [spec(j) for j in range(group)]


def _dsa_sample_score_kernel(pt_ref, iq_ref, w_ref, *rest, n_steps, dec_seq, group):
    ikc_refs, ikn_ref, key_ref = rest[:group], rest[group], rest[group + 1]
    s_id = pl.program_id(1)
    iqb = iq_ref[0].astype(bf16)
    wcol = w_ref[0] * IDX_SCALE
    t = lax.broadcasted_iota(i32, (SQ, PAGE), 0)
    j = lax.broadcasted_iota(i32, (SQ, PAGE), 1)

    def score(ik):
        rel = _dot_nt(iqb, ik.astype(bf16))
        return jnp.sum((jnp.maximum(rel, 0.0) * wcol).reshape(SQ, IDX_HEADS, PAGE), axis=1)

    @pl.when(s_id < n_steps)
    def _():
        for g in range(group):
            key_ref[0, 0, g] = jnp.where(t < dec_seq, _order_key(score(ikc_refs[g][0, 0])), INT_MIN)

    @pl.when(s_id == n_steps)
    def _():
        ok = (t < dec_seq) & (j <= t)
        key_ref[0, 0, 0] = jnp.where(ok, _order_key(score(ikn_ref[0])), INT_MIN)
        for g in range(1, group):
            key_ref[0, 0, g] = jnp.full((SQ, PAGE), INT_MIN, i32)


def _dsa_sample_scores(pt, iq_rows, w_rows, cache_ik, ik_new, layer, n_pages, dec_seq):
    nb = iq_rows.shape[0]
    group = PAGES_PER_STEP
    n_steps = n_pages // group
    kern = functools.partial(_dsa_sample_score_kernel, n_steps=n_steps, dec_seq=dec_seq, group=group)
    per_b = lambda b, s, pt: (b, 0, 0)
    return pl.pallas_call(
        kern,
        grid_spec=pltpu.PrefetchScalarGridSpec(
            num_scalar_prefetch=1,
            grid=(nb, n_steps + 1),
            in_specs=[pl.BlockSpec((1, SQ * IDX_HEADS, IDX_DIM), per_b),
                      pl.BlockSpec((1, SQ * IDX_HEADS, 1), per_b)]
                     + _page_specs((PAGE, IDX_DIM), layer, n_pages, n_steps, group)
                     + [pl.BlockSpec((1, PAGE, IDX_DIM), per_b)],
            out_specs=pl.BlockSpec((1, 1, group, SQ, PAGE), lambda b, s, pt: (b, s, 0, 0, 0)),
        ),
        out_shape=jax.ShapeDtypeStruct((nb, n_steps + 1, group, SQ, PAGE), i32),
        compiler_params=_cparams(("arbitrary", "arbitrary")),
        name="dsa_sample_scores",
    )(pt, iq_rows, w_rows, *([cache_ik] * group), ik_new)


def _sample_attn_kernel(pt_ref, q_ref, *rest, mode, n_pages, n_steps, dec_seq, group):
    kc_refs, vc_refs = rest[:group], rest[group:2 * group]
    rest = rest[2 * group:]
    if mode == "dsa":
        keys_ref, kn_ref, vn_ref, slope_ref, o_ref = rest[:5]
        m_sc, l_sc, acc_sc, carry_sc, thr_sc, need_sc = rest[5:]
    else:
        lf_refs, (kn_ref, vn_ref, lfn_ref, o_ref) = rest[:group], rest[group:group + 4]
        m_sc, l_sc, acc_sc, carry_sc = rest[group + 4:]
    s_id = pl.program_id(1)
    row = lax.broadcasted_iota(i32, (SROWS, PAGE), 0)
    col = lax.broadcasted_iota(i32, (SROWS, PAGE), 1)
    tok = row // N_HEADS
    qb = (q_ref[0] * QK_SCALE).astype(bf16)

    @pl.when(s_id == 0)
    def _():
        m_sc[...] = jnp.full(m_sc.shape, NEG, f32)
        l_sc[...] = jnp.zeros(l_sc.shape, f32)
        acc_sc[...] = jnp.zeros(acc_sc.shape, f32)
        carry_sc[...] = jnp.zeros(carry_sc.shape, f32)

    def update(ks, vs, biases):
        lg = jnp.concatenate([_dot_nt(qb, k.astype(bf16)) + b for k, b in zip(ks, biases)], axis=1)
        m = m_sc[...]
        m_new = jnp.maximum(m, jnp.max(lg, axis=1, keepdims=True))
        a = jnp.exp(m - m_new)
        p = jnp.exp(lg - m_new)
        l_sc[...] = a * l_sc[...] + jnp.sum(p, axis=1, keepdims=True)
        pr = p.astype(bf16)
        pv = _dot(pr[:, 0:PAGE], vs[0].astype(bf16))
        for g in range(1, len(vs)):
            pv = pv + _dot(pr[:, g * PAGE:(g + 1) * PAGE], vs[g].astype(bf16))
        acc_sc[...] = a * acc_sc[...] + pv
        m_sc[...] = m_new

    if mode == "dsa":
        tri = _upper_tri(PAGE)

        @pl.when(s_id == 0)
        def _():
            kk = keys_ref[0].reshape((n_steps + 1) * group, SQ, PAGE)

            def count(pred):
                return jnp.sum(jnp.sum(jnp.where(pred(kk), 1.0, 0.0), axis=0), axis=1, keepdims=True)

            def bit_body(it, t):
                cand = t ^ lax.shift_left(jnp.int32(1), 31 - it)
                return jnp.where(count(lambda a: a >= cand[None]) >= TOPK, cand, t)

            thr = lax.fori_loop(0, 32, bit_body, jnp.full((SQ, 1), INT_MIN, i32))
            thr = jnp.maximum(thr, INT_MIN + 1)
            thr_sc[...] = thr
            need_sc[...] = TOPK - count(lambda a: a > thr[None])

        def dsa_biases(kps, first_page):
            thr, need = thr_sc[...], need_sc[...]
            before = carry_sc[0:SQ, :]
            out = []
            for g, kp in enumerate(kps):
                tie = jnp.where(kp == thr, 1.0, 0.0)
                rank = before + _dot(tie.astype(bf16), tri)
                sel = jnp.where((kp > thr) | ((kp == thr) & (rank <= need)), 1.0, 0.0)
                before = before + jnp.sum(tie, axis=1, keepdims=True)
                sel_rows = jnp.concatenate(
                    [jnp.broadcast_to(sel[t:t + 1, :], (N_HEADS, PAGE)) for t in range(dec_seq)], axis=0)
                dist = (n_pages * PAGE + tok - ((first_page + g) * PAGE + col)).astype(f32)
                out.append(jnp.where(sel_rows > 0.5, -slope_ref[...] * dist, NEG))
            carry_sc[0:SQ, :] = before
            return out

        @pl.when(s_id < n_steps)
        def _():
            kps = [keys_ref[0, s_id, g] for g in range(group)]
            update([r[0, 0] for r in kc_refs], [r[0, 0] for r in vc_refs], dsa_biases(kps, s_id * group))

        @pl.when(s_id == n_steps)
        def _():
            update([kn_ref[0]], [vn_ref[0]], dsa_biases([keys_ref[0, n_steps, 0]], n_pages))
            o_ref[0] = acc_sc[...] / l_sc[...]
    else:
        tri = _upper_tri(PAGE)

        def fox_biases(lfs, mask_new):
            before = carry_sc[...]
            out = []
            for lf_t in lfs:
                cum = before + _dot3(lf_t, tri)
                before = cum[:, PAGE - 1:PAGE]
                ck = jnp.concatenate([cum] * dec_seq, axis=0)
                out.append(jnp.where(col <= tok, -ck, NEG) if mask_new else -ck)
            carry_sc[...] = before
            return out

        @pl.when(s_id < n_steps)
        def _():
            update([r[0, 0] for r in kc_refs], [r[0, 0] for r in vc_refs],
                   fox_biases([r[0, 0] for r in lf_refs], False))

        @pl.when(s_id == n_steps)
        def _():
            update([kn_ref[0]], [vn_ref[0]], fox_biases([lfn_ref[0]], True))
            o_ref[0] = acc_sc[...] / l_sc[...]


def _sample_attn(mode, pt, q_rows, cache_k, cache_v, k_new, v_new, aux, aux_new, layer, n_pages, dec_seq):
    nb = q_rows.shape[0]
    group = PAGES_PER_STEP
    n_steps = n_pages // group
    kern = functools.partial(_sample_attn_kernel, mode=mode, n_pages=n_pages, n_steps=n_steps,
                             dec_seq=dec_seq, group=group)
    per_b = lambda b, s, pt: (b, 0, 0)
    kv_pages = _page_specs((PAGE, DKV), layer, n_pages, n_steps, group)
    new_page = pl.BlockSpec((1, PAGE, DKV), per_b)
    scratch = [pltpu.VMEM((SROWS, 1), f32), pltpu.VMEM((SROWS, 1), f32), pltpu.VMEM((SROWS, DKV), f32),
               pltpu.VMEM((N_HEADS, 1), f32)]
    if mode == "dsa":
        mid_specs = [pl.BlockSpec((1, n_steps + 1, group, SQ, PAGE), lambda b, s, pt: (b, 0, 0, 0, 0)),
                     new_page, new_page, pl.BlockSpec((SROWS, 1), lambda b, s, pt: (0, 0))]
        mid_args = [aux, k_new, v_new, aux_new]
        scratch += [pltpu.VMEM((SQ, 1), i32), pltpu.VMEM((SQ, 1), f32)]
    else:
        mid_specs = (_page_specs((N_HEADS, PAGE), layer, n_pages, n_steps, group)
                     + [new_page, new_page, pl.BlockSpec((1, N_HEADS, PAGE), per_b)])
        mid_args = [aux] * group + [k_new, v_new, aux_new]
    return pl.pallas_call(
        kern,
        grid_spec=pltpu.PrefetchScalarGridSpec(
            num_scalar_prefetch=1,
            grid=(nb, n_steps + 1),
            in_specs=[pl.BlockSpec((1, SROWS, DKV), per_b)] + kv_pages + kv_pages + mid_specs,
            out_specs=pl.BlockSpec((1, SROWS, DKV), per_b),
            scratch_shapes=scratch,
        ),
        out_shape=jax.ShapeDtypeStruct((nb, SROWS, DKV), f32),
        compiler_params=_cparams(("arbitrary", "arbitrary")),
        name=mode + "_sample_attn",
    )(pt, q_rows, *([cache_k] * group), *([cache_v] * group), *mid_args)


def _outproj_kernel(x_ref, op_ref, os_ref, w_ref, g_ref, b_ref, wr_hi_ref, wr_lo_ref, br_ref,
                    x1_ref, x1b_ref, gate_ref, exp_t_ref, rank_t_ref, cnt_ref, run_sc, *, n_prompt_tiles):
    i = pl.program_id(0)
    o = jnp.where(i < n_prompt_tiles, op_ref[...], os_ref[...]).astype(bf16)
    x1 = _layer_norm(ALPHA * x_ref[...] + _dot(o, w_ref[...]), g_ref[...], b_ref[...])
    x1_ref[...] = x1
    x1b_ref[...] = x1.astype(bf16)

    hi = x1.astype(bf16)
    lo = (x1 - hi.astype(f32)).astype(bf16)
    lg = _dot(hi, wr_hi_ref[...]) + (_dot(lo, wr_hi_ref[...]) + _dot(hi, wr_lo_ref[...])) + br_ref[...]
    lane = lax.broadcasted_iota(i32, lg.shape, 1)
    out_lane = lax.broadcasted_iota(i32, (ROW_TILE, LANES), 1)
    vals = []
    experts = jnp.zeros((ROW_TILE, LANES), i32)
    for k in range(TOP_E):
        m = jnp.max(lg, axis=1, keepdims=True)
        idx = jnp.min(jnp.where(lg == m, lane, N_EXPERTS), axis=1, keepdims=True)
        vals.append(m)
        experts = jnp.where(out_lane == k, idx, experts)
        lg = jnp.where(lane == idx, -jnp.inf, lg)
    es = [jnp.exp(v - vals[0]) for v in vals]
    tot = es[0] + es[1] + es[2] + es[3]
    gates = jnp.zeros((ROW_TILE, LANES), f32)
    for k in range(TOP_E):
        gates = jnp.where(out_lane == k, es[k] / tot, gates)
    gate_ref[...] = gates

    @pl.when(i == 0)
    def _():
        run_sc[...] = jnp.zeros(run_sc.shape, f32)

    exp_t = experts.T[0:SUBLANES, :]
    e_iota = lax.broadcasted_iota(i32, (N_EXPERTS, ROW_TILE), 0)
    hits = [e_iota == exp_t[k:k + 1, :] for k in range(TOP_E)]
    member = jnp.where(hits[0] | hits[1] | hits[2] | hits[3], 1.0, 0.0)
    pos = run_sc[...] + _dot(member.astype(bf16), _upper_tri(ROW_TILE, strict=True))
    sub = lax.broadcasted_iota(i32, (SUBLANES, ROW_TILE), 0)
    rank_t = jnp.zeros((SUBLANES, ROW_TILE), f32)
    for k in range(TOP_E):
        rank_t = jnp.where(sub == k, jnp.sum(jnp.where(hits[k], pos, 0.0), axis=0, keepdims=True), rank_t)
    run_sc[...] = run_sc[...] + jnp.sum(member, axis=1, keepdims=True)
    exp_t_ref[...] = exp_t
    rank_t_ref[...] = rank_t.astype(i32)
    cnt_ref[...] = jnp.broadcast_to(run_sc[...], cnt_ref.shape).astype(i32)


def _outproj(x, o_prompt, o_sample, w_bf, g, b, wr_hi, wr_lo, br):
    n = x.shape[0]
    npt = o_prompt.shape[0] // ROW_TILE
    nst = o_sample.shape[0] // ROW_TILE
    row = lambda c: pl.BlockSpec((ROW_TILE, c), lambda i: (i, 0))
    col = pl.BlockSpec((SUBLANES, ROW_TILE), lambda i: (0, i))
    full = lambda r, c: pl.BlockSpec((r, c), lambda i: (0, 0))
    kern = functools.partial(_outproj_kernel, n_prompt_tiles=npt)
    return pl.pallas_call(
        kern,
        grid=(n // ROW_TILE,),
        in_specs=[row(D_MODEL),
                  pl.BlockSpec((ROW_TILE, DQ), lambda i: (jnp.minimum(i, npt - 1), 0)),
                  pl.BlockSpec((ROW_TILE, DQ), lambda i: (jnp.clip(i - npt, 0, nst - 1), 0)),
                  full(DQ, D_MODEL), full(1, D_MODEL), full(1, D_MODEL),
                  full(D_MODEL, N_EXPERTS), full(D_MODEL, N_EXPERTS), full(1, N_EXPERTS)],
        out_specs=[row(D_MODEL), row(D_MODEL), row(LANES), col, col, full(N_EXPERTS, LANES)],
        out_shape=[jax.ShapeDtypeStruct((n, D_MODEL), f32), jax.ShapeDtypeStruct((n, D_MODEL), bf16),
                   jax.ShapeDtypeStruct((n, LANES), f32),
                   jax.ShapeDtypeStruct((SUBLANES, n), i32), jax.ShapeDtypeStruct((SUBLANES, n), i32),
                   jax.ShapeDtypeStruct((N_EXPERTS, LANES), i32)],
        scratch_shapes=[pltpu.VMEM((N_EXPERTS, 1), f32)],
        compiler_params=_cparams(("arbitrary",)),
        name="outproj_router",
    )(x, o_prompt, o_sample, w_bf, g, b, wr_hi, wr_lo, br)


def _moe_kernel(bexp_ref, used_ref, x_ref, wgu_ref, bgu_ref, wdn_ref, bdn_ref, y_ref, wgu_bf, wdn_bf):
    i = pl.program_id(0)
    e = bexp_ref[i]
    prev = bexp_ref[jnp.maximum(i - 1, 0)]

    @pl.when((i == 0) | (e != prev))
    def _():
        wgu_bf[...] = wgu_ref[0, 0].astype(bf16)
        wdn_bf[...] = wdn_ref[0, 0].astype(bf16)

    @pl.when(i < used_ref[0])
    def _():
        h = _dot(x_ref[...], wgu_bf[...]) + bgu_ref[0, 0]
        gate = jnp.minimum(h[:, 0:D_FF], SWIGLU_LIMIT)
        up = jnp.clip(h[:, D_FF:2 * D_FF], -SWIGLU_LIMIT, SWIGLU_LIMIT)
        glu = gate * jax.nn.sigmoid(SWIGLU_ALPHA * gate)
        act = ((up + 1.0) * glu).astype(bf16)
        y_ref[...] = _dot(act, wdn_bf[...]) + bdn_ref[0, 0]

    @pl.when(i >= used_ref[0])
    def _():
        y_ref[...] = jnp.zeros(y_ref.shape, f32)


def _moe(block_exp, n_used, x_rows, w_gu, b_gu, w_dn, b_dn, layer):
    n_blocks = block_exp.shape[0]
    wspec = lambda r, c: pl.BlockSpec((1, 1, r, c), lambda i, be, nu: (layer, be[i], 0, 0))
    return pl.pallas_call(
        _moe_kernel,
        grid_spec=pltpu.PrefetchScalarGridSpec(
            num_scalar_prefetch=2,
            grid=(n_blocks,),
            in_specs=[pl.BlockSpec((MOE_BLOCK, D_MODEL), lambda i, be, nu: (i, 0)),
                      wspec(D_MODEL, 2 * D_FF), wspec(1, 2 * D_FF),
                      wspec(D_FF, D_MODEL), wspec(1, D_MODEL)],
            out_specs=pl.BlockSpec((MOE_BLOCK, D_MODEL), lambda i, be, nu: (i, 0)),
            scratch_shapes=[pltpu.VMEM((D_MODEL, 2 * D_FF), bf16), pltpu.VMEM((D_FF, D_MODEL), bf16)],
        ),
        out_shape=jax.ShapeDtypeStruct((n_blocks * MOE_BLOCK, D_MODEL), f32),
        compiler_params=_cparams(("arbitrary",)),
        name="moe_experts",
    )(block_exp, n_used, x_rows, w_gu, b_gu, w_dn, b_dn)


def _combine_kernel(x_ref, y_ref, gate_ref, g_ref, b_ref, o_ref):
    gates = gate_ref[...]
    mix = gates[:, 0:1] * y_ref[0]
    for k in range(1, TOP_E):
        mix = mix + gates[:, k:k + 1] * y_ref[k]
    o_ref[...] = _layer_norm(ALPHA * x_ref[...] + mix, g_ref[...], b_ref[...])


def _combine(x1, y_asg, gates, g, b):
    n = x1.shape[0]
    row = lambda c: pl.BlockSpec((ROW_TILE, c), lambda i: (i, 0))
    full = lambda r, c: pl.BlockSpec((r, c), lambda i: (0, 0))
    return pl.pallas_call(
        _combine_kernel,
        grid=(n // ROW_TILE,),
        in_specs=[row(D_MODEL), pl.BlockSpec((TOP_E, ROW_TILE, D_MODEL), lambda i: (0, i, 0)),
                  row(LANES), full(1, D_MODEL), full(1, D_MODEL)],
        out_specs=row(D_MODEL),
        out_shape=jax.ShapeDtypeStruct((n, D_MODEL), f32),
        compiler_params=_cparams(("arbitrary",)),
        name="combine_norm",
    )(x1, y_asg, gates, g, b)


def _route(exp_t, rank_t, counts, n_tok):
    padded = (counts + MOE_BLOCK - 1) // MOE_BLOCK * MOE_BLOCK
    pad_end = jnp.cumsum(padded)
    pad_start = pad_end - padded
    dest = (pad_start[exp_t] + rank_t).reshape(-1)
    n_blocks = -(-n_tok * TOP_E // MOE_BLOCK) + N_EXPERTS
    tok = jnp.tile(jnp.arange(n_tok, dtype=i32), TOP_E)
    row_tok = jnp.zeros((n_blocks * MOE_BLOCK,), i32).at[dest].set(tok)
    block_start = jnp.arange(n_blocks, dtype=i32) * MOE_BLOCK
    block_exp = jnp.minimum(jnp.sum(pad_end[None, :] <= block_start[:, None], axis=1), N_EXPERTS - 1).astype(i32)
    n_used = (pad_end[-1:] // MOE_BLOCK).astype(i32)
    return row_tok, block_exp, n_used, dest


def _pad_cols(w):
    cols = w.shape[1]
    padded = -(-cols // LANES) * LANES
    return jnp.pad(w, ((0, 0), (0, padded - cols))).astype(bf16)


def _diag_heads(o_full, nb, dec_seq):
    o6 = o_full.reshape(nb, dec_seq, N_KV, GROUP, N_KV, HEAD_DIM)
    return jnp.einsum('btkgkd->btkgd', o6).reshape(nb * dec_seq, DQ)


def _block_diag_q(q_s, nb, dec_seq):
    q5 = q_s.reshape(nb, dec_seq, N_KV, GROUP, 1, HEAD_DIM)
    eye = jnp.eye(N_KV, dtype=q_s.dtype).reshape(1, 1, N_KV, 1, N_KV, 1)
    return (q5 * eye).reshape(nb, SROWS, DKV)


def _pad_page(a, nb, dec_seq):
    a3 = a.reshape(nb, dec_seq, a.shape[-1])
    return jnp.pad(a3, ((0, 0), (0, PAGE - dec_seq), (0, 0)))


def kernel(x_prompt, x_sample, cache_dsa_k, cache_dsa_v, cache_dsa_ik, cache_fox_k, cache_fox_v,
           cache_fox_logf, page_table, w_in_dsa, w_out_dsa, w_in_fox, b_forget, w_out_fox,
           ln_gain, ln_bias, w_router, b_router, w_gate_up, b_gate_up, w_down, b_down):
    batch, seq, _ = x_prompt.shape
    nb, dec_seq, _ = x_sample.shape
    n_pages = page_table.shape[1]
    pool = cache_dsa_k.shape[1]
    n_p = batch * seq
    n_s = nb * dec_seq
    n_tok = n_p + n_s
    assert dec_seq * N_HEADS == SROWS and n_s == ROW_TILE and n_p % ROW_TILE == 0
    assert n_pages % PAGES_PER_STEP == 0 and seq % KEY_CHUNK == 0

    x = jnp.concatenate([x_prompt.reshape(n_p, D_MODEL), x_sample.reshape(n_s, D_MODEL)], axis=0)
    pt = page_table.reshape(-1).astype(i32)
    slopes = 2.0 ** (-8.0 * jnp.arange(1, N_HEADS + 1, dtype=f32) / N_HEADS)
    slope_rows = jnp.tile(slopes, dec_seq).reshape(SROWS, 1)

    dsa_k = cache_dsa_k.reshape(-1, pool, PAGE, DKV)
    dsa_v = cache_dsa_v.reshape(-1, pool, PAGE, DKV)
    fox_k = cache_fox_k.reshape(-1, pool, PAGE, DKV)
    fox_v = cache_fox_v.reshape(-1, pool, PAGE, DKV)
    fox_lf_t = jnp.swapaxes(cache_fox_logf, 2, 3)
    b_gu4 = b_gate_up.reshape(DEPTH, N_EXPERTS, 1, 2 * D_FF)
    b_dn4 = b_down.reshape(DEPTH, N_EXPERTS, 1, D_MODEL)

    outs = {name: [] for name in ("dk", "dv", "di", "fk", "fv", "fl")}
    for layer in range(DEPTH):
        j = layer // 2
        if layer % 2 == 0:
            q, k, v, iq, ik, iw = _inproj_dsa(x, _pad_cols(w_in_dsa[j]))
            o_p = _dsa_prompt(q, iq, iw, ik, k, v, batch, seq)
            iq_rows = jnp.pad(iq[n_p:].reshape(nb, dec_seq * IDX_HEADS, IDX_DIM),
                              ((0, 0), (0, (SQ - dec_seq) * IDX_HEADS), (0, 0)))
            w_rows = jnp.pad(iw[n_p:].reshape(nb, dec_seq * IDX_HEADS, 1),
                             ((0, 0), (0, (SQ - dec_seq) * IDX_HEADS), (0, 0)))
            keys = _dsa_sample_scores(pt, iq_rows, w_rows, cache_dsa_ik, _pad_page(ik[n_p:], nb, dec_seq),
                                      j, n_pages, dec_seq)
            o_s = _sample_attn("dsa", pt, _block_diag_q(q[n_p:], nb, dec_seq), dsa_k, dsa_v,
                               _pad_page(k[n_p:], nb, dec_seq), _pad_page(v[n_p:], nb, dec_seq),
                               keys, slope_rows, j, n_pages, dec_seq)
            w_out = w_out_dsa[j]
            outs["dk"].append(k); outs["dv"].append(v); outs["di"].append(ik)
        else:
            q, k, v, lf = _inproj_fox(x, _pad_cols(w_in_fox[j]), b_forget[j])
            cum = _fox_cum(lf[:n_p].T, batch, seq)
            o_p = _fox_prompt(q, cum, k, v, batch, seq)
            lf_new_t = jnp.swapaxes(_pad_page(lf[n_p:], nb, dec_seq), 1, 2)
            o_s = _sample_attn("fox", pt, _block_diag_q(q[n_p:], nb, dec_seq), fox_k, fox_v,
                               _pad_page(k[n_p:], nb, dec_seq), _pad_page(v[n_p:], nb, dec_seq),
                               fox_lf_t, lf_new_t, j, n_pages, dec_seq)
            w_out = w_out_fox[j]
            outs["fk"].append(k); outs["fv"].append(v); outs["fl"].append(lf)

        wr = w_router[layer]
        wr_hi = wr.astype(bf16)
        wr_lo = (wr - wr_hi.astype(f32)).astype(bf16)
        x1, x1b, gates, exp_t, rank_t, cnt = _outproj(
            x, o_p, _diag_heads(o_s, nb, dec_seq), w_out.astype(bf16),
            ln_gain[layer, 0].reshape(1, D_MODEL), ln_bias[layer, 0].reshape(1, D_MODEL),
            wr_hi, wr_lo, b_router[layer].reshape(1, N_EXPERTS))

        row_tok, block_exp, n_used, dest = _route(exp_t[:TOP_E], rank_t[:TOP_E], cnt[:, 0], n_tok)
        y_rows = _moe(block_exp, n_used, x1b[row_tok], w_gate_up, b_gu4, w_down, b_dn4, layer)
        y_asg = y_rows[dest].reshape(TOP_E, n_tok, D_MODEL)
        x = _combine(x1, y_asg, gates, ln_gain[layer, 1].reshape(1, D_MODEL),
                     ln_bias[layer, 1].reshape(1, D_MODEL))

    def stack(name, shape_p, shape_s):
        a = jnp.stack(outs[name])
        return a[:, :n_p].reshape((-1,) + shape_p), a[:, n_p:].reshape((-1,) + shape_s)

    kv_p, kv_s = (batch, seq, N_KV, HEAD_DIM), (nb, dec_seq, N_KV, HEAD_DIM)
    dk_p, dk_s = stack("dk", kv_p, kv_s)
    dv_p, dv_s = stack("dv", kv_p, kv_s)
    di_p, di_s = stack("di", (batch, seq, IDX_DIM), (nb, dec_seq, IDX_DIM))
    fk_p, fk_s = stack("fk", kv_p, kv_s)
    fv_p, fv_s = stack("fv", kv_p, kv_s)
    fl_p, fl_s = stack("fl", (batch, seq, N_HEADS), (nb, dec_seq, N_HEADS))
    return (x[:n_p].reshape(batch, seq, D_MODEL), x[n_p:].reshape(nb, dec_seq, D_MODEL),
            dk_p, dv_p, di_p, fk_p, fv_p, fl_p, dk_s, dv_s, di_s, fk_s, fv_s, fl_s)
```

```python
import functools

import jax
import jax.numpy as jnp
from jax import lax
from jax.experimental import pallas as pl
from jax.experimental.pallas import tpu as pltpu

f32 = jnp.float32
bf16 = jnp.bfloat16
i32 = jnp.int32

D_MODEL = 1024
DEPTH = 4
PAGE = 128
N_HEADS = 16
HEAD_DIM = 64
N_KV = 4
GROUP = N_HEADS // N_KV
DQ = N_HEADS * HEAD_DIM
DKV = N_KV * HEAD_DIM
IDX_HEADS = 8
IDX_DIM = 64
TOPK = 256
N_EXPERTS = 32
TOP_E = 4
D_FF = D_MODEL
SWIGLU_LIMIT = 7.0
SWIGLU_ALPHA = 1.702
MOE_BLOCK = 128
LN_EPS = 1e-5
ALPHA = (2 * DEPTH) ** 0.25
IDX_SCALE = IDX_HEADS ** -0.5 * IDX_DIM ** -0.5
QK_SCALE = HEAD_DIM ** -0.5

LANES = 128
SUBLANES = 8
Q_TILE = 128
KEY_CHUNK = 256
ROW_TILE = 128
PAGES_PER_STEP = 16
INT_MIN = -2 ** 31
NEG = -1e30
VMEM_LIMIT = 56 * 1024 * 1024

NT = (((1,), (1,)), ((), ()))


def _cparams(sem):
    return pltpu.CompilerParams(dimension_semantics=sem, vmem_limit_bytes=VMEM_LIMIT)


def _split3(x):
    hi = x.astype(bf16)
    r1 = x - hi.astype(f32)
    mid = r1.astype(bf16)
    lo = (r1 - mid.astype(f32)).astype(bf16)
    return hi, mid, lo


def _dot(a, b):
    return jnp.dot(a, b, preferred_element_type=f32)


def _dot_nt(a, b):
    return lax.dot_general(a, b, NT, preferred_element_type=f32)


def _dot3(x, w_bf):
    hi, mid, lo = _split3(x)
    return _dot(hi, w_bf) + _dot(mid, w_bf) + _dot(lo, w_bf)


def _dot3_tn(x, w_bf):
    tn = (((0,), (0,)), ((), ()))
    return sum(lax.dot_general(part, w_bf, tn, preferred_element_type=f32) for part in _split3(x))


def _layer_norm(y, g, b):
    mu = jnp.mean(y, axis=-1, keepdims=True)
    d = y - mu
    var = jnp.mean(d * d, axis=-1, keepdims=True)
    return d * lax.rsqrt(var + LN_EPS) * g + b


def _order_key(s):
    s = jnp.where(s == 0.0, 0.0, s)
    bits = pltpu.bitcast(s, i32)
    return bits ^ ((bits >> 31) & 0x7FFFFFFF)


def _upper_tri(n, strict=False):
    r = lax.broadcasted_iota(i32, (n, n), 0)
    c = lax.broadcasted_iota(i32, (n, n), 1)
    return jnp.where((r < c) if strict else (r <= c), 1.0, 0.0).astype(bf16)


def _inproj_dsa_kernel(x_ref, w_ref, q_ref, k_ref, v_ref, iq_ref, ik_ref, iw_ref):
    xb = x_ref[...].astype(bf16)
    q_ref[...] = _dot(xb, w_ref[:, 0:DQ])
    k_ref[...] = _dot(xb, w_ref[:, DQ:DQ + DKV])
    v_ref[...] = _dot(xb, w_ref[:, DQ + DKV:DQ + 2 * DKV])
    o = DQ + 2 * DKV
    iq_ref[...] = _dot(xb, w_ref[:, o:o + IDX_HEADS * IDX_DIM])
    o += IDX_HEADS * IDX_DIM
    tail = _dot(xb, w_ref[:, o:o + LANES])
    ik_ref[...] = tail[:, 0:IDX_DIM]
    iw_ref[...] = tail[:, IDX_DIM:IDX_DIM + IDX_HEADS]


def _inproj_dsa(x, w_pad):
    n = x.shape[0]
    cols = w_pad.shape[1]
    row = lambda c: pl.BlockSpec((ROW_TILE, c), lambda i: (i, 0))
    widths = (DQ, DKV, DKV, IDX_HEADS * IDX_DIM, IDX_DIM, IDX_HEADS)
    return pl.pallas_call(
        _inproj_dsa_kernel,
        grid=(n // ROW_TILE,),
        in_specs=[row(D_MODEL), pl.BlockSpec((D_MODEL, cols), lambda i: (0, 0))],
        out_specs=[row(c) for c in widths],
        out_shape=[jax.ShapeDtypeStruct((n, c), f32) for c in widths],
        compiler_params=_cparams(("arbitrary",)),
        name="inproj_dsa",
    )(x, w_pad)


def _inproj_fox_kernel(x_ref, w_ref, bf_ref, q_ref, k_ref, v_ref, lf_ref):
    xb = x_ref[...].astype(bf16)
    q_ref[...] = _dot(xb, w_ref[:, 0:DQ])
    k_ref[...] = _dot(xb, w_ref[:, DQ:DQ + DKV])
    v_ref[...] = _dot(xb, w_ref[:, DQ + DKV:DQ + 2 * DKV])
    o = DQ + 2 * DKV
    fg = _dot(xb, w_ref[:, o:o + LANES])[:, 0:N_HEADS] + bf_ref[...]
    lf_ref[...] = jax.nn.log_sigmoid(fg)


def _inproj_fox(x, w_pad, b_f):
    n = x.shape[0]
    cols = w_pad.shape[1]
    row = lambda c: pl.BlockSpec((ROW_TILE, c), lambda i: (i, 0))
    widths = (DQ, DKV, DKV, N_HEADS)
    return pl.pallas_call(
        _inproj_fox_kernel,
        grid=(n // ROW_TILE,),
        in_specs=[row(D_MODEL), pl.BlockSpec((D_MODEL, cols), lambda i: (0, 0)),
                  pl.BlockSpec((1, N_HEADS), lambda i: (0, 0))],
        out_specs=[row(c) for c in widths],
        out_shape=[jax.ShapeDtypeStruct((n, c), f32) for c in widths],
        compiler_params=_cparams(("arbitrary",)),
        name="inproj_fox",
    )(x, w_pad, b_f.reshape(1, N_HEADS))


QROWS = GROUP * Q_TILE


def _flash_block(q_ref, kbf, vbf, o_ref, qs, m_sc, acc_sc, n_chunks, bias_fn):
    for h in range(N_HEADS):
        g = h % GROUP
        qs[h // GROUP, g * Q_TILE:(g + 1) * Q_TILE, :] = (
            q_ref[:, h * HEAD_DIM:(h + 1) * HEAD_DIM] * QK_SCALE).astype(bf16)
    m_sc[...] = jnp.full(m_sc.shape, NEG, f32)
    acc_sc[...] = jnp.zeros(acc_sc.shape, f32)

    def body(c, _):
        off = pl.multiple_of(c * KEY_CHUNK, KEY_CHUNK)
        for kh in range(N_KV):
            kc = kbf[kh, pl.ds(off, KEY_CHUNK), :]
            vc = vbf[kh, pl.ds(off, KEY_CHUNK), :]
            lg = _dot_nt(qs[kh], kc) + bias_fn(c, kh)
            m = m_sc[kh]
            part = lg[:, 0:LANES]
            for j in range(1, KEY_CHUNK // LANES):
                part = jnp.maximum(part, lg[:, j * LANES:(j + 1) * LANES])
            m_new = jnp.maximum(m, jnp.max(part, axis=1, keepdims=True))
            p = jnp.exp(lg - m_new)
            acc_sc[kh] = jnp.exp(m - m_new) * acc_sc[kh] + _dot(p.astype(bf16), vc)
            m_sc[kh] = m_new
        return 0

    lax.fori_loop(0, n_chunks, body, 0)
    for h in range(N_HEADS):
        kh, g = h // GROUP, h % GROUP
        acc = acc_sc[kh, g * Q_TILE:(g + 1) * Q_TILE, :]
        o_ref[:, h * HEAD_DIM:(h + 1) * HEAD_DIM] = acc[:, 0:HEAD_DIM] / acc[:, HEAD_DIM:2 * HEAD_DIM]


_FLASH_SCRATCH = [pltpu.VMEM((N_KV, QROWS, HEAD_DIM), bf16),
                  pltpu.VMEM((N_KV, QROWS, 1), f32),
                  pltpu.VMEM((N_KV, QROWS, 2 * HEAD_DIM), f32)]


def _stage_kv(k_ref, v_ref, kbf, vbf):
    for h in range(N_KV):
        kbf[h] = k_ref[:, h * HEAD_DIM:(h + 1) * HEAD_DIM].astype(bf16)
        vbf[h, :, 0:HEAD_DIM] = v_ref[:, h * HEAD_DIM:(h + 1) * HEAD_DIM].astype(bf16)
        vbf[h, :, HEAD_DIM:2 * HEAD_DIM] = jnp.ones((v_ref.shape[0], HEAD_DIM), bf16)


def _dsa_prompt_kernel(q_ref, iq_ref, iw_ref, ik_ref, k_ref, v_ref, o_ref,
                       ikbf, kbf, vbf, key3, mb3, dist3, qs, m_sc, acc_sc):
    i = pl.program_id(1)

    @pl.when(i == 0)
    def _():
        ikbf[...] = ik_ref[...].astype(bf16)
        _stage_kv(k_ref, v_ref, kbf, vbf)

    n_chunks = (i * Q_TILE + Q_TILE + KEY_CHUNK - 1) // KEY_CHUNK
    row = lax.broadcasted_iota(i32, (Q_TILE, KEY_CHUNK), 0)
    col = lax.broadcasted_iota(i32, (Q_TILE, KEY_CHUNK), 1)
    qpos = i * Q_TILE + row
    w = iw_ref[...] * IDX_SCALE
    iqb = iq_ref[...].astype(bf16)

    def score_body(c, _):
        ikc = ikbf[pl.ds(pl.multiple_of(c * KEY_CHUNK, KEY_CHUNK), KEY_CHUNK), :]
        s = jnp.zeros((Q_TILE, KEY_CHUNK), f32)
        for h in range(IDX_HEADS):
            rel = _dot_nt(iqb[:, h * IDX_DIM:(h + 1) * IDX_DIM], ikc)
            s = s + w[:, h:h + 1] * jnp.maximum(rel, 0.0)
        d = qpos - (c * KEY_CHUNK + col)
        key3[c] = jnp.where(d >= 0, _order_key(s), INT_MIN)
        dist3[c] = d.astype(f32)
        return 0

    lax.fori_loop(0, n_chunks, score_body, 0)

    def count(pred):
        def body(c, acc):
            return acc + jnp.where(pred(key3[c]), 1.0, 0.0)
        acc = lax.fori_loop(0, n_chunks, body, jnp.zeros((Q_TILE, KEY_CHUNK), f32))
        return jnp.sum(acc, axis=1, keepdims=True)

    def bit_body(it, t):
        cand = t ^ lax.shift_left(jnp.int32(1), 31 - it)
        return jnp.where(count(lambda kk: kk >= cand) >= TOPK, cand, t)

    thr = lax.fori_loop(0, 32, bit_body, jnp.full((Q_TILE, 1), INT_MIN, i32))
    thr = jnp.maximum(thr, INT_MIN + 1)
    need = TOPK - count(lambda kk: kk > thr)
    n_tie = count(lambda kk: kk == thr)
    has_excess = jnp.max(n_tie - need) > 0.0

    @pl.when(jnp.logical_not(has_excess))
    def _():
        def body(c, _):
            mb3[c] = jnp.where(key3[c] >= thr, 0.0, NEG)
            return 0
        lax.fori_loop(0, n_chunks, body, 0)

    @pl.when(has_excess)
    def _():
        tri = _upper_tri(KEY_CHUNK)

        def body(c, before):
            kk = key3[c]
            tie = jnp.where(kk == thr, 1.0, 0.0)
            rank = before + _dot(tie.astype(bf16), tri)
            sel = (kk > thr) | ((kk == thr) & (rank <= need))
            mb3[c] = jnp.where(sel, 0.0, NEG)
            return before + jnp.sum(tie, axis=1, keepdims=True)
        lax.fori_loop(0, n_chunks, body, jnp.zeros((Q_TILE, 1), f32))

    def bias(c, kh):
        mb = mb3[c]
        d = dist3[c]
        slabs = [mb - float(2.0 ** (-8.0 * (kh * GROUP + g + 1) / N_HEADS)) * d for g in range(GROUP)]
        return jnp.concatenate(slabs, axis=0)

    _flash_block(q_ref, kbf, vbf, o_ref, qs, m_sc, acc_sc, n_chunks, bias)


def _dsa_prompt(q, iq, iw, ik, k, v, batch, seq):
    nq = seq // Q_TILE
    nc = seq // KEY_CHUNK
    qspec = lambda c: pl.BlockSpec((Q_TILE, c), lambda b, i: (b * nq + i, 0))
    bspec = lambda c: pl.BlockSpec((seq, c), lambda b, i: (b, 0))
    return pl.pallas_call(
        _dsa_prompt_kernel,
        grid=(batch, nq),
        in_specs=[qspec(DQ), qspec(IDX_HEADS * IDX_DIM), qspec(IDX_HEADS),
                  bspec(IDX_DIM), bspec(DKV), bspec(DKV)],
        out_specs=qspec(DQ),
        out_shape=jax.ShapeDtypeStruct((batch * seq, DQ), f32),
        scratch_shapes=[pltpu.VMEM((seq, IDX_DIM), bf16),
                        pltpu.VMEM((N_KV, seq, HEAD_DIM), bf16),
                        pltpu.VMEM((N_KV, seq, 2 * HEAD_DIM), bf16),
                        pltpu.VMEM((nc, Q_TILE, KEY_CHUNK), i32),
                        pltpu.VMEM((nc, Q_TILE, KEY_CHUNK), f32),
                        pltpu.VMEM((nc, Q_TILE, KEY_CHUNK), f32)] + _FLASH_SCRATCH,
        compiler_params=_cparams(("arbitrary", "arbitrary")),
        name="dsa_prompt",
    )(q, iq, iw, ik, k, v)


def _cum_kernel(lf_ref, o_ref):
    tri = _upper_tri(KEY_CHUNK)
    carry = jnp.zeros((N_HEADS, 1), f32)
    for c in range(lf_ref.shape[0] // KEY_CHUNK):
        cum = _dot3_tn(lf_ref[c * KEY_CHUNK:(c + 1) * KEY_CHUNK, :], tri) + carry
        o_ref[0, c] = cum
        carry = cum[:, KEY_CHUNK - 1:KEY_CHUNK]


def _fox_cum(lf, batch, seq):
    nc = seq // KEY_CHUNK
    return pl.pallas_call(
        _cum_kernel,
        grid=(batch,),
        in_specs=[pl.BlockSpec((seq, N_HEADS), lambda b: (b, 0))],
        out_specs=pl.BlockSpec((1, nc, N_HEADS, KEY_CHUNK), lambda b: (b, 0, 0, 0)),
        out_shape=jax.ShapeDtypeStruct((batch, nc, N_HEADS, KEY_CHUNK), f32),
        compiler_params=_cparams(("arbitrary",)),
        name="fox_cum",
    )(lf)


def _fox_prompt_kernel(q_ref, cum_ref, k_ref, v_ref, o_ref, kbf, vbf, qs, m_sc, acc_sc):
    i = pl.program_id(1)

    @pl.when(i == 0)
    def _():
        _stage_kv(k_ref, v_ref, kbf, vbf)

    n_chunks = (i * Q_TILE + Q_TILE + KEY_CHUNK - 1) // KEY_CHUNK
    row = lax.broadcasted_iota(i32, (Q_TILE, KEY_CHUNK), 0)
    col = lax.broadcasted_iota(i32, (Q_TILE, KEY_CHUNK), 1)
    rel = i * Q_TILE + row - col

    def bias(c, kh):
        ck = cum_ref[0, c]
        causal = rel - c * KEY_CHUNK >= 0
        slabs = [jnp.where(causal, -ck[kh * GROUP + g:kh * GROUP + g + 1, :], NEG) for g in range(GROUP)]
        return jnp.concatenate(slabs, axis=0)

    _flash_block(q_ref, kbf, vbf, o_ref, qs, m_sc, acc_sc, n_chunks, bias)


def _fox_prompt(q, cum, k, v, batch, seq):
    nq = seq // Q_TILE
    nc = seq // KEY_CHUNK
    qspec = lambda c: pl.BlockSpec((Q_TILE, c), lambda b, i: (b * nq + i, 0))
    bspec = lambda c: pl.BlockSpec((seq, c), lambda b, i: (b, 0))
    return pl.pallas_call(
        _fox_prompt_kernel,
        grid=(batch, nq),
        in_specs=[qspec(DQ), pl.BlockSpec((1, nc, N_HEADS, KEY_CHUNK), lambda b, i: (b, 0, 0, 0)),
                  bspec(DKV), bspec(DKV)],
        out_specs=qspec(DQ),
        out_shape=jax.ShapeDtypeStruct((batch * seq, DQ), f32),
        scratch_shapes=[pltpu.VMEM((N_KV, seq, HEAD_DIM), bf16),
                        pltpu.VMEM((N_KV, seq, 2 * HEAD_DIM), bf16)] + _FLASH_SCRATCH,
        compiler_params=_cparams(("arbitrary", "arbitrary")),
        name="fox_prompt",
    )(q, cum, k, v)


SROWS = 64
SQ = SUBLANES


def _page_specs(shape, layer, n_pages, n_steps, group):
    def spec(j):
        def index(b, s, pt):
            return (layer, pt[b * n_pages + jnp.minimum(s, n_steps - 1) * group + j], 0, 0)
        return pl.BlockSpec((1, 1) + shape, index)
    return [spec(j) for j in range(group)]


def _dsa_sample_score_kernel(pt_ref, iq_ref, w_ref, *rest, n_steps, dec_seq, group):
    ikc_refs, ikn_ref, key_ref = rest[:group], rest[group], rest[group + 1]
    s_id = pl.program_id(1)
    iqb = iq_ref[0].astype(bf16)
    wcol = w_ref[0] * IDX_SCALE
    t = lax.broadcasted_iota(i32, (SQ, PAGE), 0)
    j = lax.broadcasted_iota(i32, (SQ, PAGE), 1)

    def score(ik):
        rel = _dot_nt(iqb, ik.astype(bf16))
        return jnp.sum((jnp.maximum(rel, 0.0) * wcol).reshape(SQ, IDX_HEADS, PAGE), axis=1)

    @pl.when(s_id < n_steps)
    def _():
        for g in range(group):
            key_ref[0, 0, g] = jnp.where(t < dec_seq, _order_key(score(ikc_refs[g][0, 0])), INT_MIN)

    @pl.when(s_id == n_steps)
    def _():
        ok = (t < dec_seq) & (j <= t)
        key_ref[0, 0, 0] = jnp.where(ok, _order_key(score(ikn_ref[0])), INT_MIN)
        for g in range(1, group):
            key_ref[0, 0, g] = jnp.full((SQ, PAGE), INT_MIN, i32)


def _dsa_sample_scores(pt, iq_rows, w_rows, cache_ik, ik_new, layer, n_pages, dec_seq):
    nb = iq_rows.shape[0]
    group = PAGES_PER_STEP
    n_steps = n_pages // group
    kern = functools.partial(_dsa_sample_score_kernel, n_steps=n_steps, dec_seq=dec_seq, group=group)
    per_b = lambda b, s, pt: (b, 0, 0)
    return pl.pallas_call(
        kern,
        grid_spec=pltpu.PrefetchScalarGridSpec(
            num_scalar_prefetch=1,
            grid=(nb, n_steps + 1),
            in_specs=[pl.BlockSpec((1, SQ * IDX_HEADS, IDX_DIM), per_b),
                      pl.BlockSpec((1, SQ * IDX_HEADS, 1), per_b)]
                     + _page_specs((PAGE, IDX_DIM), layer, n_pages, n_steps, group)
                     + [pl.BlockSpec((1, PAGE, IDX_DIM), per_b)],
            out_specs=pl.BlockSpec((1, 1, group, SQ, PAGE), lambda b, s, pt: (b, s, 0, 0, 0)),
        ),
        out_shape=jax.ShapeDtypeStruct((nb, n_steps + 1, group, SQ, PAGE), i32),
        compiler_params=_cparams(("arbitrary", "arbitrary")),
        name="dsa_sample_scores",
    )(pt, iq_rows, w_rows, *([cache_ik] * group), ik_new)


def _sample_attn_kernel(pt_ref, q_ref, *rest, mode, n_pages, n_steps, dec_seq, group):
    kc_refs, vc_refs = rest[:group], rest[group:2 * group]
    rest = rest[2 * group:]
    if mode == "dsa":
        keys_ref, kn_ref, vn_ref, slope_ref, o_ref = rest[:5]
        m_sc, l_sc, acc_sc, carry_sc, thr_sc, need_sc = rest[5:]
    else:
        lf_refs, (kn_ref, vn_ref, lfn_ref, o_ref) = rest[:group], rest[group:group + 4]
        m_sc, l_sc, acc_sc, carry_sc = rest[group + 4:]
    s_id = pl.program_id(1)
    row = lax.broadcasted_iota(i32, (SROWS, PAGE), 0)
    col = lax.broadcasted_iota(i32, (SROWS, PAGE), 1)
    tok = row // N_HEADS
    qb = (q_ref[0] * QK_SCALE).astype(bf16)

    @pl.when(s_id == 0)
    def _():
        m_sc[...] = jnp.full(m_sc.shape, NEG, f32)
        l_sc[...] = jnp.zeros(l_sc.shape, f32)
        acc_sc[...] = jnp.zeros(acc_sc.shape, f32)
        carry_sc[...] = jnp.zeros(carry_sc.shape, f32)

    def update(ks, vs, biases):
        lg = jnp.concatenate([_dot_nt(qb, k.astype(bf16)) + b for k, b in zip(ks, biases)], axis=1)
        m = m_sc[...]
        m_new = jnp.maximum(m, jnp.max(lg, axis=1, keepdims=True))
        a = jnp.exp(m - m_new)
        p = jnp.exp(lg - m_new)
        l_sc[...] = a * l_sc[...] + jnp.sum(p, axis=1, keepdims=True)
        pr = p.astype(bf16)
        pv = _dot(pr[:, 0:PAGE], vs[0].astype(bf16))
        for g in range(1, len(vs)):
            pv = pv + _dot(pr[:, g * PAGE:(g + 1) * PAGE], vs[g].astype(bf16))
        acc_sc[...] = a * acc_sc[...] + pv
        m_sc[...] = m_new

    if mode == "dsa":
        tri = _upper_tri(PAGE)

        @pl.when(s_id == 0)
        def _():
            kk = keys_ref[0].reshape((n_steps + 1) * group, SQ, PAGE)

            def count(pred):
                return jnp.sum(jnp.sum(jnp.where(pred(kk), 1.0, 0.0), axis=0), axis=1, keepdims=True)

            def bit_body(it, t):
                cand = t ^ lax.shift_left(jnp.int32(1), 31 - it)
                return jnp.where(count(lambda a: a >= cand[None]) >= TOPK, cand, t)

            thr = lax.fori_loop(0, 32, bit_body, jnp.full((SQ, 1), INT_MIN, i32))
            thr = jnp.maximum(thr, INT_MIN + 1)
            thr_sc[...] = thr
            need_sc[...] = TOPK - count(lambda a: a > thr[None])

        def dsa_biases(kps, first_page):
            thr, need = thr_sc[...], need_sc[...]
            before = carry_sc[0:SQ, :]
            out = []
            for g, kp in enumerate(kps):
                tie = jnp.where(kp == thr, 1.0, 0.0)
                rank = before + _dot(tie.astype(bf16), tri)
                sel = jnp.where((kp > thr) | ((kp == thr) & (rank <= need)), 1.0, 0.0)
                before = before + jnp.sum(tie, axis=1, keepdims=True)
                sel_rows = jnp.concatenate(
                    [jnp.broadcast_to(sel[t:t + 1, :], (N_HEADS, PAGE)) for t in range(dec_seq)], axis=0)
                dist = (n_pages * PAGE + tok - ((first_page + g) * PAGE + col)).astype(f32)
                out.append(jnp.where(sel_rows > 0.5, -slope_ref[...] * dist, NEG))
            carry_sc[0:SQ, :] = before
            return out

        @pl.when(s_id < n_steps)
        def _():
            kps = [keys_ref[0, s_id, g] for g in range(group)]
            update([r[0, 0] for r in kc_refs], [r[0, 0] for r in vc_refs], dsa_biases(kps, s_id * group))

        @pl.when(s_id == n_steps)
        def _():
            update([kn_ref[0]], [vn_ref[0]], dsa_biases([keys_ref[0, n_steps, 0]], n_pages))
            o_ref[0] = acc_sc[...] / l_sc[...]
    else:
        tri = _upper_tri(PAGE)

        def fox_biases(lfs, mask_new):
            before = carry_sc[...]
            out = []
            for lf in lfs:
                cum = before + _dot3_tn(lf, tri)
                before = cum[:, PAGE - 1:PAGE]
                ck = jnp.concatenate([cum] * dec_seq, axis=0)
                out.append(jnp.where(col <= tok, -ck, NEG) if mask_new else -ck)
            carry_sc[...] = before
            return out

        @pl.when(s_id < n_steps)
        def _():
            update([r[0, 0] for r in kc_refs], [r[0, 0] for r in vc_refs],
                   fox_biases([r[0, 0] for r in lf_refs], False))

        @pl.when(s_id == n_steps)
        def _():
            update([kn_ref[0]], [vn_ref[0]], fox_biases([lfn_ref[0]], True))
            o_ref[0] = acc_sc[...] / l_sc[...]


def _sample_attn(mode, pt, q_rows, cache_k, cache_v, k_new, v_new, aux, aux_new, layer, n_pages, dec_seq):
    nb = q_rows.shape[0]
    group = PAGES_PER_STEP
    n_steps = n_pages // group
    kern = functools.partial(_sample_attn_kernel, mode=mode, n_pages=n_pages, n_steps=n_steps,
                             dec_seq=dec_seq, group=group)
    per_b = lambda b, s, pt: (b, 0, 0)
    kv_pages = _page_specs((PAGE, DKV), layer, n_pages, n_steps, group)
    new_page = pl.BlockSpec((1, PAGE, DKV), per_b)
    scratch = [pltpu.VMEM((SROWS, 1), f32), pltpu.VMEM((SROWS, 1), f32), pltpu.VMEM((SROWS, DKV), f32),
               pltpu.VMEM((N_HEADS, 1), f32)]
    if mode == "dsa":
        mid_specs = [pl.BlockSpec((1, n_steps + 1, group, SQ, PAGE), lambda b, s, pt: (b, 0, 0, 0, 0)),
                     new_page, new_page, pl.BlockSpec((SROWS, 1), lambda b, s, pt: (0, 0))]
        mid_args = [aux, k_new, v_new, aux_new]
        scratch += [pltpu.VMEM((SQ, 1), i32), pltpu.VMEM((SQ, 1), f32)]
    else:
        mid_specs = (_page_specs((PAGE, N_HEADS), layer, n_pages, n_steps, group)
                     + [new_page, new_page, pl.BlockSpec((1, PAGE, N_HEADS), per_b)])
        mid_args = [aux] * group + [k_new, v_new, aux_new]
    return pl.pallas_call(
        kern,
        grid_spec=pltpu.PrefetchScalarGridSpec(
            num_scalar_prefetch=1,
            grid=(nb, n_steps + 1),
            in_specs=[pl.BlockSpec((1, SROWS, DKV), per_b)] + kv_pages + kv_pages + mid_specs,
            out_specs=pl.BlockSpec((1, SROWS, DKV), per_b),
            scratch_shapes=scratch,
        ),
        out_shape=jax.ShapeDtypeStruct((nb, SROWS, DKV), f32),
        compiler_params=_cparams(("arbitrary", "arbitrary")),
        name=mode + "_sample_attn",
    )(pt, q_rows, *([cache_k] * group), *([cache_v] * group), *mid_args)


def _outproj_kernel(x_ref, op_ref, os_ref, w_ref, g_ref, b_ref, wr_hi_ref, wr_lo_ref, br_ref,
                    x1_ref, gate_ref, exp_t_ref, rank_t_ref, cnt_ref, run_sc, *, n_prompt_tiles):
    i = pl.program_id(0)
    o = jnp.where(i < n_prompt_tiles, op_ref[...], os_ref[...]).astype(bf16)
    x1 = _layer_norm(ALPHA * x_ref[...] + _dot(o, w_ref[...]), g_ref[...], b_ref[...])
    x1_ref[...] = x1

    hi = x1.astype(bf16)
    lo = (x1 - hi.astype(f32)).astype(bf16)
    lg = _dot(hi, wr_hi_ref[...]) + (_dot(lo, wr_hi_ref[...]) + _dot(hi, wr_lo_ref[...])) + br_ref[...]
    lane = lax.broadcasted_iota(i32, lg.shape, 1)
    out_lane = lax.broadcasted_iota(i32, (ROW_TILE, LANES), 1)
    vals = []
    experts = jnp.zeros((ROW_TILE, LANES), i32)
    for k in range(TOP_E):
        m = jnp.max(lg, axis=1, keepdims=True)
        idx = jnp.min(jnp.where(lg == m, lane, N_EXPERTS), axis=1, keepdims=True)
        vals.append(m)
        experts = jnp.where(out_lane == k, idx, experts)
        lg = jnp.where(lane == idx, -jnp.inf, lg)
    es = [jnp.exp(v - vals[0]) for v in vals]
    tot = es[0] + es[1] + es[2] + es[3]
    gates = jnp.zeros((ROW_TILE, LANES), f32)
    for k in range(TOP_E):
        gates = jnp.where(out_lane == k, es[k] / tot, gates)
    gate_ref[...] = gates

    @pl.when(i == 0)
    def _():
        run_sc[...] = jnp.zeros(run_sc.shape, f32)

    exp_t = experts.T[0:SUBLANES, :]
    e_iota = lax.broadcasted_iota(i32, (N_EXPERTS, ROW_TILE), 0)
    hits = [e_iota == exp_t[k:k + 1, :] for k in range(TOP_E)]
    member = jnp.where(hits[0] | hits[1] | hits[2] | hits[3], 1.0, 0.0)
    pos = run_sc[...] + _dot(member.astype(bf16), _upper_tri(ROW_TILE, strict=True))
    sub = lax.broadcasted_iota(i32, (SUBLANES, ROW_TILE), 0)
    rank_t = jnp.zeros((SUBLANES, ROW_TILE), f32)
    for k in range(TOP_E):
        rank_t = jnp.where(sub == k, jnp.sum(jnp.where(hits[k], pos, 0.0), axis=0, keepdims=True), rank_t)
    run_sc[...] = run_sc[...] + jnp.sum(member, axis=1, keepdims=True)
    exp_t_ref[...] = exp_t
    rank_t_ref[...] = rank_t.astype(i32)
    cnt_ref[...] = jnp.broadcast_to(run_sc[...], cnt_ref.shape).astype(i32)


def _outproj(x, o_prompt, o_sample, w_bf, g, b, wr_hi, wr_lo, br):
    n = x.shape[0]
    npt = o_prompt.shape[0] // ROW_TILE
    nst = o_sample.shape[0] // ROW_TILE
    row = lambda c: pl.BlockSpec((ROW_TILE, c), lambda i: (i, 0))
    col = pl.BlockSpec((SUBLANES, ROW_TILE), lambda i: (0, i))
    full = lambda r, c: pl.BlockSpec((r, c), lambda i: (0, 0))
    kern = functools.partial(_outproj_kernel, n_prompt_tiles=npt)
    return pl.pallas_call(
        kern,
        grid=(n // ROW_TILE,),
        in_specs=[row(D_MODEL),
                  pl.BlockSpec((ROW_TILE, DQ), lambda i: (jnp.minimum(i, npt - 1), 0)),
                  pl.BlockSpec((ROW_TILE, DQ), lambda i: (jnp.clip(i - npt, 0, nst - 1), 0)),
                  full(DQ, D_MODEL), full(1, D_MODEL), full(1, D_MODEL),
                  full(D_MODEL, N_EXPERTS), full(D_MODEL, N_EXPERTS), full(1, N_EXPERTS)],
        out_specs=[row(D_MODEL), row(LANES), col, col, full(N_EXPERTS, LANES)],
        out_shape=[jax.ShapeDtypeStruct((n, D_MODEL), f32),
                   jax.ShapeDtypeStruct((n, LANES), f32),
                   jax.ShapeDtypeStruct((SUBLANES, n), i32), jax.ShapeDtypeStruct((SUBLANES, n), i32),
                   jax.ShapeDtypeStruct((N_EXPERTS, LANES), i32)],
        scratch_shapes=[pltpu.VMEM((N_EXPERTS, 1), f32)],
        compiler_params=_cparams(("arbitrary",)),
        name="outproj_router",
    )(x, o_prompt, o_sample, w_bf, g, b, wr_hi, wr_lo, br)


def _moe_kernel(bexp_ref, used_ref, rows_ref, rows_next_ref, x_hbm, wgu_ref, bgu_ref, wdn_ref, bdn_ref,
                y_ref, wgu_bf, wdn_bf, xbuf, sem):
    i = pl.program_id(0)
    n_used = used_ref[0]
    slot = lax.rem(i, 2)
    e = bexp_ref[i]
    prev = bexp_ref[jnp.maximum(i - 1, 0)]

    def row_copy(tok, slot_, r):
        return pltpu.make_async_copy(x_hbm.at[pl.ds(tok, 1), :], xbuf.at[slot_, pl.ds(r, 1), :], sem.at[slot_])

    def start_block(tok_ref, slot_):
        for r in range(MOE_BLOCK):
            row_copy(tok_ref[0, 0, r], slot_, r).start()

    @pl.when(i == 0)
    def _():
        start_block(rows_ref, 0)

    @pl.when(i + 1 < n_used)
    def _():
        start_block(rows_next_ref, 1 - slot)

    @pl.when((i == 0) | (e != prev))
    def _():
        wgu_bf[...] = wgu_ref[0, 0].astype(bf16)
        wdn_bf[...] = wdn_ref[0, 0].astype(bf16)

    @pl.when(i < n_used)
    def _():
        for r in range(MOE_BLOCK):
            row_copy(0, slot, r).wait()
        h = _dot(xbuf[slot].astype(bf16), wgu_bf[...]) + bgu_ref[0, 0]
        gate = jnp.minimum(h[:, 0:D_FF], SWIGLU_LIMIT)
        up = jnp.clip(h[:, D_FF:2 * D_FF], -SWIGLU_LIMIT, SWIGLU_LIMIT)
        glu = gate * jax.nn.sigmoid(SWIGLU_ALPHA * gate)
        act = ((up + 1.0) * glu).astype(bf16)
        y_ref[...] = _dot(act, wdn_bf[...]) + bdn_ref[0, 0]

    @pl.when(i >= n_used)
    def _():
        y_ref[...] = jnp.zeros(y_ref.shape, f32)


def _moe(block_exp, n_used, row_tok, x, w_gu, b_gu, w_dn, b_dn, layer):
    n_blocks = block_exp.shape[0]
    rows3 = row_tok.reshape(n_blocks, 1, MOE_BLOCK)
    wspec = lambda r, c: pl.BlockSpec((1, 1, r, c), lambda i, be, nu: (layer, be[i], 0, 0))
    rows_spec = lambda ahead: pl.BlockSpec(
        (1, 1, MOE_BLOCK), lambda i, be, nu: (jnp.minimum(i + ahead, n_blocks - 1), 0, 0),
        memory_space=pltpu.SMEM)
    return pl.pallas_call(
        _moe_kernel,
        grid_spec=pltpu.PrefetchScalarGridSpec(
            num_scalar_prefetch=2,
            grid=(n_blocks,),
            in_specs=[rows_spec(0), rows_spec(1), pl.BlockSpec(memory_space=pl.ANY),
                      wspec(D_MODEL, 2 * D_FF), wspec(1, 2 * D_FF),
                      wspec(D_FF, D_MODEL), wspec(1, D_MODEL)],
            out_specs=pl.BlockSpec((MOE_BLOCK, D_MODEL), lambda i, be, nu: (i, 0)),
            scratch_shapes=[pltpu.VMEM((D_MODEL, 2 * D_FF), bf16), pltpu.VMEM((D_FF, D_MODEL), bf16),
                            pltpu.VMEM((2, MOE_BLOCK, D_MODEL), f32), pltpu.SemaphoreType.DMA((2,))],
        ),
        out_shape=jax.ShapeDtypeStruct((n_blocks * MOE_BLOCK, D_MODEL), f32),
        compiler_params=_cparams(("arbitrary",)),
        name="moe_experts",
    )(block_exp, n_used, rows3, rows3, x, w_gu, b_gu, w_dn, b_dn)


def _combine_kernel(x_ref, y_ref, gate_ref, g_ref, b_ref, o_ref):
    gates = gate_ref[...]
    mix = gates[:, 0:1] * y_ref[0]
    for k in range(1, TOP_E):
        mix = mix + gates[:, k:k + 1] * y_ref[k]
    o_ref[...] = _layer_norm(ALPHA * x_ref[...] + mix, g_ref[...], b_ref[...])


def _combine(x1, y_asg, gates, g, b):
    n = x1.shape[0]
    row = lambda c: pl.BlockSpec((ROW_TILE, c), lambda i: (i, 0))
    full = lambda r, c: pl.BlockSpec((r, c), lambda i: (0, 0))
    return pl.pallas_call(
        _combine_kernel,
        grid=(n // ROW_TILE,),
        in_specs=[row(D_MODEL), pl.BlockSpec((TOP_E, ROW_TILE, D_MODEL), lambda i: (0, i, 0)),
                  row(LANES), full(1, D_MODEL), full(1, D_MODEL)],
        out_specs=row(D_MODEL),
        out_shape=jax.ShapeDtypeStruct((n, D_MODEL), f32),
        compiler_params=_cparams(("arbitrary",)),
        name="combine_norm",
    )(x1, y_asg, gates, g, b)


def _route(exp_t, rank_t, counts, n_tok):
    padded = (counts + MOE_BLOCK - 1) // MOE_BLOCK * MOE_BLOCK
    pad_end = jnp.cumsum(padded)
    pad_start = pad_end - padded
    hit = exp_t[:, :, None] == jnp.arange(N_EXPERTS, dtype=i32)
    dest = (jnp.sum(jnp.where(hit, pad_start, 0), axis=-1) + rank_t).reshape(-1)
    n_blocks = -(-n_tok * TOP_E // MOE_BLOCK) + N_EXPERTS
    tok = jnp.tile(jnp.arange(n_tok, dtype=i32), TOP_E)
    row_tok = jnp.zeros((n_blocks * MOE_BLOCK,), i32).at[dest].set(tok)
    block_start = jnp.arange(n_blocks, dtype=i32) * MOE_BLOCK
    block_exp = jnp.minimum(jnp.sum(pad_end[None, :] <= block_start[:, None], axis=1), N_EXPERTS - 1).astype(i32)
    n_used = (pad_end[-1:] // MOE_BLOCK).astype(i32)
    return row_tok, block_exp, n_used, dest


def _pad_cols(w):
    cols = w.shape[1]
    padded = -(-cols // LANES) * LANES
    return jnp.pad(w, ((0, 0), (0, padded - cols))).astype(bf16)


def _diag_heads(o_full, nb, dec_seq):
    o6 = o_full.reshape(nb, dec_seq, N_KV, GROUP, N_KV, HEAD_DIM)
    return jnp.einsum('btkgkd->btkgd', o6).reshape(nb * dec_seq, DQ)


def _block_diag_q(q_s, nb, dec_seq):
    q5 = q_s.reshape(nb, dec_seq, N_KV, GROUP, 1, HEAD_DIM)
    eye = jnp.eye(N_KV, dtype=q_s.dtype).reshape(1, 1, N_KV, 1, N_KV, 1)
    return (q5 * eye).reshape(nb, SROWS, DKV)


def _pad_page(a, nb, dec_seq):
    a3 = a.reshape(nb, dec_seq, a.shape[-1])
    return jnp.pad(a3, ((0, 0), (0, PAGE - dec_seq), (0, 0)))


def kernel(x_prompt, x_sample, cache_dsa_k, cache_dsa_v, cache_dsa_ik, cache_fox_k, cache_fox_v,
           cache_fox_logf, page_table, w_in_dsa, w_out_dsa, w_in_fox, b_forget, w_out_fox,
           ln_gain, ln_bias, w_router, b_router, w_gate_up, b_gate_up, w_down, b_down):
    batch, seq, _ = x_prompt.shape
    nb, dec_seq, _ = x_sample.shape
    n_pages = page_table.shape[1]
    pool = cache_dsa_k.shape[1]
    n_p = batch * seq
    n_s = nb * dec_seq
    n_tok = n_p + n_s
    assert dec_seq * N_HEADS == SROWS and n_s == ROW_TILE and n_p % ROW_TILE == 0
    assert n_pages % PAGES_PER_STEP == 0 and seq % KEY_CHUNK == 0

    x = jnp.concatenate([x_prompt.reshape(n_p, D_MODEL), x_sample.reshape(n_s, D_MODEL)], axis=0)
    pt = page_table.reshape(-1).astype(i32)
    slopes = 2.0 ** (-8.0 * jnp.arange(1, N_HEADS + 1, dtype=f32) / N_HEADS)
    slope_rows = jnp.tile(slopes, dec_seq).reshape(SROWS, 1)

    dsa_k = cache_dsa_k.reshape(-1, pool, PAGE, DKV)
    dsa_v = cache_dsa_v.reshape(-1, pool, PAGE, DKV)
    fox_k = cache_fox_k.reshape(-1, pool, PAGE, DKV)
    fox_v = cache_fox_v.reshape(-1, pool, PAGE, DKV)
    b_gu4 = b_gate_up.reshape(DEPTH, N_EXPERTS, 1, 2 * D_FF)
    b_dn4 = b_down.reshape(DEPTH, N_EXPERTS, 1, D_MODEL)

    outs = {name: [] for name in ("dk", "dv", "di", "fk", "fv", "fl")}
    for layer in range(DEPTH):
        j = layer // 2
        if layer % 2 == 0:
            q, k, v, iq, ik, iw = _inproj_dsa(x, _pad_cols(w_in_dsa[j]))
            o_p = _dsa_prompt(q, iq, iw, ik, k, v, batch, seq)
            iq_rows = jnp.pad(iq[n_p:].reshape(nb, dec_seq * IDX_HEADS, IDX_DIM),
                              ((0, 0), (0, (SQ - dec_seq) * IDX_HEADS), (0, 0)))
            w_rows = jnp.pad(iw[n_p:].reshape(nb, dec_seq * IDX_HEADS, 1),
                             ((0, 0), (0, (SQ - dec_seq) * IDX_HEADS), (0, 0)))
            keys = _dsa_sample_scores(pt, iq_rows, w_rows, cache_dsa_ik, _pad_page(ik[n_p:], nb, dec_seq),
                                      j, n_pages, dec_seq)
            o_s = _sample_attn("dsa", pt, _block_diag_q(q[n_p:], nb, dec_seq), dsa_k, dsa_v,
                               _pad_page(k[n_p:], nb, dec_seq), _pad_page(v[n_p:], nb, dec_seq),
                               keys, slope_rows, j, n_pages, dec_seq)
            w_out = w_out_dsa[j]
            outs["dk"].append(k); outs["dv"].append(v); outs["di"].append(ik)
        else:
            q, k, v, lf = _inproj_fox(x, _pad_cols(w_in_fox[j]), b_forget[j])
            cum = _fox_cum(lf, batch, seq)
            o_p = _fox_prompt(q, cum, k, v, batch, seq)
            o_s = _sample_attn("fox", pt, _block_diag_q(q[n_p:], nb, dec_seq), fox_k, fox_v,
                               _pad_page(k[n_p:], nb, dec_seq), _pad_page(v[n_p:], nb, dec_seq),
                               cache_fox_logf, _pad_page(lf[n_p:], nb, dec_seq), j, n_pages, dec_seq)
            w_out = w_out_fox[j]
            outs["fk"].append(k); outs["fv"].append(v); outs["fl"].append(lf)

        wr = w_router[layer]
        wr_hi = wr.astype(bf16)
        wr_lo = (wr - wr_hi.astype(f32)).astype(bf16)
        x1, gates, exp_t, rank_t, cnt = _outproj(
            x, o_p, _diag_heads(o_s, nb, dec_seq), w_out.astype(bf16),
            ln_gain[layer, 0].reshape(1, D_MODEL), ln_bias[layer, 0].reshape(1, D_MODEL),
            wr_hi, wr_lo, b_router[layer].reshape(1, N_EXPERTS))

        row_tok, block_exp, n_used, dest = _route(exp_t[:TOP_E], rank_t[:TOP_E], cnt[:, 0], n_tok)
        y_rows = _moe(block_exp, n_used, row_tok, x1, w_gate_up, b_gu4, w_down, b_dn4, layer)
        y_asg = y_rows[dest].reshape(TOP_E, n_tok, D_MODEL)
        x = _combine(x1, y_asg, gates, ln_gain[layer, 1].reshape(1, D_MODEL),
                     ln_bias[layer, 1].reshape(1, D_MODEL))

    def stack(name, shape_p, shape_s):
        a = jnp.stack(outs[name])
        return a[:, :n_p].reshape((-1,) + shape_p), a[:, n_p:].reshape((-1,) + shape_s)

    kv_p, kv_s = (batch, seq, N_KV, HEAD_DIM), (nb, dec_seq, N_KV, HEAD_DIM)
    dk_p, dk_s = stack("dk", kv_p, kv_s)
    dv_p, dv_s = stack("dv", kv_p, kv_s)
    di_p, di_s = stack("di", (batch, seq, IDX_DIM), (nb, dec_seq, IDX_DIM))
    fk_p, fk_s = stack("fk", kv_p, kv_s)
    fv_p, fv_s = stack("fv", kv_p, kv_s)
    fl_p, fl_s = stack("fl", (batch, seq, N_HEADS), (nb, dec_seq, N_HEADS))
    return (x[:n_p].reshape(batch, seq, D_MODEL), x[n_p:].reshape(nb, dec_seq, D_MODEL),
            dk_p, dv_p, di_p, fk_p, fv_p, fl_p, dk_s, dv_s, di_s, fk_s, fv_s, fl_s)
```

```python
import functools

import jax
import jax.numpy as jnp
from jax import lax
from jax.experimental import pallas as pl
from jax.experimental.pallas import tpu as pltpu

f32 = jnp.float32
bf16 = jnp.bfloat16
i32 = jnp.int32

D_MODEL = 1024
DEPTH = 4
PAGE = 128
N_HEADS = 16
HEAD_DIM = 64
N_KV = 4
GROUP = N_HEADS // N_KV
DQ = N_HEADS * HEAD_DIM
DKV = N_KV * HEAD_DIM
IDX_HEADS = 8
IDX_DIM = 64
TOPK = 256
N_EXPERTS = 32
TOP_E = 4
D_FF = D_MODEL
SWIGLU_LIMIT = 7.0
SWIGLU_ALPHA = 1.702
MOE_BLOCK = 128
LN_EPS = 1e-5
ALPHA = (2 * DEPTH) ** 0.25
IDX_SCALE = IDX_HEADS ** -0.5 * IDX_DIM ** -0.5
QK_SCALE = HEAD_DIM ** -0.5

LANES = 128
SUBLANES = 8
Q_TILE = 128
KEY_CHUNK = 256
ROW_TILE = 128
PAGES_PER_STEP = 16
INT_MIN = -2 ** 31
NEG = -1e30
VMEM_LIMIT = 56 * 1024 * 1024

NT = (((1,), (1,)), ((), ()))


def _cparams(sem):
    return pltpu.CompilerParams(dimension_semantics=sem, vmem_limit_bytes=VMEM_LIMIT)


def _split3(x):
    hi = x.astype(bf16)
    r1 = x - hi.astype(f32)
    mid = r1.astype(bf16)
    lo = (r1 - mid.astype(f32)).astype(bf16)
    return hi, mid, lo


def _dot(a, b):
    return jnp.dot(a, b, preferred_element_type=f32)


def _dot_nt(a, b):
    return lax.dot_general(a, b, NT, preferred_element_type=f32)


def _dot3(x, w_bf):
    hi, mid, lo = _split3(x)
    return _dot(hi, w_bf) + _dot(mid, w_bf) + _dot(lo, w_bf)


def _dot3_tn(x, w_bf):
    tn = (((0,), (0,)), ((), ()))
    return sum(lax.dot_general(part, w_bf, tn, preferred_element_type=f32) for part in _split3(x))


def _layer_norm(y, g, b):
    mu = jnp.mean(y, axis=-1, keepdims=True)
    d = y - mu
    var = jnp.mean(d * d, axis=-1, keepdims=True)
    return d * lax.rsqrt(var + LN_EPS) * g + b


def _order_key(s):
    s = jnp.where(s == 0.0, 0.0, s)
    bits = pltpu.bitcast(s, i32)
    return bits ^ ((bits >> 31) & 0x7FFFFFFF)


def _upper_tri(n, strict=False):
    r = lax.broadcasted_iota(i32, (n, n), 0)
    c = lax.broadcasted_iota(i32, (n, n), 1)
    return jnp.where((r < c) if strict else (r <= c), 1.0, 0.0).astype(bf16)


def _inproj_dsa_kernel(x_ref, w_ref, q_ref, k_ref, v_ref, iq_ref, ik_ref, iw_ref):
    xb = x_ref[...].astype(bf16)
    q_ref[...] = _dot(xb, w_ref[:, 0:DQ])
    k_ref[...] = _dot(xb, w_ref[:, DQ:DQ + DKV])
    v_ref[...] = _dot(xb, w_ref[:, DQ + DKV:DQ + 2 * DKV])
    o = DQ + 2 * DKV
    iq_ref[...] = _dot(xb, w_ref[:, o:o + IDX_HEADS * IDX_DIM])
    o += IDX_HEADS * IDX_DIM
    tail = _dot(xb, w_ref[:, o:o + LANES])
    ik_ref[...] = tail[:, 0:IDX_DIM]
    iw_ref[...] = tail[:, IDX_DIM:IDX_DIM + IDX_HEADS]


def _inproj_dsa(x, w_pad):
    n = x.shape[0]
    cols = w_pad.shape[1]
    row = lambda c: pl.BlockSpec((ROW_TILE, c), lambda i: (i, 0))
    widths = (DQ, DKV, DKV, IDX_HEADS * IDX_DIM, IDX_DIM, IDX_HEADS)
    return pl.pallas_call(
        _inproj_dsa_kernel,
        grid=(n // ROW_TILE,),
        in_specs=[row(D_MODEL), pl.BlockSpec((D_MODEL, cols), lambda i: (0, 0))],
        out_specs=[row(c) for c in widths],
        out_shape=[jax.ShapeDtypeStruct((n, c), f32) for c in widths],
        compiler_params=_cparams(("arbitrary",)),
        name="inproj_dsa",
    )(x, w_pad)


def _inproj_fox_kernel(x_ref, w_ref, bf_ref, q_ref, k_ref, v_ref, lf_ref):
    xb = x_ref[...].astype(bf16)
    q_ref[...] = _dot(xb, w_ref[:, 0:DQ])
    k_ref[...] = _dot(xb, w_ref[:, DQ:DQ + DKV])
    v_ref[...] = _dot(xb, w_ref[:, DQ + DKV:DQ + 2 * DKV])
    o = DQ + 2 * DKV
    fg = _dot(xb, w_ref[:, o:o + LANES])[:, 0:N_HEADS] + bf_ref[...]
    lf_ref[...] = jax.nn.log_sigmoid(fg)


def _inproj_fox(x, w_pad, b_f):
    n = x.shape[0]
    cols = w_pad.shape[1]
    row = lambda c: pl.BlockSpec((ROW_TILE, c), lambda i: (i, 0))
    widths = (DQ, DKV, DKV, N_HEADS)
    return pl.pallas_call(
        _inproj_fox_kernel,
        grid=(n // ROW_TILE,),
        in_specs=[row(D_MODEL), pl.BlockSpec((D_MODEL, cols), lambda i: (0, 0)),
                  pl.BlockSpec((1, N_HEADS), lambda i: (0, 0))],
        out_specs=[row(c) for c in widths],
        out_shape=[jax.ShapeDtypeStruct((n, c), f32) for c in widths],
        compiler_params=_cparams(("arbitrary",)),
        name="inproj_fox",
    )(x, w_pad, b_f.reshape(1, N_HEADS))


QROWS = GROUP * Q_TILE


def _flash_block(q_ref, kbf, vbf, o_ref, qs, m_sc, acc_sc, n_chunks, bias_fn):
    for h in range(N_HEADS):
        g = h % GROUP
        qs[h // GROUP, g * Q_TILE:(g + 1) * Q_TILE, :] = (
            q_ref[:, h * HEAD_DIM:(h + 1) * HEAD_DIM] * QK_SCALE).astype(bf16)
    m_sc[...] = jnp.full(m_sc.shape, NEG, f32)
    acc_sc[...] = jnp.zeros(acc_sc.shape, f32)

    def body(c, _):
        off = pl.multiple_of(c * KEY_CHUNK, KEY_CHUNK)
        for kh in range(N_KV):
            kc = kbf[kh, pl.ds(off, KEY_CHUNK), :]
            vc = vbf[kh, pl.ds(off, KEY_CHUNK), :]
            lg = _dot_nt(qs[kh], kc) + bias_fn(c, kh)
            m = m_sc[kh]
            part = lg[:, 0:LANES]
            for j in range(1, KEY_CHUNK // LANES):
                part = jnp.maximum(part, lg[:, j * LANES:(j + 1) * LANES])
            m_new = jnp.maximum(m, jnp.max(part, axis=1, keepdims=True))
            p = jnp.exp(lg - jnp.concatenate([m_new] * (KEY_CHUNK // LANES), axis=1))
            acc_sc[kh] = jnp.exp(m - m_new) * acc_sc[kh] + _dot(p.astype(bf16), vc)
            m_sc[kh] = m_new
        return 0

    lax.fori_loop(0, n_chunks, body, 0)
    for h in range(N_HEADS):
        kh, g = h // GROUP, h % GROUP
        acc = acc_sc[kh, g * Q_TILE:(g + 1) * Q_TILE, :]
        o_ref[:, h * HEAD_DIM:(h + 1) * HEAD_DIM] = acc[:, 0:HEAD_DIM] / acc[:, HEAD_DIM:2 * HEAD_DIM]


_FLASH_SCRATCH = [pltpu.VMEM((N_KV, QROWS, HEAD_DIM), bf16),
                  pltpu.VMEM((N_KV, QROWS, LANES), f32),
                  pltpu.VMEM((N_KV, QROWS, 2 * HEAD_DIM), f32)]


def _stage_kv(k_ref, v_ref, kbf, vbf):
    for h in range(N_KV):
        kbf[h] = k_ref[:, h * HEAD_DIM:(h + 1) * HEAD_DIM].astype(bf16)
        vbf[h, :, 0:HEAD_DIM] = v_ref[:, h * HEAD_DIM:(h + 1) * HEAD_DIM].astype(bf16)
        vbf[h, :, HEAD_DIM:2 * HEAD_DIM] = jnp.ones((v_ref.shape[0], HEAD_DIM), bf16)


def _dsa_prompt_kernel(q_ref, iq_ref, iw_ref, ik_ref, k_ref, v_ref, o_ref,
                       ikbf, kbf, vbf, key3, mb3, dist3, qs, m_sc, acc_sc):
    i = pl.program_id(1)

    @pl.when(i == 0)
    def _():
        ikbf[...] = ik_ref[...].astype(bf16)
        _stage_kv(k_ref, v_ref, kbf, vbf)

    n_chunks = (i * Q_TILE + Q_TILE + KEY_CHUNK - 1) // KEY_CHUNK
    row = lax.broadcasted_iota(i32, (Q_TILE, KEY_CHUNK), 0)
    col = lax.broadcasted_iota(i32, (Q_TILE, KEY_CHUNK), 1)
    qpos = i * Q_TILE + row
    w = iw_ref[...] * IDX_SCALE
    w_rep = [jnp.broadcast_to(w[:, h:h + 1], (Q_TILE, KEY_CHUNK)) for h in range(IDX_HEADS)]
    iqb = iq_ref[...].astype(bf16)

    def score_body(c, _):
        ikc = ikbf[pl.ds(pl.multiple_of(c * KEY_CHUNK, KEY_CHUNK), KEY_CHUNK), :]
        s = jnp.zeros((Q_TILE, KEY_CHUNK), f32)
        for h in range(IDX_HEADS):
            rel = _dot_nt(iqb[:, h * IDX_DIM:(h + 1) * IDX_DIM], ikc)
            s = s + w_rep[h] * jnp.maximum(rel, 0.0)
        d = qpos - (c * KEY_CHUNK + col)
        key3[c] = jnp.where(d >= 0, _order_key(s), INT_MIN)
        dist3[c] = d.astype(f32)
        return 0

    lax.fori_loop(0, n_chunks, score_body, 0)

    def count(pred):
        def body(c, acc):
            return acc + jnp.where(pred(key3[c]), 1.0, 0.0)
        acc = lax.fori_loop(0, n_chunks, body, jnp.zeros((Q_TILE, KEY_CHUNK), f32))
        return jnp.sum(acc, axis=1, keepdims=True)

    def bit_body(it, t):
        cand = t ^ lax.shift_left(jnp.int32(1), 31 - it)
        return jnp.where(count(lambda kk: kk >= cand) >= TOPK, cand, t)

    thr = lax.fori_loop(0, 32, bit_body, jnp.full((Q_TILE, 1), INT_MIN, i32))
    thr = jnp.maximum(thr, INT_MIN + 1)
    need = TOPK - count(lambda kk: kk > thr)
    n_tie = count(lambda kk: kk == thr)
    has_excess = jnp.max(n_tie - need) > 0.0

    @pl.when(jnp.logical_not(has_excess))
    def _():
        def body(c, _):
            mb3[c] = jnp.where(key3[c] >= thr, 0.0, NEG)
            return 0
        lax.fori_loop(0, n_chunks, body, 0)

    @pl.when(has_excess)
    def _():
        tri = _upper_tri(KEY_CHUNK)

        def body(c, before):
            kk = key3[c]
            tie = jnp.where(kk == thr, 1.0, 0.0)
            rank = before + _dot(tie.astype(bf16), tri)
            sel = (kk > thr) | ((kk == thr) & (rank <= need))
            mb3[c] = jnp.where(sel, 0.0, NEG)
            return before + jnp.sum(tie, axis=1, keepdims=True)
        lax.fori_loop(0, n_chunks, body, jnp.zeros((Q_TILE, 1), f32))

    def bias(c, kh):
        mb = mb3[c]
        d = dist3[c]
        slabs = [mb - float(2.0 ** (-8.0 * (kh * GROUP + g + 1) / N_HEADS)) * d for g in range(GROUP)]
        return jnp.concatenate(slabs, axis=0)

    _flash_block(q_ref, kbf, vbf, o_ref, qs, m_sc, acc_sc, n_chunks, bias)


def _dsa_prompt(q, iq, iw, ik, k, v, batch, seq):
    nq = seq // Q_TILE
    nc = seq // KEY_CHUNK
    qspec = lambda c: pl.BlockSpec((Q_TILE, c), lambda b, i: (b * nq + i, 0))
    bspec = lambda c: pl.BlockSpec((seq, c), lambda b, i: (b, 0))
    return pl.pallas_call(
        _dsa_prompt_kernel,
        grid=(batch, nq),
        in_specs=[qspec(DQ), qspec(IDX_HEADS * IDX_DIM), qspec(IDX_HEADS),
                  bspec(IDX_DIM), bspec(DKV), bspec(DKV)],
        out_specs=qspec(DQ),
        out_shape=jax.ShapeDtypeStruct((batch * seq, DQ), f32),
        scratch_shapes=[pltpu.VMEM((seq, IDX_DIM), bf16),
                        pltpu.VMEM((N_KV, seq, HEAD_DIM), bf16),
                        pltpu.VMEM((N_KV, seq, 2 * HEAD_DIM), bf16),
                        pltpu.VMEM((nc, Q_TILE, KEY_CHUNK), i32),
                        pltpu.VMEM((nc, Q_TILE, KEY_CHUNK), f32),
                        pltpu.VMEM((nc, Q_TILE, KEY_CHUNK), f32)] + _FLASH_SCRATCH,
        compiler_params=_cparams(("arbitrary", "arbitrary")),
        name="dsa_prompt",
    )(q, iq, iw, ik, k, v)


def _cum_kernel(lf_ref, o_ref):
    tri = _upper_tri(KEY_CHUNK)
    carry = jnp.zeros((N_HEADS, 1), f32)
    for c in range(lf_ref.shape[0] // KEY_CHUNK):
        cum = _dot3_tn(lf_ref[c * KEY_CHUNK:(c + 1) * KEY_CHUNK, :], tri) + carry
        o_ref[0, c] = cum
        carry = cum[:, KEY_CHUNK - 1:KEY_CHUNK]


def _fox_cum(lf, batch, seq):
    nc = seq // KEY_CHUNK
    return pl.pallas_call(
        _cum_kernel,
        grid=(batch,),
        in_specs=[pl.BlockSpec((seq, N_HEADS), lambda b: (b, 0))],
        out_specs=pl.BlockSpec((1, nc, N_HEADS, KEY_CHUNK), lambda b: (b, 0, 0, 0)),
        out_shape=jax.ShapeDtypeStruct((batch, nc, N_HEADS, KEY_CHUNK), f32),
        compiler_params=_cparams(("arbitrary",)),
        name="fox_cum",
    )(lf)


def _fox_prompt_kernel(q_ref, cum_ref, k_ref, v_ref, o_ref, kbf, vbf, qs, m_sc, acc_sc):
    i = pl.program_id(1)

    @pl.when(i == 0)
    def _():
        _stage_kv(k_ref, v_ref, kbf, vbf)

    n_chunks = (i * Q_TILE + Q_TILE + KEY_CHUNK - 1) // KEY_CHUNK
    row = lax.broadcasted_iota(i32, (Q_TILE, KEY_CHUNK), 0)
    col = lax.broadcasted_iota(i32, (Q_TILE, KEY_CHUNK), 1)
    rel = i * Q_TILE + row - col

    def bias(c, kh):
        ck = cum_ref[0, c]
        causal = rel - c * KEY_CHUNK >= 0
        slabs = [jnp.where(causal, -ck[kh * GROUP + g:kh * GROUP + g + 1, :], NEG) for g in range(GROUP)]
        return jnp.concatenate(slabs, axis=0)

    _flash_block(q_ref, kbf, vbf, o_ref, qs, m_sc, acc_sc, n_chunks, bias)


def _fox_prompt(q, cum, k, v, batch, seq):
    nq = seq // Q_TILE
    nc = seq // KEY_CHUNK
    qspec = lambda c: pl.BlockSpec((Q_TILE, c), lambda b, i: (b * nq + i, 0))
    bspec = lambda c: pl.BlockSpec((seq, c), lambda b, i: (b, 0))
    return pl.pallas_call(
        _fox_prompt_kernel,
        grid=(batch, nq),
        in_specs=[qspec(DQ), pl.BlockSpec((1, nc, N_HEADS, KEY_CHUNK), lambda b, i: (b, 0, 0, 0)),
                  bspec(DKV), bspec(DKV)],
        out_specs=qspec(DQ),
        out_shape=jax.ShapeDtypeStruct((batch * seq, DQ), f32),
        scratch_shapes=[pltpu.VMEM((N_KV, seq, HEAD_DIM), bf16),
                        pltpu.VMEM((N_KV, seq, 2 * HEAD_DIM), bf16)] + _FLASH_SCRATCH,
        compiler_params=_cparams(("arbitrary", "arbitrary")),
        name="fox_prompt",
    )(q, cum, k, v)


SROWS = 64
SQ = SUBLANES


def _page_specs(shape, layer, n_pages, n_steps, group):
    def spec(j):
        def index(b, s, pt):
            return (layer, pt[b * n_pages + jnp.minimum(s, n_steps - 1) * group + j], 0, 0)
        return pl.BlockSpec((1, 1) + shape, index)
    return [spec(j) for j in range(group)]


def _dsa_sample_score_kernel(pt_ref, iq_ref, w_ref, *rest, n_steps, dec_seq, group):
    ikc_refs, ikn_ref, key_ref = rest[:group], rest[group], rest[group + 1]
    s_id = pl.program_id(1)
    iqb = iq_ref[0].astype(bf16)
    wcol = w_ref[0] * IDX_SCALE
    t = lax.broadcasted_iota(i32, (SQ, PAGE), 0)
    j = lax.broadcasted_iota(i32, (SQ, PAGE), 1)

    def score(ik):
        rel = _dot_nt(iqb, ik.astype(bf16))
        return jnp.sum((jnp.maximum(rel, 0.0) * wcol).reshape(SQ, IDX_HEADS, PAGE), axis=1)

    @pl.when(s_id < n_steps)
    def _():
        for g in range(group):
            key_ref[0, 0, g] = jnp.where(t < dec_seq, _order_key(score(ikc_refs[g][0, 0])), INT_MIN)

    @pl.when(s_id == n_steps)
    def _():
        ok = (t < dec_seq) & (j <= t)
        key_ref[0, 0, 0] = jnp.where(ok, _order_key(score(ikn_ref[0])), INT_MIN)
        for g in range(1, group):
            key_ref[0, 0, g] = jnp.full((SQ, PAGE), INT_MIN, i32)


def _dsa_sample_scores(pt, iq_rows, w_rows, cache_ik, ik_new, layer, n_pages, dec_seq):
    nb = iq_rows.shape[0]
    group = PAGES_PER_STEP
    n_steps = n_pages // group
    kern = functools.partial(_dsa_sample_score_kernel, n_steps=n_steps, dec_seq=dec_seq, group=group)
    per_b = lambda b, s, pt: (b, 0, 0)
    return pl.pallas_call(
        kern,
        grid_spec=pltpu.PrefetchScalarGridSpec(
            num_scalar_prefetch=1,
            grid=(nb, n_steps + 1),
            in_specs=[pl.BlockSpec((1, SQ * IDX_HEADS, IDX_DIM), per_b),
                      pl.BlockSpec((1, SQ * IDX_HEADS, 1), per_b)]
                     + _page_specs((PAGE, IDX_DIM), layer, n_pages, n_steps, group)
                     + [pl.BlockSpec((1, PAGE, IDX_DIM), per_b)],
            out_specs=pl.BlockSpec((1, 1, group, SQ, PAGE), lambda b, s, pt: (b, s, 0, 0, 0)),
        ),
        out_shape=jax.ShapeDtypeStruct((nb, n_steps + 1, group, SQ, PAGE), i32),
        compiler_params=_cparams(("arbitrary", "arbitrary")),
        name="dsa_sample_scores",
    )(pt, iq_rows, w_rows, *([cache_ik] * group), ik_new)


def _sample_attn_kernel(pt_ref, q_ref, *rest, mode, n_pages, n_steps, dec_seq, group):
    kc_refs, vc_refs = rest[:group], rest[group:2 * group]
    rest = rest[2 * group:]
    if mode == "dsa":
        keys_ref, kn_ref, vn_ref, slope_ref, o_ref = rest[:5]
        m_sc, l_sc, acc_sc, carry_sc, thr_sc, need_sc = rest[5:]
    else:
        lf_refs, (kn_ref, vn_ref, lfn_ref, o_ref) = rest[:group], rest[group:group + 4]
        m_sc, l_sc, acc_sc, carry_sc = rest[group + 4:]
    s_id = pl.program_id(1)
    row = lax.broadcasted_iota(i32, (SROWS, PAGE), 0)
    col = lax.broadcasted_iota(i32, (SROWS, PAGE), 1)
    tok = row // N_HEADS
    qb = (q_ref[0] * QK_SCALE).astype(bf16)

    @pl.when(s_id == 0)
    def _():
        m_sc[...] = jnp.full(m_sc.shape, NEG, f32)
        l_sc[...] = jnp.zeros(l_sc.shape, f32)
        acc_sc[...] = jnp.zeros(acc_sc.shape, f32)
        carry_sc[...] = jnp.zeros(carry_sc.shape, f32)

    def update(ks, vs, biases):
        lg = jnp.concatenate([_dot_nt(qb, k.astype(bf16)) + b for k, b in zip(ks, biases)], axis=1)
        m = m_sc[...]
        m_new = jnp.maximum(m, jnp.max(lg, axis=1, keepdims=True))
        a = jnp.exp(m - m_new)
        p = jnp.exp(lg - m_new)
        l_sc[...] = a * l_sc[...] + jnp.sum(p, axis=1, keepdims=True)
        pr = p.astype(bf16)
        pv = _dot(pr[:, 0:PAGE], vs[0].astype(bf16))
        for g in range(1, len(vs)):
            pv = pv + _dot(pr[:, g * PAGE:(g + 1) * PAGE], vs[g].astype(bf16))
        acc_sc[...] = a * acc_sc[...] + pv
        m_sc[...] = m_new

    if mode == "dsa":
        tri = _upper_tri(PAGE)

        @pl.when(s_id == 0)
        def _():
            kk = keys_ref[0].reshape((n_steps + 1) * group, SQ, PAGE)

            def count(pred):
                return jnp.sum(jnp.sum(jnp.where(pred(kk), 1.0, 0.0), axis=0), axis=1, keepdims=True)

            def bit_body(it, t):
                cand = t ^ lax.shift_left(jnp.int32(1), 31 - it)
                return jnp.where(count(lambda a: a >= cand[None]) >= TOPK, cand, t)

            thr = lax.fori_loop(0, 32, bit_body, jnp.full((SQ, 1), INT_MIN, i32))
            thr = jnp.maximum(thr, INT_MIN + 1)
            thr_sc[...] = thr
            need_sc[...] = TOPK - count(lambda a: a > thr[None])

        def dsa_biases(kps, first_page):
            thr, need = thr_sc[...], need_sc[...]
            before = carry_sc[0:SQ, :]
            out = []
            for g, kp in enumerate(kps):
                tie = jnp.where(kp == thr, 1.0, 0.0)
                rank = before + _dot(tie.astype(bf16), tri)
                sel = jnp.where((kp > thr) | ((kp == thr) & (rank <= need)), 1.0, 0.0)
                before = before + jnp.sum(tie, axis=1, keepdims=True)
                sel_rows = jnp.concatenate(
                    [jnp.broadcast_to(sel[t:t + 1, :], (N_HEADS, PAGE)) for t in range(dec_seq)], axis=0)
                dist = (n_pages * PAGE + tok - ((first_page + g) * PAGE + col)).astype(f32)
                out.append(jnp.where(sel_rows > 0.5, -slope_ref[...] * dist, NEG))
            carry_sc[0:SQ, :] = before
            return out

        @pl.when(s_id < n_steps)
        def _():
            kps = [keys_ref[0, s_id, g] for g in range(group)]
            update([r[0, 0] for r in kc_refs], [r[0, 0] for r in vc_refs], dsa_biases(kps, s_id * group))

        @pl.when(s_id == n_steps)
        def _():
            update([kn_ref[0]], [vn_ref[0]], dsa_biases([keys_ref[0, n_steps, 0]], n_pages))
            o_ref[0] = acc_sc[...] / l_sc[...]
    else:
        tri = _upper_tri(PAGE)

        def fox_biases(lfs, mask_new):
            before = carry_sc[...]
            out = []
            for lf in lfs:
                cum = before + _dot3_tn(lf, tri)
                before = cum[:, PAGE - 1:PAGE]
                ck = jnp.concatenate([cum] * dec_seq, axis=0)
                out.append(jnp.where(col <= tok, -ck, NEG) if mask_new else -ck)
            carry_sc[...] = before
            return out

        @pl.when(s_id < n_steps)
        def _():
            update([r[0, 0] for r in kc_refs], [r[0, 0] for r in vc_refs],
                   fox_biases([r[0, 0] for r in lf_refs], False))

        @pl.when(s_id == n_steps)
        def _():
            update([kn_ref[0]], [vn_ref[0]], fox_biases([lfn_ref[0]], True))
            o_ref[0] = acc_sc[...] / l_sc[...]


def _sample_attn(mode, pt, q_rows, cache_k, cache_v, k_new, v_new, aux, aux_new, layer, n_pages, dec_seq):
    nb = q_rows.shape[0]
    group = PAGES_PER_STEP
    n_steps = n_pages // group
    kern = functools.partial(_sample_attn_kernel, mode=mode, n_pages=n_pages, n_steps=n_steps,
                             dec_seq=dec_seq, group=group)
    per_b = lambda b, s, pt: (b, 0, 0)
    kv_pages = _page_specs((PAGE, DKV), layer, n_pages, n_steps, group)
    new_page = pl.BlockSpec((1, PAGE, DKV), per_b)
    scratch = [pltpu.VMEM((SROWS, 1), f32), pltpu.VMEM((SROWS, 1), f32), pltpu.VMEM((SROWS, DKV), f32),
               pltpu.VMEM((N_HEADS, 1), f32)]
    if mode == "dsa":
        mid_specs = [pl.BlockSpec((1, n_steps + 1, group, SQ, PAGE), lambda b, s, pt: (b, 0, 0, 0, 0)),
                     new_page, new_page, pl.BlockSpec((SROWS, 1), lambda b, s, pt: (0, 0))]
        mid_args = [aux, k_new, v_new, aux_new]
        scratch += [pltpu.VMEM((SQ, 1), i32), pltpu.VMEM((SQ, 1), f32)]
    else:
        mid_specs = (_page_specs((PAGE, N_HEADS), layer, n_pages, n_steps, group)
                     + [new_page, new_page, pl.BlockSpec((1, PAGE, N_HEADS), per_b)])
        mid_args = [aux] * group + [k_new, v_new, aux_new]
    return pl.pallas_call(
        kern,
        grid_spec=pltpu.PrefetchScalarGridSpec(
            num_scalar_prefetch=1,
            grid=(nb, n_steps + 1),
            in_specs=[pl.BlockSpec((1, SROWS, DKV), per_b)] + kv_pages + kv_pages + mid_specs,
            out_specs=pl.BlockSpec((1, SROWS, DKV), per_b),
            scratch_shapes=scratch,
        ),
        out_shape=jax.ShapeDtypeStruct((nb, SROWS, DKV), f32),
        compiler_params=_cparams(("arbitrary", "arbitrary")),
        name=mode + "_sample_attn",
    )(pt, q_rows, *([cache_k] * group), *([cache_v] * group), *mid_args)


def _outproj_kernel(x_ref, op_ref, os_ref, w_ref, g_ref, b_ref, wr_hi_ref, wr_lo_ref, br_ref,
                    x1_ref, gate_ref, exp_t_ref, rank_t_ref, cnt_ref, run_sc, *, n_prompt_tiles):
    i = pl.program_id(0)
    o = jnp.where(i < n_prompt_tiles, op_ref[...], os_ref[...]).astype(bf16)
    x1 = _layer_norm(ALPHA * x_ref[...] + _dot(o, w_ref[...]), g_ref[...], b_ref[...])
    x1_ref[...] = x1

    hi = x1.astype(bf16)
    lo = (x1 - hi.astype(f32)).astype(bf16)
    lg = _dot(hi, wr_hi_ref[...]) + (_dot(lo, wr_hi_ref[...]) + _dot(hi, wr_lo_ref[...])) + br_ref[...]
    lane = lax.broadcasted_iota(i32, lg.shape, 1)
    out_lane = lax.broadcasted_iota(i32, (ROW_TILE, LANES), 1)
    vals = []
    experts = jnp.zeros((ROW_TILE, LANES), i32)
    for k in range(TOP_E):
        m = jnp.max(lg, axis=1, keepdims=True)
        idx = jnp.min(jnp.where(lg == m, lane, N_EXPERTS), axis=1, keepdims=True)
        vals.append(m)
        experts = jnp.where(out_lane == k, idx, experts)
        lg = jnp.where(lane == idx, -jnp.inf, lg)
    es = [jnp.exp(v - vals[0]) for v in vals]
    tot = es[0] + es[1] + es[2] + es[3]
    gates = jnp.zeros((ROW_TILE, LANES), f32)
    for k in range(TOP_E):
        gates = jnp.where(out_lane == k, es[k] / tot, gates)
    gate_ref[...] = gates

    @pl.when(i == 0)
    def _():
        run_sc[...] = jnp.zeros(run_sc.shape, f32)

    exp_t = experts.T[0:SUBLANES, :]
    e_iota = lax.broadcasted_iota(i32, (N_EXPERTS, ROW_TILE), 0)
    hits = [e_iota == exp_t[k:k + 1, :] for k in range(TOP_E)]
    member = jnp.where(hits[0] | hits[1] | hits[2] | hits[3], 1.0, 0.0)
    pos = run_sc[...] + _dot(member.astype(bf16), _upper_tri(ROW_TILE, strict=True))
    sub = lax.broadcasted_iota(i32, (SUBLANES, ROW_TILE), 0)
    rank_t = jnp.zeros((SUBLANES, ROW_TILE), f32)
    for k in range(TOP_E):
        rank_t = jnp.where(sub == k, jnp.sum(jnp.where(hits[k], pos, 0.0), axis=0, keepdims=True), rank_t)
    run_sc[...] = run_sc[...] + jnp.sum(member, axis=1, keepdims=True)
    exp_t_ref[...] = exp_t
    rank_t_ref[...] = rank_t.astype(i32)
    cnt_ref[...] = jnp.broadcast_to(run_sc[...], cnt_ref.shape).astype(i32)


def _outproj(x, o_prompt, o_sample, w_bf, g, b, wr_hi, wr_lo, br):
    n = x.shape[0]
    npt = o_prompt.shape[0] // ROW_TILE
    nst = o_sample.shape[0] // ROW_TILE
    row = lambda c: pl.BlockSpec((ROW_TILE, c), lambda i: (i, 0))
    col = pl.BlockSpec((SUBLANES, ROW_TILE), lambda i: (0, i))
    full = lambda r, c: pl.BlockSpec((r, c), lambda i: (0, 0))
    kern = functools.partial(_outproj_kernel, n_prompt_tiles=npt)
    return pl.pallas_call(
        kern,
        grid=(n // ROW_TILE,),
        in_specs=[row(D_MODEL),
                  pl.BlockSpec((ROW_TILE, DQ), lambda i: (jnp.minimum(i, npt - 1), 0)),
                  pl.BlockSpec((ROW_TILE, DQ), lambda i: (jnp.clip(i - npt, 0, nst - 1), 0)),
                  full(DQ, D_MODEL), full(1, D_MODEL), full(1, D_MODEL),
                  full(D_MODEL, N_EXPERTS), full(D_MODEL, N_EXPERTS), full(1, N_EXPERTS)],
        out_specs=[row(D_MODEL), row(LANES), col, col, full(N_EXPERTS, LANES)],
        out_shape=[jax.ShapeDtypeStruct((n, D_MODEL), f32),
                   jax.ShapeDtypeStruct((n, LANES), f32),
                   jax.ShapeDtypeStruct((SUBLANES, n), i32), jax.ShapeDtypeStruct((SUBLANES, n), i32),
                   jax.ShapeDtypeStruct((N_EXPERTS, LANES), i32)],
        scratch_shapes=[pltpu.VMEM((N_EXPERTS, 1), f32)],
        compiler_params=_cparams(("arbitrary",)),
        name="outproj_router",
    )(x, o_prompt, o_sample, w_bf, g, b, wr_hi, wr_lo, br)


def _moe_kernel(bexp_ref, used_ref, rows_ref, rows_next_ref, x_hbm, wgu_ref, bgu_ref, wdn_ref, bdn_ref,
                y_ref, wgu_bf, wdn_bf, xbuf, sem):
    i = pl.program_id(0)
    n_used = used_ref[0]
    slot = lax.rem(i, 2)
    e = bexp_ref[i]
    prev = bexp_ref[jnp.maximum(i - 1, 0)]

    def row_copy(tok, slot_, r):
        return pltpu.make_async_copy(x_hbm.at[pl.ds(tok, 1), :], xbuf.at[slot_, pl.ds(r, 1), :], sem.at[slot_])

    def start_block(tok_ref, slot_):
        for r in range(MOE_BLOCK):
            row_copy(tok_ref[0, 0, r], slot_, r).start()

    def wait_block():
        for r in range(MOE_BLOCK):
            row_copy(0, slot, r).wait()

    @pl.when(i == 0)
    def _():
        start_block(rows_ref, 0)

    @pl.when((i == 0) | (e != prev))
    def _():
        wgu_bf[...] = wgu_ref[0, 0].astype(bf16)
        wdn_bf[...] = wdn_ref[0, 0].astype(bf16)

    @pl.when(i < n_used)
    def _():
        wait_block()
        start_block(rows_next_ref, 1 - slot)
        h = _dot(xbuf[slot].astype(bf16), wgu_bf[...]) + bgu_ref[0, 0]
        gate = jnp.minimum(h[:, 0:D_FF], SWIGLU_LIMIT)
        up = jnp.clip(h[:, D_FF:2 * D_FF], -SWIGLU_LIMIT, SWIGLU_LIMIT)
        glu = gate * jax.nn.sigmoid(SWIGLU_ALPHA * gate)
        act = ((up + 1.0) * glu).astype(bf16)
        y_ref[...] = _dot(act, wdn_bf[...]) + bdn_ref[0, 0]

    @pl.when(i == n_used)
    def _():
        wait_block()

    @pl.when(i >= n_used)
    def _():
        y_ref[...] = jnp.zeros(y_ref.shape, f32)


def _moe(block_exp, n_used, row_tok, x, w_gu, b_gu, w_dn, b_dn, layer):
    n_blocks = block_exp.shape[0]
    rows3 = row_tok.reshape(n_blocks, 1, MOE_BLOCK)
    wspec = lambda r, c: pl.BlockSpec((1, 1, r, c), lambda i, be, nu: (layer, be[i], 0, 0))
    rows_spec = lambda ahead: pl.BlockSpec(
        (1, 1, MOE_BLOCK), lambda i, be, nu: (jnp.minimum(i + ahead, n_blocks - 1), 0, 0),
        memory_space=pltpu.SMEM)
    return pl.pallas_call(
        _moe_kernel,
        grid_spec=pltpu.PrefetchScalarGridSpec(
            num_scalar_prefetch=2,
            grid=(n_blocks,),
            in_specs=[rows_spec(0), rows_spec(1), pl.BlockSpec(memory_space=pl.ANY),
                      wspec(D_MODEL, 2 * D_FF), wspec(1, 2 * D_FF),
                      wspec(D_FF, D_MODEL), wspec(1, D_MODEL)],
            out_specs=pl.BlockSpec((MOE_BLOCK, D_MODEL), lambda i, be, nu: (i, 0)),
            scratch_shapes=[pltpu.VMEM((D_MODEL, 2 * D_FF), bf16), pltpu.VMEM((D_FF, D_MODEL), bf16),
                            pltpu.VMEM((2, MOE_BLOCK, D_MODEL), f32), pltpu.SemaphoreType.DMA((2,))],
        ),
        out_shape=jax.ShapeDtypeStruct((n_blocks * MOE_BLOCK, D_MODEL), f32),
        compiler_params=_cparams(("arbitrary",)),
        name="moe_experts",
    )(block_exp, n_used, rows3, rows3, x, w_gu, b_gu, w_dn, b_dn)


def _combine_kernel(dest_ref, dest_next_ref, x_ref, y_hbm, gate_ref, g_ref, b_ref, o_ref, ybuf, sem):
    i = pl.program_id(0)
    slot = lax.rem(i, 2)

    def row_copy(src_row, slot_, j):
        return pltpu.make_async_copy(y_hbm.at[pl.ds(src_row, 1), :], ybuf.at[slot_, pl.ds(j, 1), :], sem.at[slot_])

    def start_tile(ids_ref, slot_):
        for j in range(TOP_E * ROW_TILE):
            row_copy(ids_ref[0, 0, j], slot_, j).start(priority=j % 2)

    @pl.when(i == 0)
    def _():
        start_tile(dest_ref, 0)

    @pl.when(i + 1 < pl.num_programs(0))
    def _():
        start_tile(dest_next_ref, 1 - slot)

    for j in range(TOP_E * ROW_TILE):
        row_copy(0, slot, j).wait()
    gates = gate_ref[...]
    mix = gates[:, 0:1] * ybuf[slot, 0:ROW_TILE, :]
    for k in range(1, TOP_E):
        mix = mix + gates[:, k:k + 1] * ybuf[slot, k * ROW_TILE:(k + 1) * ROW_TILE, :]
    o_ref[...] = _layer_norm(ALPHA * x_ref[...] + mix, g_ref[...], b_ref[...])


def _combine(x1, y_rows, dest, gates, g, b):
    n = x1.shape[0]
    n_tiles = n // ROW_TILE
    ids = dest.reshape(TOP_E, n_tiles, ROW_TILE).transpose(1, 0, 2).reshape(n_tiles, 1, TOP_E * ROW_TILE)
    row = lambda c: pl.BlockSpec((ROW_TILE, c), lambda i: (i, 0))
    full = lambda r, c: pl.BlockSpec((r, c), lambda i: (0, 0))
    ids_spec = lambda ahead: pl.BlockSpec(
        (1, 1, TOP_E * ROW_TILE), lambda i: (jnp.minimum(i + ahead, n_tiles - 1), 0, 0),
        memory_space=pltpu.SMEM)
    return pl.pallas_call(
        _combine_kernel,
        grid=(n_tiles,),
        in_specs=[ids_spec(0), ids_spec(1), row(D_MODEL), pl.BlockSpec(memory_space=pl.ANY),
                  row(LANES), full(1, D_MODEL), full(1, D_MODEL)],
        out_specs=row(D_MODEL),
        out_shape=jax.ShapeDtypeStruct((n, D_MODEL), f32),
        scratch_shapes=[pltpu.VMEM((2, TOP_E * ROW_TILE, D_MODEL), f32), pltpu.SemaphoreType.DMA((2,))],
        compiler_params=_cparams(("arbitrary",)),
        name="combine_norm",
    )(ids, ids, x1, y_rows, gates, g, b)


def _route(exp_t, rank_t, counts, n_tok):
    padded = (counts + MOE_BLOCK - 1) // MOE_BLOCK * MOE_BLOCK
    pad_end = jnp.cumsum(padded)
    pad_start = pad_end - padded
    hit = exp_t[:, :, None] == jnp.arange(N_EXPERTS, dtype=i32)
    dest = (jnp.sum(jnp.where(hit, pad_start, 0), axis=-1) + rank_t).reshape(-1)
    n_blocks = -(-n_tok * TOP_E // MOE_BLOCK) + N_EXPERTS + 1
    tok = jnp.tile(jnp.arange(n_tok, dtype=i32), TOP_E)
    row_tok = jnp.zeros((n_blocks * MOE_BLOCK,), i32).at[dest].set(tok)
    block_start = jnp.arange(n_blocks, dtype=i32) * MOE_BLOCK
    block_exp = jnp.minimum(jnp.sum(pad_end[None, :] <= block_start[:, None], axis=1), N_EXPERTS - 1).astype(i32)
    n_used = (pad_end[-1:] // MOE_BLOCK).astype(i32)
    return row_tok, block_exp, n_used, dest


def _pad_cols(w):
    cols = w.shape[1]
    padded = -(-cols // LANES) * LANES
    return jnp.pad(w, ((0, 0), (0, padded - cols))).astype(bf16)


def _diag_heads(o_full, nb, dec_seq):
    o6 = o_full.reshape(nb, dec_seq, N_KV, GROUP, N_KV, HEAD_DIM)
    return jnp.einsum('btkgkd->btkgd', o6).reshape(nb * dec_seq, DQ)


def _block_diag_q(q_s, nb, dec_seq):
    q5 = q_s.reshape(nb, dec_seq, N_KV, GROUP, 1, HEAD_DIM)
    eye = jnp.eye(N_KV, dtype=q_s.dtype).reshape(1, 1, N_KV, 1, N_KV, 1)
    return (q5 * eye).reshape(nb, SROWS, DKV)


def _pad_page(a, nb, dec_seq):
    a3 = a.reshape(nb, dec_seq, a.shape[-1])
    return jnp.pad(a3, ((0, 0), (0, PAGE - dec_seq), (0, 0)))


def kernel(x_prompt, x_sample, cache_dsa_k, cache_dsa_v, cache_dsa_ik, cache_fox_k, cache_fox_v,
           cache_fox_logf, page_table, w_in_dsa, w_out_dsa, w_in_fox, b_forget, w_out_fox,
           ln_gain, ln_bias, w_router, b_router, w_gate_up, b_gate_up, w_down, b_down):
    batch, seq, _ = x_prompt.shape
    nb, dec_seq, _ = x_sample.shape
    n_pages = page_table.shape[1]
    pool = cache_dsa_k.shape[1]
    n_p = batch * seq
    n_s = nb * dec_seq
    n_tok = n_p + n_s
    assert dec_seq * N_HEADS == SROWS and n_s == ROW_TILE and n_p % ROW_TILE == 0
    assert n_pages % PAGES_PER_STEP == 0 and seq % KEY_CHUNK == 0

    x = jnp.concatenate([x_prompt.reshape(n_p, D_MODEL), x_sample.reshape(n_s, D_MODEL)], axis=0)
    pt = page_table.reshape(-1).astype(i32)
    slopes = 2.0 ** (-8.0 * jnp.arange(1, N_HEADS + 1, dtype=f32) / N_HEADS)
    slope_rows = jnp.tile(slopes, dec_seq).reshape(SROWS, 1)

    dsa_k = cache_dsa_k.reshape(-1, pool, PAGE, DKV)
    dsa_v = cache_dsa_v.reshape(-1, pool, PAGE, DKV)
    fox_k = cache_fox_k.reshape(-1, pool, PAGE, DKV)
    fox_v = cache_fox_v.reshape(-1, pool, PAGE, DKV)
    b_gu4 = b_gate_up.reshape(DEPTH, N_EXPERTS, 1, 2 * D_FF)
    b_dn4 = b_down.reshape(DEPTH, N_EXPERTS, 1, D_MODEL)

    outs = {name: [] for name in ("dk", "dv", "di", "fk", "fv", "fl")}
    for layer in range(DEPTH):
        j = layer // 2
        if layer % 2 == 0:
            q, k, v, iq, ik, iw = _inproj_dsa(x, _pad_cols(w_in_dsa[j]))
            o_p = _dsa_prompt(q, iq, iw, ik, k, v, batch, seq)
            iq_rows = jnp.pad(iq[n_p:].reshape(nb, dec_seq * IDX_HEADS, IDX_DIM),
                              ((0, 0), (0, (SQ - dec_seq) * IDX_HEADS), (0, 0)))
            w_rows = jnp.pad(iw[n_p:].reshape(nb, dec_seq * IDX_HEADS, 1),
                             ((0, 0), (0, (SQ - dec_seq) * IDX_HEADS), (0, 0)))
            keys = _dsa_sample_scores(pt, iq_rows, w_rows, cache_dsa_ik, _pad_page(ik[n_p:], nb, dec_seq),
                                      j, n_pages, dec_seq)
            o_s = _sample_attn("dsa", pt, _block_diag_q(q[n_p:], nb, dec_seq), dsa_k, dsa_v,
                               _pad_page(k[n_p:], nb, dec_seq), _pad_page(v[n_p:], nb, dec_seq),
                               keys, slope_rows, j, n_pages, dec_seq)
            w_out = w_out_dsa[j]
            outs["dk"].append(k); outs["dv"].append(v); outs["di"].append(ik)
        else:
            q, k, v, lf = _inproj_fox(x, _pad_cols(w_in_fox[j]), b_forget[j])
            cum = _fox_cum(lf, batch, seq)
            o_p = _fox_prompt(q, cum, k, v, batch, seq)
            o_s = _sample_attn("fox", pt, _block_diag_q(q[n_p:], nb, dec_seq), fox_k, fox_v,
                               _pad_page(k[n_p:], nb, dec_seq), _pad_page(v[n_p:], nb, dec_seq),
                               cache_fox_logf, _pad_page(lf[n_p:], nb, dec_seq), j, n_pages, dec_seq)
            w_out = w_out_fox[j]
            outs["fk"].append(k); outs["fv"].append(v); outs["fl"].append(lf)

        wr = w_router[layer]
        wr_hi = wr.astype(bf16)
        wr_lo = (wr - wr_hi.astype(f32)).astype(bf16)
        x1, gates, exp_t, rank_t, cnt = _outproj(
            x, o_p, _diag_heads(o_s, nb, dec_seq), w_out.astype(bf16),
            ln_gain[layer, 0].reshape(1, D_MODEL), ln_bias[layer, 0].reshape(1, D_MODEL),
            wr_hi, wr_lo, b_router[layer].reshape(1, N_EXPERTS))

        row_tok, block_exp, n_used, dest = _route(exp_t[:TOP_E], rank_t[:TOP_E], cnt[:, 0], n_tok)
        y_rows = _moe(block_exp, n_used, row_tok, x1, w_gate_up, b_gu4, w_down, b_dn4, layer)
        x = _combine(x1, y_rows, dest, gates, ln_gain[layer, 1].reshape(1, D_MODEL),
                     ln_bias[layer, 1].reshape(1, D_MODEL))

    def stack(name, shape_p, shape_s):
        a = jnp.stack(outs[name])
        return a[:, :n_p].reshape((-1,) + shape_p), a[:, n_p:].reshape((-1,) + shape_s)

    kv_p, kv_s = (batch, seq, N_KV, HEAD_DIM), (nb, dec_seq, N_KV, HEAD_DIM)
    dk_p, dk_s = stack("dk", kv_p, kv_s)
    dv_p, dv_s = stack("dv", kv_p, kv_s)
    di_p, di_s = stack("di", (batch, seq, IDX_DIM), (nb, dec_seq, IDX_DIM))
    fk_p, fk_s = stack("fk", kv_p, kv_s)
    fv_p, fv_s = stack("fv", kv_p, kv_s)
    fl_p, fl_s = stack("fl", (batch, seq, N_HEADS), (nb, dec_seq, N_HEADS))
    return (x[:n_p].reshape(batch, seq, D_MODEL), x[n_p:].reshape(nb, dec_seq, D_MODEL),
            dk_p, dv_p, di_p, fk_p, fv_p, fl_p, dk_s, dv_s, di_s, fk_s, fv_s, fl_s)
```

```python
import functools

import jax
import jax.numpy as jnp
from jax import lax
from jax.experimental import pallas as pl
from jax.experimental.pallas import tpu as pltpu

f32 = jnp.float32
bf16 = jnp.bfloat16
i32 = jnp.int32

D_MODEL = 1024
DEPTH = 4
PAGE = 128
N_HEADS = 16
HEAD_DIM = 64
N_KV = 4
GROUP = N_HEADS // N_KV
DQ = N_HEADS * HEAD_DIM
DKV = N_KV * HEAD_DIM
IDX_HEADS = 8
IDX_DIM = 64
TOPK = 256
N_EXPERTS = 32
TOP_E = 4
D_FF = D_MODEL
SWIGLU_LIMIT = 7.0
SWIGLU_ALPHA = 1.702
MOE_BLOCK = 128
LN_EPS = 1e-5
ALPHA = (2 * DEPTH) ** 0.25
IDX_SCALE = IDX_HEADS ** -0.5 * IDX_DIM ** -0.5
QK_SCALE = HEAD_DIM ** -0.5

LANES = 128
SUBLANES = 8
Q_TILE = 128
KEY_CHUNK = 256
ROW_TILE = 128
PAGES_PER_STEP = 16
INT_MIN = -2 ** 31
NEG = -1e30
VMEM_LIMIT = 56 * 1024 * 1024

NT = (((1,), (1,)), ((), ()))


def _cparams(sem):
    return pltpu.CompilerParams(dimension_semantics=sem, vmem_limit_bytes=VMEM_LIMIT)


def _split3(x):
    hi = x.astype(bf16)
    r1 = x - hi.astype(f32)
    mid = r1.astype(bf16)
    lo = (r1 - mid.astype(f32)).astype(bf16)
    return hi, mid, lo


def _dot(a, b):
    return jnp.dot(a, b, preferred_element_type=f32)


def _dot_nt(a, b):
    return lax.dot_general(a, b, NT, preferred_element_type=f32)


def _dot3(x, w_bf):
    hi, mid, lo = _split3(x)
    return _dot(hi, w_bf) + _dot(mid, w_bf) + _dot(lo, w_bf)


def _dot3_tn(x, w_bf):
    tn = (((0,), (0,)), ((), ()))
    return sum(lax.dot_general(part, w_bf, tn, preferred_element_type=f32) for part in _split3(x))


def _layer_norm(y, g, b):
    mu = jnp.mean(y, axis=-1, keepdims=True)
    d = y - mu
    var = jnp.mean(d * d, axis=-1, keepdims=True)
    return d * lax.rsqrt(var + LN_EPS) * g + b


def _order_key(s):
    s = jnp.where(s == 0.0, 0.0, s)
    bits = pltpu.bitcast(s, i32)
    return bits ^ ((bits >> 31) & 0x7FFFFFFF)


def _upper_tri(n, strict=False):
    r = lax.broadcasted_iota(i32, (n, n), 0)
    c = lax.broadcasted_iota(i32, (n, n), 1)
    return jnp.where((r < c) if strict else (r <= c), 1.0, 0.0).astype(bf16)


def _inproj_dsa_kernel(x_ref, w_ref, q_ref, k_ref, v_ref, iq_ref, ik_ref, iw_ref):
    xb = x_ref[...].astype(bf16)
    q_ref[...] = _dot(xb, w_ref[:, 0:DQ])
    k_ref[...] = _dot(xb, w_ref[:, DQ:DQ + DKV])
    v_ref[...] = _dot(xb, w_ref[:, DQ + DKV:DQ + 2 * DKV])
    o = DQ + 2 * DKV
    iq_ref[...] = _dot(xb, w_ref[:, o:o + IDX_HEADS * IDX_DIM])
    o += IDX_HEADS * IDX_DIM
    tail = _dot(xb, w_ref[:, o:o + LANES])
    ik_ref[...] = tail[:, 0:IDX_DIM]
    iw_ref[...] = tail[:, IDX_DIM:IDX_DIM + IDX_HEADS]


def _inproj_dsa(x, w_pad):
    n = x.shape[0]
    cols = w_pad.shape[1]
    row = lambda c: pl.BlockSpec((ROW_TILE, c), lambda i: (i, 0))
    widths = (DQ, DKV, DKV, IDX_HEADS * IDX_DIM, IDX_DIM, IDX_HEADS)
    return pl.pallas_call(
        _inproj_dsa_kernel,
        grid=(n // ROW_TILE,),
        in_specs=[row(D_MODEL), pl.BlockSpec((D_MODEL, cols), lambda i: (0, 0))],
        out_specs=[row(c) for c in widths],
        out_shape=[jax.ShapeDtypeStruct((n, c), f32) for c in widths],
        compiler_params=_cparams(("arbitrary",)),
        name="inproj_dsa",
    )(x, w_pad)


def _inproj_fox_kernel(x_ref, w_ref, bf_ref, q_ref, k_ref, v_ref, lf_ref):
    xb = x_ref[...].astype(bf16)
    q_ref[...] = _dot(xb, w_ref[:, 0:DQ])
    k_ref[...] = _dot(xb, w_ref[:, DQ:DQ + DKV])
    v_ref[...] = _dot(xb, w_ref[:, DQ + DKV:DQ + 2 * DKV])
    o = DQ + 2 * DKV
    fg = _dot(xb, w_ref[:, o:o + LANES])[:, 0:N_HEADS] + bf_ref[...]
    lf_ref[...] = jax.nn.log_sigmoid(fg)


def _inproj_fox(x, w_pad, b_f):
    n = x.shape[0]
    cols = w_pad.shape[1]
    row = lambda c: pl.BlockSpec((ROW_TILE, c), lambda i: (i, 0))
    widths = (DQ, DKV, DKV, N_HEADS)
    return pl.pallas_call(
        _inproj_fox_kernel,
        grid=(n // ROW_TILE,),
        in_specs=[row(D_MODEL), pl.BlockSpec((D_MODEL, cols), lambda i: (0, 0)),
                  pl.BlockSpec((1, N_HEADS), lambda i: (0, 0))],
        out_specs=[row(c) for c in widths],
        out_shape=[jax.ShapeDtypeStruct((n, c), f32) for c in widths],
        compiler_params=_cparams(("arbitrary",)),
        name="inproj_fox",
    )(x, w_pad, b_f.reshape(1, N_HEADS))


QROWS = GROUP * Q_TILE


def _flash_block(q_ref, kbf, vbf, o_ref, qs, m_sc, acc_sc, n_chunks, bias_fn):
    for h in range(N_HEADS):
        g = h % GROUP
        qs[h // GROUP, g * Q_TILE:(g + 1) * Q_TILE, :] = (
            q_ref[:, h * HEAD_DIM:(h + 1) * HEAD_DIM] * QK_SCALE).astype(bf16)
    m_sc[...] = jnp.full(m_sc.shape, NEG, f32)
    acc_sc[...] = jnp.zeros(acc_sc.shape, f32)

    def body(c, _):
        off = pl.multiple_of(c * KEY_CHUNK, KEY_CHUNK)
        for kh in range(N_KV):
            kc = kbf[kh, pl.ds(off, KEY_CHUNK), :]
            vc = vbf[kh, pl.ds(off, KEY_CHUNK), :]
            lg = _dot_nt(qs[kh], kc) + bias_fn(c, kh)
            m = m_sc[kh]
            part = lg[:, 0:LANES]
            for j in range(1, KEY_CHUNK // LANES):
                part = jnp.maximum(part, lg[:, j * LANES:(j + 1) * LANES])
            m_new = jnp.maximum(m, jnp.max(part, axis=1, keepdims=True))
            p = jnp.exp(lg - jnp.concatenate([m_new] * (KEY_CHUNK // LANES), axis=1))
            acc_sc[kh] = jnp.exp(m - m_new) * acc_sc[kh] + _dot(p.astype(bf16), vc)
            m_sc[kh] = m_new
        return 0

    lax.fori_loop(0, n_chunks, body, 0)
    for h in range(N_HEADS):
        kh, g = h // GROUP, h % GROUP
        acc = acc_sc[kh, g * Q_TILE:(g + 1) * Q_TILE, :]
        o_ref[:, h * HEAD_DIM:(h + 1) * HEAD_DIM] = acc[:, 0:HEAD_DIM] / acc[:, HEAD_DIM:2 * HEAD_DIM]


_FLASH_SCRATCH = [pltpu.VMEM((N_KV, QROWS, HEAD_DIM), bf16),
                  pltpu.VMEM((N_KV, QROWS, LANES), f32),
                  pltpu.VMEM((N_KV, QROWS, 2 * HEAD_DIM), f32)]


def _stage_kv(k_ref, v_ref, kbf, vbf):
    for h in range(N_KV):
        kbf[h] = k_ref[:, h * HEAD_DIM:(h + 1) * HEAD_DIM].astype(bf16)
        vbf[h, :, 0:HEAD_DIM] = v_ref[:, h * HEAD_DIM:(h + 1) * HEAD_DIM].astype(bf16)
        vbf[h, :, HEAD_DIM:2 * HEAD_DIM] = jnp.ones((v_ref.shape[0], HEAD_DIM), bf16)


def _dsa_prompt_kernel(q_ref, iq_ref, iw_ref, ik_ref, k_ref, v_ref, o_ref,
                       ikbf, kbf, vbf, key3, mb3, dist3, qs, m_sc, acc_sc):
    i = pl.program_id(1)

    @pl.when(i == 0)
    def _():
        ikbf[...] = ik_ref[...].astype(bf16)
        _stage_kv(k_ref, v_ref, kbf, vbf)

    n_chunks = (i * Q_TILE + Q_TILE + KEY_CHUNK - 1) // KEY_CHUNK
    row = lax.broadcasted_iota(i32, (Q_TILE, KEY_CHUNK), 0)
    col = lax.broadcasted_iota(i32, (Q_TILE, KEY_CHUNK), 1)
    qpos = i * Q_TILE + row
    w = iw_ref[...] * IDX_SCALE
    w_rep = [jnp.broadcast_to(w[:, h:h + 1], (Q_TILE, KEY_CHUNK)) for h in range(IDX_HEADS)]
    iqb = iq_ref[...].astype(bf16)

    def score_body(c, _):
        ikc = ikbf[pl.ds(pl.multiple_of(c * KEY_CHUNK, KEY_CHUNK), KEY_CHUNK), :]
        s = jnp.zeros((Q_TILE, KEY_CHUNK), f32)
        for h in range(IDX_HEADS):
            rel = _dot_nt(iqb[:, h * IDX_DIM:(h + 1) * IDX_DIM], ikc)
            s = s + w_rep[h] * jnp.maximum(rel, 0.0)
        d = qpos - (c * KEY_CHUNK + col)
        key3[c] = jnp.where(d >= 0, _order_key(s), INT_MIN)
        dist3[c] = d.astype(f32)
        return 0

    lax.fori_loop(0, n_chunks, score_body, 0)

    def count(pred):
        def body(c, acc):
            return acc + jnp.where(pred(key3[c]), 1.0, 0.0)
        acc = lax.fori_loop(0, n_chunks, body, jnp.zeros((Q_TILE, KEY_CHUNK), f32))
        return jnp.sum(acc, axis=1, keepdims=True)

    def bit_body(it, t):
        cand = t ^ lax.shift_left(jnp.int32(1), 31 - it)
        return jnp.where(count(lambda kk: kk >= cand) >= TOPK, cand, t)

    thr = lax.fori_loop(0, 32, bit_body, jnp.full((Q_TILE, 1), INT_MIN, i32))
    thr = jnp.maximum(thr, INT_MIN + 1)
    need = TOPK - count(lambda kk: kk > thr)
    n_tie = count(lambda kk: kk == thr)
    has_excess = jnp.max(n_tie - need) > 0.0

    @pl.when(jnp.logical_not(has_excess))
    def _():
        def body(c, _):
            mb3[c] = jnp.where(key3[c] >= thr, 0.0, NEG)
            return 0
        lax.fori_loop(0, n_chunks, body, 0)

    @pl.when(has_excess)
    def _():
        tri = _upper_tri(KEY_CHUNK)

        def body(c, before):
            kk = key3[c]
            tie = jnp.where(kk == thr, 1.0, 0.0)
            rank = before + _dot(tie.astype(bf16), tri)
            sel = (kk > thr) | ((kk == thr) & (rank <= need))
            mb3[c] = jnp.where(sel, 0.0, NEG)
            return before + jnp.sum(tie, axis=1, keepdims=True)
        lax.fori_loop(0, n_chunks, body, jnp.zeros((Q_TILE, 1), f32))

    def bias(c, kh):
        mb = mb3[c]
        d = dist3[c]
        slabs = [mb - float(2.0 ** (-8.0 * (kh * GROUP + g + 1) / N_HEADS)) * d for g in range(GROUP)]
        return jnp.concatenate(slabs, axis=0)

    _flash_block(q_ref, kbf, vbf, o_ref, qs, m_sc, acc_sc, n_chunks, bias)


def _dsa_prompt(q, iq, iw, ik, k, v, batch, seq):
    nq = seq // Q_TILE
    nc = seq // KEY_CHUNK
    qspec = lambda c: pl.BlockSpec((Q_TILE, c), lambda b, i: (b * nq + i, 0))
    bspec = lambda c: pl.BlockSpec((seq, c), lambda b, i: (b, 0))
    return pl.pallas_call(
        _dsa_prompt_kernel,
        grid=(batch, nq),
        in_specs=[qspec(DQ), qspec(IDX_HEADS * IDX_DIM), qspec(IDX_HEADS),
                  bspec(IDX_DIM), bspec(DKV), bspec(DKV)],
        out_specs=qspec(DQ),
        out_shape=jax.ShapeDtypeStruct((batch * seq, DQ), f32),
        scratch_shapes=[pltpu.VMEM((seq, IDX_DIM), bf16),
                        pltpu.VMEM((N_KV, seq, HEAD_DIM), bf16),
                        pltpu.VMEM((N_KV, seq, 2 * HEAD_DIM), bf16),
                        pltpu.VMEM((nc, Q_TILE, KEY_CHUNK), i32),
                        pltpu.VMEM((nc, Q_TILE, KEY_CHUNK), f32),
                        pltpu.VMEM((nc, Q_TILE, KEY_CHUNK), f32)] + _FLASH_SCRATCH,
        compiler_params=_cparams(("arbitrary", "arbitrary")),
        name="dsa_prompt",
    )(q, iq, iw, ik, k, v)


def _cum_kernel(lf_ref, o_ref):
    tri = _upper_tri(KEY_CHUNK)
    carry = jnp.zeros((N_HEADS, 1), f32)
    for c in range(lf_ref.shape[0] // KEY_CHUNK):
        cum = _dot3_tn(lf_ref[c * KEY_CHUNK:(c + 1) * KEY_CHUNK, :], tri) + carry
        o_ref[0, c] = cum
        carry = cum[:, KEY_CHUNK - 1:KEY_CHUNK]


def _fox_cum(lf, batch, seq):
    nc = seq // KEY_CHUNK
    return pl.pallas_call(
        _cum_kernel,
        grid=(batch,),
        in_specs=[pl.BlockSpec((seq, N_HEADS), lambda b: (b, 0))],
        out_specs=pl.BlockSpec((1, nc, N_HEADS, KEY_CHUNK), lambda b: (b, 0, 0, 0)),
        out_shape=jax.ShapeDtypeStruct((batch, nc, N_HEADS, KEY_CHUNK), f32),
        compiler_params=_cparams(("arbitrary",)),
        name="fox_cum",
    )(lf)


def _fox_prompt_kernel(q_ref, cum_ref, k_ref, v_ref, o_ref, kbf, vbf, qs, m_sc, acc_sc):
    i = pl.program_id(1)

    @pl.when(i == 0)
    def _():
        _stage_kv(k_ref, v_ref, kbf, vbf)

    n_chunks = (i * Q_TILE + Q_TILE + KEY_CHUNK - 1) // KEY_CHUNK
    row = lax.broadcasted_iota(i32, (Q_TILE, KEY_CHUNK), 0)
    col = lax.broadcasted_iota(i32, (Q_TILE, KEY_CHUNK), 1)
    rel = i * Q_TILE + row - col

    def bias(c, kh):
        ck = cum_ref[0, c]
        causal = rel - c * KEY_CHUNK >= 0
        slabs = [jnp.where(causal, -ck[kh * GROUP + g:kh * GROUP + g + 1, :], NEG) for g in range(GROUP)]
        return jnp.concatenate(slabs, axis=0)

    _flash_block(q_ref, kbf, vbf, o_ref, qs, m_sc, acc_sc, n_chunks, bias)


def _fox_prompt(q, cum, k, v, batch, seq):
    nq = seq // Q_TILE
    nc = seq // KEY_CHUNK
    qspec = lambda c: pl.BlockSpec((Q_TILE, c), lambda b, i: (b * nq + i, 0))
    bspec = lambda c: pl.BlockSpec((seq, c), lambda b, i: (b, 0))
    return pl.pallas_call(
        _fox_prompt_kernel,
        grid=(batch, nq),
        in_specs=[qspec(DQ), pl.BlockSpec((1, nc, N_HEADS, KEY_CHUNK), lambda b, i: (b, 0, 0, 0)),
                  bspec(DKV), bspec(DKV)],
        out_specs=qspec(DQ),
        out_shape=jax.ShapeDtypeStruct((batch * seq, DQ), f32),
        scratch_shapes=[pltpu.VMEM((N_KV, seq, HEAD_DIM), bf16),
                        pltpu.VMEM((N_KV, seq, 2 * HEAD_DIM), bf16)] + _FLASH_SCRATCH,
        compiler_params=_cparams(("arbitrary", "arbitrary")),
        name="fox_prompt",
    )(q, cum, k, v)


SROWS = 64
SQ = SUBLANES


def _page_specs(shape, layer, n_pages, n_steps, group):
    def spec(j):
        def index(b, s, pt):
            return (layer, pt[b * n_pages + jnp.minimum(s, n_steps - 1) * group + j], 0, 0)
        return pl.BlockSpec((1, 1) + shape, index)
    return [spec(j) for j in range(group)]


def _dsa_sample_score_kernel(pt_ref, iq_ref, w_ref, *rest, n_steps, dec_seq, group):
    ikc_refs, ikn_ref, key_ref = rest[:group], rest[group], rest[group + 1]
    s_id = pl.program_id(1)
    iqb = iq_ref[0].astype(bf16)
    wcol = w_ref[0] * IDX_SCALE
    t = lax.broadcasted_iota(i32, (SQ, PAGE), 0)
    j = lax.broadcasted_iota(i32, (SQ, PAGE), 1)

    def score(ik_t):
        rel = _dot(iqb, ik_t.astype(bf16))
        return jnp.sum((jnp.maximum(rel, 0.0) * wcol).reshape(SQ, IDX_HEADS, PAGE), axis=1)

    @pl.when(s_id < n_steps)
    def _():
        for g in range(group):
            key_ref[0, 0, g] = jnp.where(t < dec_seq, _order_key(score(ikc_refs[g][0, 0])), INT_MIN)

    @pl.when(s_id == n_steps)
    def _():
        ok = (t < dec_seq) & (j <= t)
        key_ref[0, 0, 0] = jnp.where(ok, _order_key(score(ikn_ref[0])), INT_MIN)
        for g in range(1, group):
            key_ref[0, 0, g] = jnp.full((SQ, PAGE), INT_MIN, i32)


def _dsa_sample_scores(pt, iq_rows, w_rows, cache_ik, ik_new, layer, n_pages, dec_seq):
    nb = iq_rows.shape[0]
    group = PAGES_PER_STEP
    n_steps = n_pages // group
    kern = functools.partial(_dsa_sample_score_kernel, n_steps=n_steps, dec_seq=dec_seq, group=group)
    per_b = lambda b, s, pt: (b, 0, 0)
    return pl.pallas_call(
        kern,
        grid_spec=pltpu.PrefetchScalarGridSpec(
            num_scalar_prefetch=1,
            grid=(nb, n_steps + 1),
            in_specs=[pl.BlockSpec((1, SQ * IDX_HEADS, IDX_DIM), per_b),
                      pl.BlockSpec((1, SQ * IDX_HEADS, 1), per_b)]
                     + _page_specs((IDX_DIM, PAGE), layer, n_pages, n_steps, group)
                     + [pl.BlockSpec((1, IDX_DIM, PAGE), per_b)],
            out_specs=pl.BlockSpec((1, 1, group, SQ, PAGE), lambda b, s, pt: (b, s, 0, 0, 0)),
        ),
        out_shape=jax.ShapeDtypeStruct((nb, n_steps + 1, group, SQ, PAGE), i32),
        compiler_params=_cparams(("arbitrary", "arbitrary")),
        name="dsa_sample_scores",
    )(pt, iq_rows, w_rows, *([cache_ik] * group), ik_new)


def _sample_attn_kernel(pt_ref, q_ref, *rest, mode, n_pages, n_steps, dec_seq, group):
    kc_refs, vc_refs = rest[:group], rest[group:2 * group]
    rest = rest[2 * group:]
    if mode == "dsa":
        keys_ref, kn_ref, vn_ref, slope_ref, o_ref = rest[:5]
        m_sc, l_sc, acc_sc, carry_sc, thr_sc, need_sc = rest[5:]
    else:
        lf_refs, (kn_ref, vn_ref, lfn_ref, o_ref) = rest[:group], rest[group:group + 4]
        m_sc, l_sc, acc_sc, carry_sc = rest[group + 4:]
    s_id = pl.program_id(1)
    row = lax.broadcasted_iota(i32, (SROWS, PAGE), 0)
    col = lax.broadcasted_iota(i32, (SROWS, PAGE), 1)
    tok = row // N_HEADS
    qb = (q_ref[0] * QK_SCALE).astype(bf16)

    @pl.when(s_id == 0)
    def _():
        m_sc[...] = jnp.full(m_sc.shape, NEG, f32)
        l_sc[...] = jnp.zeros(l_sc.shape, f32)
        acc_sc[...] = jnp.zeros(acc_sc.shape, f32)
        carry_sc[...] = jnp.zeros(carry_sc.shape, f32)

    def update(ks, vs, biases):
        lg = jnp.concatenate([_dot(qb, k.astype(bf16)) + b for k, b in zip(ks, biases)], axis=1)
        m = m_sc[...]
        m_new = jnp.maximum(m, jnp.max(lg, axis=1, keepdims=True))
        a = jnp.exp(m - m_new)
        p = jnp.exp(lg - m_new)
        l_sc[...] = a * l_sc[...] + jnp.sum(p, axis=1, keepdims=True)
        pr = p.astype(bf16)
        pv = _dot_nt(pr[:, 0:PAGE], vs[0].astype(bf16))
        for g in range(1, len(vs)):
            pv = pv + _dot_nt(pr[:, g * PAGE:(g + 1) * PAGE], vs[g].astype(bf16))
        acc_sc[...] = a * acc_sc[...] + pv
        m_sc[...] = m_new

    if mode == "dsa":
        tri = _upper_tri(PAGE)

        @pl.when(s_id == 0)
        def _():
            kk = keys_ref[0].reshape((n_steps + 1) * group, SQ, PAGE)

            def count(pred):
                return jnp.sum(jnp.sum(jnp.where(pred(kk), 1.0, 0.0), axis=0), axis=1, keepdims=True)

            def bit_body(it, t):
                cand = t ^ lax.shift_left(jnp.int32(1), 31 - it)
                return jnp.where(count(lambda a: a >= cand[None]) >= TOPK, cand, t)

            thr = lax.fori_loop(0, 32, bit_body, jnp.full((SQ, 1), INT_MIN, i32))
            thr = jnp.maximum(thr, INT_MIN + 1)
            thr_sc[...] = thr
            need_sc[...] = TOPK - count(lambda a: a > thr[None])

        def dsa_biases(kps, first_page):
            thr, need = thr_sc[...], need_sc[...]
            before = carry_sc[0:SQ, :]
            out = []
            for g, kp in enumerate(kps):
                tie = jnp.where(kp == thr, 1.0, 0.0)
                rank = before + _dot(tie.astype(bf16), tri)
                sel = jnp.where((kp > thr) | ((kp == thr) & (rank <= need)), 1.0, 0.0)
                before = before + jnp.sum(tie, axis=1, keepdims=True)
                sel_rows = jnp.concatenate(
                    [jnp.broadcast_to(sel[t:t + 1, :], (N_HEADS, PAGE)) for t in range(dec_seq)], axis=0)
                dist = (n_pages * PAGE + tok - ((first_page + g) * PAGE + col)).astype(f32)
                out.append(jnp.where(sel_rows > 0.5, -slope_ref[...] * dist, NEG))
            carry_sc[0:SQ, :] = before
            return out

        @pl.when(s_id < n_steps)
        def _():
            kps = [keys_ref[0, s_id, g] for g in range(group)]
            update([r[0, 0] for r in kc_refs], [r[0, 0] for r in vc_refs], dsa_biases(kps, s_id * group))

        @pl.when(s_id == n_steps)
        def _():
            update([kn_ref[0]], [vn_ref[0]], dsa_biases([keys_ref[0, n_steps, 0]], n_pages))
            o_ref[0] = acc_sc[...] / l_sc[...]
    else:
        tri = _upper_tri(PAGE)

        def fox_biases(lfs, mask_new):
            before = carry_sc[...]
            out = []
            for lf_t in lfs:
                cum = before + _dot3(lf_t, tri)
                before = cum[:, PAGE - 1:PAGE]
                ck = jnp.concatenate([cum] * dec_seq, axis=0)
                out.append(jnp.where(col <= tok, -ck, NEG) if mask_new else -ck)
            carry_sc[...] = before
            return out

        @pl.when(s_id < n_steps)
        def _():
            update([r[0, 0] for r in kc_refs], [r[0, 0] for r in vc_refs],
                   fox_biases([r[0, 0] for r in lf_refs], False))

        @pl.when(s_id == n_steps)
        def _():
            update([kn_ref[0]], [vn_ref[0]], fox_biases([lfn_ref[0]], True))
            o_ref[0] = acc_sc[...] / l_sc[...]


def _sample_attn(mode, pt, q_rows, cache_k, cache_v, k_new, v_new, aux, aux_new, layer, n_pages, dec_seq):
    nb = q_rows.shape[0]
    group = PAGES_PER_STEP
    n_steps = n_pages // group
    kern = functools.partial(_sample_attn_kernel, mode=mode, n_pages=n_pages, n_steps=n_steps,
                             dec_seq=dec_seq, group=group)
    per_b = lambda b, s, pt: (b, 0, 0)
    kv_pages = _page_specs((DKV, PAGE), layer, n_pages, n_steps, group)
    new_page = pl.BlockSpec((1, DKV, PAGE), per_b)
    scratch = [pltpu.VMEM((SROWS, 1), f32), pltpu.VMEM((SROWS, 1), f32), pltpu.VMEM((SROWS, DKV), f32),
               pltpu.VMEM((N_HEADS, 1), f32)]
    if mode == "dsa":
        mid_specs = [pl.BlockSpec((1, n_steps + 1, group, SQ, PAGE), lambda b, s, pt: (b, 0, 0, 0, 0)),
                     new_page, new_page, pl.BlockSpec((SROWS, 1), lambda b, s, pt: (0, 0))]
        mid_args = [aux, k_new, v_new, aux_new]
        scratch += [pltpu.VMEM((SQ, 1), i32), pltpu.VMEM((SQ, 1), f32)]
    else:
        mid_specs = (_page_specs((N_HEADS, PAGE), layer, n_pages, n_steps, group)
                     + [new_page, new_page, pl.BlockSpec((1, N_HEADS, PAGE), per_b)])
        mid_args = [aux] * group + [k_new, v_new, aux_new]
    return pl.pallas_call(
        kern,
        grid_spec=pltpu.PrefetchScalarGridSpec(
            num_scalar_prefetch=1,
            grid=(nb, n_steps + 1),
            in_specs=[pl.BlockSpec((1, SROWS, DKV), per_b)] + kv_pages + kv_pages + mid_specs,
            out_specs=pl.BlockSpec((1, SROWS, DKV), per_b),
            scratch_shapes=scratch,
        ),
        out_shape=jax.ShapeDtypeStruct((nb, SROWS, DKV), f32),
        compiler_params=_cparams(("arbitrary", "arbitrary")),
        name=mode + "_sample_attn",
    )(pt, q_rows, *([cache_k] * group), *([cache_v] * group), *mid_args)


def _outproj_kernel(x_ref, op_ref, os_ref, w_ref, g_ref, b_ref, wr_hi_ref, wr_lo_ref, br_ref,
                    x1_ref, gate_ref, exp_t_ref, rank_t_ref, cnt_ref, run_sc, *, n_prompt_tiles):
    i = pl.program_id(0)
    o = jnp.where(i < n_prompt_tiles, op_ref[...], os_ref[...]).astype(bf16)
    x1 = _layer_norm(ALPHA * x_ref[...] + _dot(o, w_ref[...]), g_ref[...], b_ref[...])
    x1_ref[...] = x1

    hi = x1.astype(bf16)
    lo = (x1 - hi.astype(f32)).astype(bf16)
    lg = _dot(hi, wr_hi_ref[...]) + (_dot(lo, wr_hi_ref[...]) + _dot(hi, wr_lo_ref[...])) + br_ref[...]
    lane = lax.broadcasted_iota(i32, lg.shape, 1)
    out_lane = lax.broadcasted_iota(i32, (ROW_TILE, LANES), 1)
    vals = []
    experts = jnp.zeros((ROW_TILE, LANES), i32)
    for k in range(TOP_E):
        m = jnp.max(lg, axis=1, keepdims=True)
        idx = jnp.min(jnp.where(lg == m, lane, N_EXPERTS), axis=1, keepdims=True)
        vals.append(m)
        experts = jnp.where(out_lane == k, idx, experts)
        lg = jnp.where(lane == idx, -jnp.inf, lg)
    es = [jnp.exp(v - vals[0]) for v in vals]
    tot = es[0] + es[1] + es[2] + es[3]
    gates = jnp.zeros((ROW_TILE, LANES), f32)
    for k in range(TOP_E):
        gates = jnp.where(out_lane == k, es[k] / tot, gates)
    gate_ref[...] = gates

    @pl.when(i == 0)
    def _():
        run_sc[...] = jnp.zeros(run_sc.shape, f32)

    exp_t = experts.T[0:SUBLANES, :]
    e_iota = lax.broadcasted_iota(i32, (N_EXPERTS, ROW_TILE), 0)
    hits = [e_iota == exp_t[k:k + 1, :] for k in range(TOP_E)]
    member = jnp.where(hits[0] | hits[1] | hits[2] | hits[3], 1.0, 0.0)
    pos = run_sc[...] + _dot(member.astype(bf16), _upper_tri(ROW_TILE, strict=True))
    sub = lax.broadcasted_iota(i32, (SUBLANES, ROW_TILE), 0)
    rank_t = jnp.zeros((SUBLANES, ROW_TILE), f32)
    for k in range(TOP_E):
        rank_t = jnp.where(sub == k, jnp.sum(jnp.where(hits[k], pos, 0.0), axis=0, keepdims=True), rank_t)
    run_sc[...] = run_sc[...] + jnp.sum(member, axis=1, keepdims=True)
    exp_t_ref[...] = exp_t
    rank_t_ref[...] = rank_t.astype(i32)
    cnt_ref[...] = jnp.broadcast_to(run_sc[...], cnt_ref.shape).astype(i32)


def _outproj(x, o_prompt, o_sample, w_bf, g, b, wr_hi, wr_lo, br):
    n = x.shape[0]
    npt = o_prompt.shape[0] // ROW_TILE
    nst = o_sample.shape[0] // ROW_TILE
    row = lambda c: pl.BlockSpec((ROW_TILE, c), lambda i: (i, 0))
    col = pl.BlockSpec((SUBLANES, ROW_TILE), lambda i: (0, i))
    full = lambda r, c: pl.BlockSpec((r, c), lambda i: (0, 0))
    kern = functools.partial(_outproj_kernel, n_prompt_tiles=npt)
    return pl.pallas_call(
        kern,
        grid=(n // ROW_TILE,),
        in_specs=[row(D_MODEL),
                  pl.BlockSpec((ROW_TILE, DQ), lambda i: (jnp.minimum(i, npt - 1), 0)),
                  pl.BlockSpec((ROW_TILE, DQ), lambda i: (jnp.clip(i - npt, 0, nst - 1), 0)),
                  full(DQ, D_MODEL), full(1, D_MODEL), full(1, D_MODEL),
                  full(D_MODEL, N_EXPERTS), full(D_MODEL, N_EXPERTS), full(1, N_EXPERTS)],
        out_specs=[row(D_MODEL), row(LANES), col, col, full(N_EXPERTS, LANES)],
        out_shape=[jax.ShapeDtypeStruct((n, D_MODEL), f32),
                   jax.ShapeDtypeStruct((n, LANES), f32),
                   jax.ShapeDtypeStruct((SUBLANES, n), i32), jax.ShapeDtypeStruct((SUBLANES, n), i32),
                   jax.ShapeDtypeStruct((N_EXPERTS, LANES), i32)],
        scratch_shapes=[pltpu.VMEM((N_EXPERTS, 1), f32)],
        compiler_params=_cparams(("arbitrary",)),
        name="outproj_router",
    )(x, o_prompt, o_sample, w_bf, g, b, wr_hi, wr_lo, br)


def _moe_kernel(bexp_ref, used_ref, rows_ref, rows_next_ref, x_hbm, wgu_ref, bgu_ref, wdn_ref, bdn_ref,
                y_ref, wgu_bf, wdn_bf, xbuf, sem):
    i = pl.program_id(0)
    n_used = used_ref[0]
    slot = lax.rem(i, 2)
    e = bexp_ref[i]
    prev = bexp_ref[jnp.maximum(i - 1, 0)]

    def row_copy(tok, slot_, r):
        return pltpu.make_async_copy(x_hbm.at[pl.ds(tok, 1), :], xbuf.at[slot_, pl.ds(r, 1), :], sem.at[slot_])

    def start_block(tok_ref, slot_):
        for r in range(MOE_BLOCK):
            row_copy(tok_ref[0, 0, r], slot_, r).start()

    def wait_block():
        for r in range(MOE_BLOCK):
            row_copy(0, slot, r).wait()

    @pl.when(i == 0)
    def _():
        start_block(rows_ref, 0)

    @pl.when((i == 0) | (e != prev))
    def _():
        wgu_bf[...] = wgu_ref[0, 0].astype(bf16)
        wdn_bf[...] = wdn_ref[0, 0].astype(bf16)

    @pl.when(i < n_used)
    def _():
        wait_block()
        start_block(rows_next_ref, 1 - slot)
        h = _dot(xbuf[slot].astype(bf16), wgu_bf[...]) + bgu_ref[0, 0]
        gate = jnp.minimum(h[:, 0:D_FF], SWIGLU_LIMIT)
        up = jnp.clip(h[:, D_FF:2 * D_FF], -SWIGLU_LIMIT, SWIGLU_LIMIT)
        glu = gate * jax.nn.sigmoid(SWIGLU_ALPHA * gate)
        act = ((up + 1.0) * glu).astype(bf16)
        y_ref[...] = _dot(act, wdn_bf[...]) + bdn_ref[0, 0]

    @pl.when(i == n_used)
    def _():
        wait_block()

    @pl.when(i >= n_used)
    def _():
        y_ref[...] = jnp.zeros(y_ref.shape, f32)


def _moe(block_exp, n_used, row_tok, x, w_gu, b_gu, w_dn, b_dn, layer):
    n_blocks = block_exp.shape[0]
    rows3 = row_tok.reshape(n_blocks, 1, MOE_BLOCK)
    wspec = lambda r, c: pl.BlockSpec((1, 1, r, c), lambda i, be, nu: (layer, be[i], 0, 0))
    rows_spec = lambda ahead: pl.BlockSpec(
        (1, 1, MOE_BLOCK), lambda i, be, nu: (jnp.minimum(i + ahead, n_blocks - 1), 0, 0),
        memory_space=pltpu.SMEM)
    return pl.pallas_call(
        _moe_kernel,
        grid_spec=pltpu.PrefetchScalarGridSpec(
            num_scalar_prefetch=2,
            grid=(n_blocks,),
            in_specs=[rows_spec(0), rows_spec(1), pl.BlockSpec(memory_space=pl.ANY),
                      wspec(D_MODEL, 2 * D_FF), wspec(1, 2 * D_FF),
                      wspec(D_FF, D_MODEL), wspec(1, D_MODEL)],
            out_specs=pl.BlockSpec((MOE_BLOCK, D_MODEL), lambda i, be, nu: (i, 0)),
            scratch_shapes=[pltpu.VMEM((D_MODEL, 2 * D_FF), bf16), pltpu.VMEM((D_FF, D_MODEL), bf16),
                            pltpu.VMEM((2, MOE_BLOCK, D_MODEL), f32), pltpu.SemaphoreType.DMA((2,))],
        ),
        out_shape=jax.ShapeDtypeStruct((n_blocks * MOE_BLOCK, D_MODEL), f32),
        compiler_params=_cparams(("arbitrary",)),
        name="moe_experts",
    )(block_exp, n_used, rows3, rows3, x, w_gu, b_gu, w_dn, b_dn)


def _combine_kernel(dest_ref, dest_next_ref, x_ref, y_hbm, gate_ref, g_ref, b_ref, o_ref, ybuf, sem):
    i = pl.program_id(0)
    slot = lax.rem(i, 2)

    def row_copy(src_row, slot_, j):
        return pltpu.make_async_copy(y_hbm.at[pl.ds(src_row, 1), :], ybuf.at[slot_, pl.ds(j, 1), :], sem.at[slot_])

    def start_tile(ids_ref, slot_):
        for j in range(TOP_E * ROW_TILE):
            row_copy(ids_ref[0, 0, j], slot_, j).start(priority=j % 2)

    @pl.when(i == 0)
    def _():
        start_tile(dest_ref, 0)

    @pl.when(i + 1 < pl.num_programs(0))
    def _():
        start_tile(dest_next_ref, 1 - slot)

    for j in range(TOP_E * ROW_TILE):
        row_copy(0, slot, j).wait()
    gates = gate_ref[...]
    mix = gates[:, 0:1] * ybuf[slot, 0:ROW_TILE, :]
    for k in range(1, TOP_E):
        mix = mix + gates[:, k:k + 1] * ybuf[slot, k * ROW_TILE:(k + 1) * ROW_TILE, :]
    o_ref[...] = _layer_norm(ALPHA * x_ref[...] + mix, g_ref[...], b_ref[...])


def _combine(x1, y_rows, dest, gates, g, b):
    n = x1.shape[0]
    n_tiles = n // ROW_TILE
    ids = dest.reshape(TOP_E, n_tiles, ROW_TILE).transpose(1, 0, 2).reshape(n_tiles, 1, TOP_E * ROW_TILE)
    row = lambda c: pl.BlockSpec((ROW_TILE, c), lambda i: (i, 0))
    full = lambda r, c: pl.BlockSpec((r, c), lambda i: (0, 0))
    ids_spec = lambda ahead: pl.BlockSpec(
        (1, 1, TOP_E * ROW_TILE), lambda i: (jnp.minimum(i + ahead, n_tiles - 1), 0, 0),
        memory_space=pltpu.SMEM)
    return pl.pallas_call(
        _combine_kernel,
        grid=(n_tiles,),
        in_specs=[ids_spec(0), ids_spec(1), row(D_MODEL), pl.BlockSpec(memory_space=pl.ANY),
                  row(LANES), full(1, D_MODEL), full(1, D_MODEL)],
        out_specs=row(D_MODEL),
        out_shape=jax.ShapeDtypeStruct((n, D_MODEL), f32),
        scratch_shapes=[pltpu.VMEM((2, TOP_E * ROW_TILE, D_MODEL), f32), pltpu.SemaphoreType.DMA((2,))],
        compiler_params=_cparams(("arbitrary",)),
        name="combine_norm",
    )(ids, ids, x1, y_rows, gates, g, b)


def _route(exp_t, rank_t, counts, n_tok):
    padded = (counts + MOE_BLOCK - 1) // MOE_BLOCK * MOE_BLOCK
    pad_end = jnp.cumsum(padded)
    pad_start = pad_end - padded
    hit = exp_t[:, :, None] == jnp.arange(N_EXPERTS, dtype=i32)
    dest = (jnp.sum(jnp.where(hit, pad_start, 0), axis=-1) + rank_t).reshape(-1)
    n_blocks = -(-n_tok * TOP_E // MOE_BLOCK) + N_EXPERTS + 1
    tok = jnp.tile(jnp.arange(n_tok, dtype=i32), TOP_E)
    row_tok = jnp.zeros((n_blocks * MOE_BLOCK,), i32).at[dest].set(tok)
    block_start = jnp.arange(n_blocks, dtype=i32) * MOE_BLOCK
    block_exp = jnp.minimum(jnp.sum(pad_end[None, :] <= block_start[:, None], axis=1), N_EXPERTS - 1).astype(i32)
    n_used = (pad_end[-1:] // MOE_BLOCK).astype(i32)
    return row_tok, block_exp, n_used, dest


def _pad_cols(w):
    cols = w.shape[1]
    padded = -(-cols // LANES) * LANES
    return jnp.pad(w, ((0, 0), (0, padded - cols))).astype(bf16)


def _diag_heads(o_full, nb, dec_seq):
    o6 = o_full.reshape(nb, dec_seq, N_KV, GROUP, N_KV, HEAD_DIM)
    return jnp.einsum('btkgkd->btkgd', o6).reshape(nb * dec_seq, DQ)


def _block_diag_q(q_s, nb, dec_seq):
    q5 = q_s.reshape(nb, dec_seq, N_KV, GROUP, 1, HEAD_DIM)
    eye = jnp.eye(N_KV, dtype=q_s.dtype).reshape(1, 1, N_KV, 1, N_KV, 1)
    return (q5 * eye).reshape(nb, SROWS, DKV)


def _pad_page(a, nb, dec_seq):
    a3 = a.reshape(nb, dec_seq, a.shape[-1])
    return jnp.pad(a3, ((0, 0), (0, PAGE - dec_seq), (0, 0)))


def kernel(x_prompt, x_sample, cache_dsa_k, cache_dsa_v, cache_dsa_ik, cache_fox_k, cache_fox_v,
           cache_fox_logf, page_table, w_in_dsa, w_out_dsa, w_in_fox, b_forget, w_out_fox,
           ln_gain, ln_bias, w_router, b_router, w_gate_up, b_gate_up, w_down, b_down):
    batch, seq, _ = x_prompt.shape
    nb, dec_seq, _ = x_sample.shape
    n_pages = page_table.shape[1]
    pool = cache_dsa_k.shape[1]
    n_p = batch * seq
    n_s = nb * dec_seq
    n_tok = n_p + n_s
    assert dec_seq * N_HEADS == SROWS and n_s == ROW_TILE and n_p % ROW_TILE == 0
    assert n_pages % PAGES_PER_STEP == 0 and seq % KEY_CHUNK == 0

    x = jnp.concatenate([x_prompt.reshape(n_p, D_MODEL), x_sample.reshape(n_s, D_MODEL)], axis=0)
    pt = page_table.reshape(-1).astype(i32)
    slopes = 2.0 ** (-8.0 * jnp.arange(1, N_HEADS + 1, dtype=f32) / N_HEADS)
    slope_rows = jnp.tile(slopes, dec_seq).reshape(SROWS, 1)

    kv_t = lambda c: jnp.transpose(c, (0, 1, 3, 4, 2)).reshape(-1, pool, DKV, PAGE)
    dsa_k, dsa_v, fox_k, fox_v = kv_t(cache_dsa_k), kv_t(cache_dsa_v), kv_t(cache_fox_k), kv_t(cache_fox_v)
    dsa_ik_t = jnp.swapaxes(cache_dsa_ik, 2, 3)
    fox_lf_t = jnp.swapaxes(cache_fox_logf, 2, 3)
    new_t = lambda a: jnp.swapaxes(_pad_page(a, nb, dec_seq), 1, 2)
    b_gu4 = b_gate_up.reshape(DEPTH, N_EXPERTS, 1, 2 * D_FF)
    b_dn4 = b_down.reshape(DEPTH, N_EXPERTS, 1, D_MODEL)

    outs = {name: [] for name in ("dk", "dv", "di", "fk", "fv", "fl")}
    for layer in range(DEPTH):
        j = layer // 2
        if layer % 2 == 0:
            q, k, v, iq, ik, iw = _inproj_dsa(x, _pad_cols(w_in_dsa[j]))
            o_p = _dsa_prompt(q, iq, iw, ik, k, v, batch, seq)
            iq_rows = jnp.pad(iq[n_p:].reshape(nb, dec_seq * IDX_HEADS, IDX_DIM),
                              ((0, 0), (0, (SQ - dec_seq) * IDX_HEADS), (0, 0)))
            w_rows = jnp.pad(iw[n_p:].reshape(nb, dec_seq * IDX_HEADS, 1),
                             ((0, 0), (0, (SQ - dec_seq) * IDX_HEADS), (0, 0)))
            keys = _dsa_sample_scores(pt, iq_rows, w_rows, dsa_ik_t, new_t(ik[n_p:]),
                                      j, n_pages, dec_seq)
            o_s = _sample_attn("dsa", pt, _block_diag_q(q[n_p:], nb, dec_seq), dsa_k, dsa_v,
                               new_t(k[n_p:]), new_t(v[n_p:]),
                               keys, slope_rows, j, n_pages, dec_seq)
            w_out = w_out_dsa[j]
            outs["dk"].append(k); outs["dv"].append(v); outs["di"].append(ik)
        else:
            q, k, v, lf = _inproj_fox(x, _pad_cols(w_in_fox[j]), b_forget[j])
            cum = _fox_cum(lf, batch, seq)
            o_p = _fox_prompt(q, cum, k, v, batch, seq)
            o_s = _sample_attn("fox", pt, _block_diag_q(q[n_p:], nb, dec_seq), fox_k, fox_v,
                               new_t(k[n_p:]), new_t(v[n_p:]),
                               fox_lf_t, new_t(lf[n_p:]), j, n_pages, dec_seq)
            w_out = w_out_fox[j]
            outs["fk"].append(k); outs["fv"].append(v); outs["fl"].append(lf)

        wr = w_router[layer]
        wr_hi = wr.astype(bf16)
        wr_lo = (wr - wr_hi.astype(f32)).astype(bf16)
        x1, gates, exp_t, rank_t, cnt = _outproj(
            x, o_p, _diag_heads(o_s, nb, dec_seq), w_out.astype(bf16),
            ln_gain[layer, 0].reshape(1, D_MODEL), ln_bias[layer, 0].reshape(1, D_MODEL),
            wr_hi, wr_lo, b_router[layer].reshape(1, N_EXPERTS))

        row_tok, block_exp, n_used, dest = _route(exp_t[:TOP_E], rank_t[:TOP_E], cnt[:, 0], n_tok)
        y_rows = _moe(block_exp, n_used, row_tok, x1, w_gate_up, b_gu4, w_down, b_dn4, layer)
        x = _combine(x1, y_rows, dest, gates, ln_gain[layer, 1].reshape(1, D_MODEL),
                     ln_bias[layer, 1].reshape(1, D_MODEL))

    def stack(name, shape_p, shape_s):
        a = jnp.stack(outs[name])
        return a[:, :n_p].reshape((-1,) + shape_p), a[:, n_p:].reshape((-1,) + shape_s)

    kv_p, kv_s = (batch, seq, N_KV, HEAD_DIM), (nb, dec_seq, N_KV, HEAD_DIM)
    dk_p, dk_s = stack("dk", kv_p, kv_s)
    dv_p, dv_s = stack("dv", kv_p, kv_s)
    di_p, di_s = stack("di", (batch, seq, IDX_DIM), (nb, dec_seq, IDX_DIM))
    fk_p, fk_s = stack("fk", kv_p, kv_s)
    fv_p, fv_s = stack("fv", kv_p, kv_s)
    fl_p, fl_s = stack("fl", (batch, seq, N_HEADS), (nb, dec_seq, N_HEADS))
    return (x[:n_p].reshape(batch, seq, D_MODEL), x[n_p:].reshape(nb, dec_seq, D_MODEL),
            dk_p, dv_p, di_p, fk_p, fv_p, fl_p, dk_s, dv_s, di_s, fk_s, fv_s, fl_s)
```

```python
import functools

import jax
import jax.numpy as jnp
from jax import lax
from jax.experimental import pallas as pl
from jax.experimental.pallas import tpu as pltpu

f32 = jnp.float32
bf16 = jnp.bfloat16
i32 = jnp.int32

D_MODEL = 1024
DEPTH = 4
PAGE = 128
N_HEADS = 16
HEAD_DIM = 64
N_KV = 4
GROUP = N_HEADS // N_KV
DQ = N_HEADS * HEAD_DIM
DKV = N_KV * HEAD_DIM
IDX_HEADS = 8
IDX_DIM = 64
TOPK = 256
N_EXPERTS = 32
TOP_E = 4
D_FF = D_MODEL
SWIGLU_LIMIT = 7.0
SWIGLU_ALPHA = 1.702
MOE_BLOCK = 256
LN_EPS = 1e-5
ALPHA = (2 * DEPTH) ** 0.25
IDX_SCALE = IDX_HEADS ** -0.5 * IDX_DIM ** -0.5
QK_SCALE = HEAD_DIM ** -0.5
LOG2E = 1.4426950408889634

LANES = 128
SUBLANES = 8
Q_TILE = 128
KEY_CHUNK = 256
ROW_TILE = 128
PAGES_PER_STEP = 16
INT_MIN = -2 ** 31
NEG = -1e30
VMEM_LIMIT = 56 * 1024 * 1024

NT = (((1,), (1,)), ((), ()))


def _cparams(sem):
    return pltpu.CompilerParams(dimension_semantics=sem, vmem_limit_bytes=VMEM_LIMIT)


def _split3(x):
    hi = x.astype(bf16)
    r1 = x - hi.astype(f32)
    mid = r1.astype(bf16)
    lo = (r1 - mid.astype(f32)).astype(bf16)
    return hi, mid, lo


def _dot(a, b):
    return jnp.dot(a, b, preferred_element_type=f32)


def _dot_nt(a, b):
    return lax.dot_general(a, b, NT, preferred_element_type=f32)


def _dot3(x, w_bf):
    hi, mid, lo = _split3(x)
    return _dot(hi, w_bf) + _dot(mid, w_bf) + _dot(lo, w_bf)


def _dot3_tn(x, w_bf):
    tn = (((0,), (0,)), ((), ()))
    return sum(lax.dot_general(part, w_bf, tn, preferred_element_type=f32) for part in _split3(x))


def _layer_norm(y, g, b):
    mu = jnp.mean(y, axis=-1, keepdims=True)
    d = y - mu
    var = jnp.mean(d * d, axis=-1, keepdims=True)
    return d * lax.rsqrt(var + LN_EPS) * g + b


def _order_key(s):
    s = jnp.where(s == 0.0, 0.0, s)
    bits = pltpu.bitcast(s, i32)
    return bits ^ ((bits >> 31) & 0x7FFFFFFF)


def _upper_tri(n, strict=False):
    r = lax.broadcasted_iota(i32, (n, n), 0)
    c = lax.broadcasted_iota(i32, (n, n), 1)
    return jnp.where((r < c) if strict else (r <= c), 1.0, 0.0).astype(bf16)


def _inproj_dsa_kernel(x_ref, w_ref, q_ref, k_ref, v_ref, iq_ref, ik_ref, iw_ref):
    xb = x_ref[...].astype(bf16)
    q_ref[...] = _dot(xb, w_ref[:, 0:DQ])
    k_ref[...] = _dot(xb, w_ref[:, DQ:DQ + DKV])
    v_ref[...] = _dot(xb, w_ref[:, DQ + DKV:DQ + 2 * DKV])
    o = DQ + 2 * DKV
    iq_ref[...] = _dot(xb, w_ref[:, o:o + IDX_HEADS * IDX_DIM])
    o += IDX_HEADS * IDX_DIM
    tail = _dot(xb, w_ref[:, o:o + LANES])
    ik_ref[...] = tail[:, 0:IDX_DIM]
    iw_ref[...] = tail[:, IDX_DIM:IDX_DIM + IDX_HEADS]


def _inproj_dsa(x, w_pad):
    n = x.shape[0]
    cols = w_pad.shape[1]
    row = lambda c: pl.BlockSpec((ROW_TILE, c), lambda i: (i, 0))
    widths = (DQ, DKV, DKV, IDX_HEADS * IDX_DIM, IDX_DIM, IDX_HEADS)
    return pl.pallas_call(
        _inproj_dsa_kernel,
        grid=(n // ROW_TILE,),
        in_specs=[row(D_MODEL), pl.BlockSpec((D_MODEL, cols), lambda i: (0, 0))],
        out_specs=[row(c) for c in widths],
        out_shape=[jax.ShapeDtypeStruct((n, c), f32) for c in widths],
        compiler_params=_cparams(("arbitrary",)),
        name="inproj_dsa",
    )(x, w_pad)


def _inproj_fox_kernel(x_ref, w_ref, bf_ref, q_ref, k_ref, v_ref, lf_ref):
    xb = x_ref[...].astype(bf16)
    q_ref[...] = _dot(xb, w_ref[:, 0:DQ])
    k_ref[...] = _dot(xb, w_ref[:, DQ:DQ + DKV])
    v_ref[...] = _dot(xb, w_ref[:, DQ + DKV:DQ + 2 * DKV])
    o = DQ + 2 * DKV
    fg = _dot(xb, w_ref[:, o:o + LANES])[:, 0:N_HEADS] + bf_ref[...]
    lf_ref[...] = jax.nn.log_sigmoid(fg)


def _inproj_fox(x, w_pad, b_f):
    n = x.shape[0]
    cols = w_pad.shape[1]
    row = lambda c: pl.BlockSpec((ROW_TILE, c), lambda i: (i, 0))
    widths = (DQ, DKV, DKV, N_HEADS)
    return pl.pallas_call(
        _inproj_fox_kernel,
        grid=(n // ROW_TILE,),
        in_specs=[row(D_MODEL), pl.BlockSpec((D_MODEL, cols), lambda i: (0, 0)),
                  pl.BlockSpec((1, N_HEADS), lambda i: (0, 0))],
        out_specs=[row(c) for c in widths],
        out_shape=[jax.ShapeDtypeStruct((n, c), f32) for c in widths],
        compiler_params=_cparams(("arbitrary",)),
        name="inproj_fox",
    )(x, w_pad, b_f.reshape(1, N_HEADS))


QROWS = GROUP * Q_TILE


def _flash_block(q_ref, kbf, vbf, o_ref, qs, m_sc, acc_sc, n_chunks, bias_fn, last_bias_fn=None):
    for h in range(N_HEADS):
        g = h % GROUP
        qs[h // GROUP, g * Q_TILE:(g + 1) * Q_TILE, :] = (
            q_ref[:, h * HEAD_DIM:(h + 1) * HEAD_DIM] * (QK_SCALE * LOG2E)).astype(bf16)
    m_sc[...] = jnp.full(m_sc.shape, NEG, f32)
    acc_sc[...] = jnp.zeros(acc_sc.shape, f32)

    def body_with(bias):
        def body(c, _):
            off = pl.multiple_of(c * KEY_CHUNK, KEY_CHUNK)
            for kh in range(N_KV):
                kc = kbf[kh, pl.ds(off, KEY_CHUNK), :]
                vc = vbf[kh, pl.ds(off, KEY_CHUNK), :]
                lg = _dot_nt(qs[kh], kc) + bias(c, kh)
                m = m_sc[kh]
                part = lg[:, 0:LANES]
                for j in range(1, KEY_CHUNK // LANES):
                    part = jnp.maximum(part, lg[:, j * LANES:(j + 1) * LANES])
                m_new = jnp.maximum(m, jnp.max(part, axis=1, keepdims=True))
                p = jnp.exp2(lg - jnp.concatenate([m_new] * (KEY_CHUNK // LANES), axis=1))
                acc_sc[kh] = jnp.exp2(m - m_new) * acc_sc[kh] + _dot(p.astype(bf16), vc)
                m_sc[kh] = m_new
            return 0
        return body

    if last_bias_fn is None:
        lax.fori_loop(0, n_chunks, body_with(bias_fn), 0)
    else:
        lax.fori_loop(0, n_chunks - 1, body_with(bias_fn), 0)
        body_with(last_bias_fn)(n_chunks - 1, 0)
    for h in range(N_HEADS):
        kh, g = h // GROUP, h % GROUP
        acc = acc_sc[kh, g * Q_TILE:(g + 1) * Q_TILE, :]
        o_ref[:, h * HEAD_DIM:(h + 1) * HEAD_DIM] = acc[:, 0:HEAD_DIM] / acc[:, HEAD_DIM:2 * HEAD_DIM]


_FLASH_SCRATCH = [pltpu.VMEM((N_KV, QROWS, HEAD_DIM), bf16),
                  pltpu.VMEM((N_KV, QROWS, LANES), f32),
                  pltpu.VMEM((N_KV, QROWS, 2 * HEAD_DIM), f32)]


def _stage_kv(k_ref, v_ref, kbf, vbf):
    for h in range(N_KV):
        kbf[h] = k_ref[:, h * HEAD_DIM:(h + 1) * HEAD_DIM].astype(bf16)
        vbf[h, :, 0:HEAD_DIM] = v_ref[:, h * HEAD_DIM:(h + 1) * HEAD_DIM].astype(bf16)
        vbf[h, :, HEAD_DIM:2 * HEAD_DIM] = jnp.ones((v_ref.shape[0], HEAD_DIM), bf16)


def _dsa_prompt_kernel(q_ref, iq_ref, iw_ref, ik_ref, k_ref, v_ref, o_ref,
                       ikbf, kbf, vbf, key3, mb3, dist3, iqs, qs, m_sc, acc_sc):
    i = pl.program_id(1)

    @pl.when(i == 0)
    def _():
        ikbf[...] = ik_ref[...].astype(bf16)
        _stage_kv(k_ref, v_ref, kbf, vbf)

    n_chunks = (i * Q_TILE + Q_TILE + KEY_CHUNK - 1) // KEY_CHUNK
    row = lax.broadcasted_iota(i32, (Q_TILE, KEY_CHUNK), 0)
    col = lax.broadcasted_iota(i32, (Q_TILE, KEY_CHUNK), 1)
    qpos = i * Q_TILE + row
    w = iw_ref[...] * IDX_SCALE
    w_rep = [jnp.broadcast_to(w[:, h:h + 1], (Q_TILE, KEY_CHUNK)) for h in range(IDX_HEADS)]
    for h in range(IDX_HEADS):
        iqs[h * Q_TILE:(h + 1) * Q_TILE, :] = iq_ref[:, h * IDX_DIM:(h + 1) * IDX_DIM].astype(bf16)

    def score_body(c, _):
        ikc = ikbf[pl.ds(pl.multiple_of(c * KEY_CHUNK, KEY_CHUNK), KEY_CHUNK), :]
        rel = _dot_nt(iqs[...], ikc)
        s = jnp.zeros((Q_TILE, KEY_CHUNK), f32)
        for h in range(IDX_HEADS):
            s = s + w_rep[h] * jnp.maximum(rel[h * Q_TILE:(h + 1) * Q_TILE, :], 0.0)
        d = qpos - (c * KEY_CHUNK + col)
        key3[c] = jnp.where(d >= 0, _order_key(s), INT_MIN)
        dist3[c] = d.astype(f32)
        return 0

    lax.fori_loop(0, n_chunks, score_body, 0)

    def count(pred):
        def body(c, acc):
            return acc + jnp.where(pred(key3[c]), 1.0, 0.0)
        acc = lax.fori_loop(0, n_chunks, body, jnp.zeros((Q_TILE, KEY_CHUNK), f32))
        return jnp.sum(acc, axis=1, keepdims=True)

    def bit_body(it, t):
        cand = t ^ lax.shift_left(jnp.int32(1), 31 - it)
        return jnp.where(count(lambda kk: kk >= cand) >= TOPK, cand, t)

    thr = lax.fori_loop(0, 32, bit_body, jnp.full((Q_TILE, 1), INT_MIN, i32))
    thr = jnp.maximum(thr, INT_MIN + 1)
    need = TOPK - count(lambda kk: kk > thr)
    n_tie = count(lambda kk: kk == thr)
    has_excess = jnp.max(n_tie - need) > 0.0

    @pl.when(jnp.logical_not(has_excess))
    def _():
        def body(c, _):
            mb3[c] = jnp.where(key3[c] >= thr, 0.0, NEG)
            return 0
        lax.fori_loop(0, n_chunks, body, 0)

    @pl.when(has_excess)
    def _():
        tri = _upper_tri(KEY_CHUNK)

        def body(c, before):
            kk = key3[c]
            tie = jnp.where(kk == thr, 1.0, 0.0)
            rank = before + _dot(tie.astype(bf16), tri)
            sel = (kk > thr) | ((kk == thr) & (rank <= need))
            mb3[c] = jnp.where(sel, 0.0, NEG)
            return before + jnp.sum(tie, axis=1, keepdims=True)
        lax.fori_loop(0, n_chunks, body, jnp.zeros((Q_TILE, 1), f32))

    def bias(c, kh):
        mb = mb3[c]
        d = dist3[c]
        slabs = [mb - float(LOG2E * 2.0 ** (-8.0 * (kh * GROUP + g + 1) / N_HEADS)) * d for g in range(GROUP)]
        return jnp.concatenate(slabs, axis=0)

    _flash_block(q_ref, kbf, vbf, o_ref, qs, m_sc, acc_sc, n_chunks, bias)


def _dsa_prompt(q, iq, iw, ik, k, v, batch, seq):
    nq = seq // Q_TILE
    nc = seq // KEY_CHUNK
    qspec = lambda c: pl.BlockSpec((Q_TILE, c), lambda b, i: (b * nq + i, 0))
    bspec = lambda c: pl.BlockSpec((seq, c), lambda b, i: (b, 0))
    return pl.pallas_call(
        _dsa_prompt_kernel,
        grid=(batch, nq),
        in_specs=[qspec(DQ), qspec(IDX_HEADS * IDX_DIM), qspec(IDX_HEADS),
                  bspec(IDX_DIM), bspec(DKV), bspec(DKV)],
        out_specs=qspec(DQ),
        out_shape=jax.ShapeDtypeStruct((batch * seq, DQ), f32),
        scratch_shapes=[pltpu.VMEM((seq, IDX_DIM), bf16),
                        pltpu.VMEM((N_KV, seq, HEAD_DIM), bf16),
                        pltpu.VMEM((N_KV, seq, 2 * HEAD_DIM), bf16),
                        pltpu.VMEM((nc, Q_TILE, KEY_CHUNK), i32),
                        pltpu.VMEM((nc, Q_TILE, KEY_CHUNK), f32),
                        pltpu.VMEM((nc, Q_TILE, KEY_CHUNK), f32),
                        pltpu.VMEM((IDX_HEADS * Q_TILE, IDX_DIM), bf16)] + _FLASH_SCRATCH,
        compiler_params=_cparams(("arbitrary", "arbitrary")),
        name="dsa_prompt",
    )(q, iq, iw, ik, k, v)


def _cum_kernel(lf_ref, o_ref):
    tri = _upper_tri(KEY_CHUNK)
    carry = jnp.zeros((N_HEADS, 1), f32)
    for c in range(lf_ref.shape[0] // KEY_CHUNK):
        cum = _dot3_tn(lf_ref[c * KEY_CHUNK:(c + 1) * KEY_CHUNK, :], tri) + carry
        o_ref[0, c] = cum
        carry = cum[:, KEY_CHUNK - 1:KEY_CHUNK]


def _fox_cum(lf, batch, seq):
    nc = seq // KEY_CHUNK
    return pl.pallas_call(
        _cum_kernel,
        grid=(batch,),
        in_specs=[pl.BlockSpec((seq, N_HEADS), lambda b: (b, 0))],
        out_specs=pl.BlockSpec((1, nc, N_HEADS, KEY_CHUNK), lambda b: (b, 0, 0, 0)),
        out_shape=jax.ShapeDtypeStruct((batch, nc, N_HEADS, KEY_CHUNK), f32),
        compiler_params=_cparams(("arbitrary",)),
        name="fox_cum",
    )(lf)


def _fox_prompt_kernel(q_ref, cum_ref, k_ref, v_ref, o_ref, kbf, vbf, qs, m_sc, acc_sc):
    i = pl.program_id(1)

    @pl.when(i == 0)
    def _():
        _stage_kv(k_ref, v_ref, kbf, vbf)

    n_chunks = (i * Q_TILE + Q_TILE + KEY_CHUNK - 1) // KEY_CHUNK
    row = lax.broadcasted_iota(i32, (Q_TILE, KEY_CHUNK), 0)
    col = lax.broadcasted_iota(i32, (Q_TILE, KEY_CHUNK), 1)
    rel = i * Q_TILE + row - col

    def bias(c, kh, masked=False):
        ck = cum_ref[0, c] * (-LOG2E)
        causal = rel - c * KEY_CHUNK >= 0
        slabs = []
        for g in range(GROUP):
            row = jnp.broadcast_to(ck[kh * GROUP + g:kh * GROUP + g + 1, :], (Q_TILE, KEY_CHUNK))
            slabs.append(jnp.where(causal, row, NEG) if masked else row)
        return jnp.concatenate(slabs, axis=0)

    _flash_block(q_ref, kbf, vbf, o_ref, qs, m_sc, acc_sc, n_chunks, functools.partial(bias, masked=True))


def _fox_prompt(q, cum, k, v, batch, seq):
    nq = seq // Q_TILE
    nc = seq // KEY_CHUNK
    qspec = lambda c: pl.BlockSpec((Q_TILE, c), lambda b, i: (b * nq + i, 0))
    bspec = lambda c: pl.BlockSpec((seq, c), lambda b, i: (b, 0))
    return pl.pallas_call(
        _fox_prompt_kernel,
        grid=(batch, nq),
        in_specs=[qspec(DQ), pl.BlockSpec((1, nc, N_HEADS, KEY_CHUNK), lambda b, i: (b, 0, 0, 0)),
                  bspec(DKV), bspec(DKV)],
        out_specs=qspec(DQ),
        out_shape=jax.ShapeDtypeStruct((batch * seq, DQ), f32),
        scratch_shapes=[pltpu.VMEM((N_KV, seq, HEAD_DIM), bf16),
                        pltpu.VMEM((N_KV, seq, 2 * HEAD_DIM), bf16)] + _FLASH_SCRATCH,
        compiler_params=_cparams(("arbitrary", "arbitrary")),
        name="fox_prompt",
    )(q, cum, k, v)


SROWS = 64
SQ = SUBLANES


def _page_specs(shape, layer, n_pages, n_steps, group):
    def spec(j):
        def index(b, s, pt):
            return (layer, pt[b * n_pages + jnp.minimum(s, n_steps - 1) * group + j], 0, 0)
        return pl.BlockSpec((1, 1) + shape, index)
    return [spec(j) for j in range(group)]


def _dsa_sample_score_kernel(pt_ref, iq_ref, w_ref, *rest, n_steps, dec_seq, group):
    ikc_refs, ikn_ref, key_ref = rest[:group], rest[group], rest[group + 1]
    s_id = pl.program_id(1)
    iqb = iq_ref[0].astype(bf16)
    wcol = w_ref[0] * IDX_SCALE
    t = lax.broadcasted_iota(i32, (SQ, PAGE), 0)
    j = lax.broadcasted_iota(i32, (SQ, PAGE), 1)

    def score(ik_t):
        rel = _dot(iqb, ik_t.astype(bf16))
        return jnp.sum((jnp.maximum(rel, 0.0) * wcol).reshape(SQ, IDX_HEADS, PAGE), axis=1)

    @pl.when(s_id < n_steps)
    def _():
        for g in range(group):
            key_ref[0, 0, g] = jnp.where(t < dec_seq, _order_key(score(ikc_refs[g][0, 0])), INT_MIN)

    @pl.when(s_id == n_steps)
    def _():
        ok = (t < dec_seq) & (j <= t)
        key_ref[0, 0, 0] = jnp.where(ok, _order_key(score(ikn_ref[0])), INT_MIN)
        for g in range(1, group):
            key_ref[0, 0, g] = jnp.full((SQ, PAGE), INT_MIN, i32)


def _dsa_sample_scores(pt, iq_rows, w_rows, cache_ik, ik_new, layer, n_pages, dec_seq):
    nb = iq_rows.shape[0]
    group = PAGES_PER_STEP
    n_steps = n_pages // group
    kern = functools.partial(_dsa_sample_score_kernel, n_steps=n_steps, dec_seq=dec_seq, group=group)
    per_b = lambda b, s, pt: (b, 0, 0)
    return pl.pallas_call(
        kern,
        grid_spec=pltpu.PrefetchScalarGridSpec(
            num_scalar_prefetch=1,
            grid=(nb, n_steps + 1),
            in_specs=[pl.BlockSpec((1, SQ * IDX_HEADS, IDX_DIM), per_b),
                      pl.BlockSpec((1, SQ * IDX_HEADS, 1), per_b)]
                     + _page_specs((IDX_DIM, PAGE), layer, n_pages, n_steps, group)
                     + [pl.BlockSpec((1, IDX_DIM, PAGE), per_b)],
            out_specs=pl.BlockSpec((1, 1, group, SQ, PAGE), lambda b, s, pt: (b, s, 0, 0, 0)),
        ),
        out_shape=jax.ShapeDtypeStruct((nb, n_steps + 1, group, SQ, PAGE), i32),
        compiler_params=_cparams(("arbitrary", "arbitrary")),
        name="dsa_sample_scores",
    )(pt, iq_rows, w_rows, *([cache_ik] * group), ik_new)


def _sample_attn_kernel(pt_ref, q_ref, *rest, mode, n_pages, n_steps, dec_seq, group):
    kc_refs, vc_refs = rest[:group], rest[group:2 * group]
    rest = rest[2 * group:]
    if mode == "dsa":
        keys_ref, kn_ref, vn_ref, slope_ref, o_ref = rest[:5]
        m_sc, l_sc, acc_sc, carry_sc, thr_sc, need_sc = rest[5:]
    else:
        lf_refs, (kn_ref, vn_ref, lfn_ref, o_ref) = rest[:group], rest[group:group + 4]
        m_sc, l_sc, acc_sc, carry_sc = rest[group + 4:]
    s_id = pl.program_id(1)
    row = lax.broadcasted_iota(i32, (SROWS, PAGE), 0)
    col = lax.broadcasted_iota(i32, (SROWS, PAGE), 1)
    tok = row // N_HEADS
    qb = (q_ref[0] * QK_SCALE).astype(bf16)

    @pl.when(s_id == 0)
    def _():
        m_sc[...] = jnp.full(m_sc.shape, NEG, f32)
        l_sc[...] = jnp.zeros(l_sc.shape, f32)
        acc_sc[...] = jnp.zeros(acc_sc.shape, f32)
        carry_sc[...] = jnp.zeros(carry_sc.shape, f32)

    def update(ks, vs, biases):
        lg = jnp.concatenate([_dot(qb, k.astype(bf16)) + b for k, b in zip(ks, biases)], axis=1)
        m = m_sc[...]
        m_new = jnp.maximum(m, jnp.max(lg, axis=1, keepdims=True))
        a = jnp.exp(m - m_new)
        p = jnp.exp(lg - m_new)
        l_sc[...] = a * l_sc[...] + jnp.sum(p, axis=1, keepdims=True)
        pr = p.astype(bf16)
        pv = _dot_nt(pr[:, 0:PAGE], vs[0].astype(bf16))
        for g in range(1, len(vs)):
            pv = pv + _dot_nt(pr[:, g * PAGE:(g + 1) * PAGE], vs[g].astype(bf16))
        acc_sc[...] = a * acc_sc[...] + pv
        m_sc[...] = m_new

    if mode == "dsa":
        tri = _upper_tri(PAGE)

        @pl.when(s_id == 0)
        def _():
            kk = keys_ref[0].reshape((n_steps + 1) * group, SQ, PAGE)

            def count(pred):
                return jnp.sum(jnp.sum(jnp.where(pred(kk), 1.0, 0.0), axis=0), axis=1, keepdims=True)

            def bit_body(it, t):
                cand = t ^ lax.shift_left(jnp.int32(1), 31 - it)
                return jnp.where(count(lambda a: a >= cand[None]) >= TOPK, cand, t)

            thr = lax.fori_loop(0, 32, bit_body, jnp.full((SQ, 1), INT_MIN, i32))
            thr = jnp.maximum(thr, INT_MIN + 1)
            thr_sc[...] = thr
            need_sc[...] = TOPK - count(lambda a: a > thr[None])

        def dsa_biases(kps, first_page):
            thr, need = thr_sc[...], need_sc[...]
            before = carry_sc[0:SQ, :]
            out = []
            prefix = [_dot(jnp.where(kp == thr, 1.0, 0.0).astype(bf16), tri) for kp in kps]
            for g, kp in enumerate(kps):
                rank = before + prefix[g]
                sel = jnp.where((kp > thr) | ((kp == thr) & (rank <= need)), 1.0, 0.0)
                before = before + prefix[g][:, PAGE - 1:PAGE]
                sel_rows = jnp.concatenate(
                    [jnp.broadcast_to(sel[t:t + 1, :], (N_HEADS, PAGE)) for t in range(dec_seq)], axis=0)
                dist = (n_pages * PAGE + tok - ((first_page + g) * PAGE + col)).astype(f32)
                out.append(jnp.where(sel_rows > 0.5, -slope_ref[...] * dist, NEG))
            carry_sc[0:SQ, :] = before
            return out

        @pl.when(s_id < n_steps)
        def _():
            kps = [keys_ref[0, s_id, g] for g in range(group)]
            update([r[0, 0] for r in kc_refs], [r[0, 0] for r in vc_refs], dsa_biases(kps, s_id * group))

        @pl.when(s_id == n_steps)
        def _():
            update([kn_ref[0]], [vn_ref[0]], dsa_biases([keys_ref[0, n_steps, 0]], n_pages))
            o_ref[0] = acc_sc[...] / l_sc[...]
    else:
        tri = _upper_tri(PAGE)

        def fox_biases(lfs, mask_new):
            before = carry_sc[...]
            out = []
            prefix = [_dot3(lf_t, tri) for lf_t in lfs]
            for pre in prefix:
                cum = before + pre
                before = before + pre[:, PAGE - 1:PAGE]
                ck = jnp.concatenate([cum] * dec_seq, axis=0)
                out.append(jnp.where(col <= tok, -ck, NEG) if mask_new else -ck)
            carry_sc[...] = before
            return out

        @pl.when(s_id < n_steps)
        def _():
            update([r[0, 0] for r in kc_refs], [r[0, 0] for r in vc_refs],
                   fox_biases([r[0, 0] for r in lf_refs], False))

        @pl.when(s_id == n_steps)
        def _():
            update([kn_ref[0]], [vn_ref[0]], fox_biases([lfn_ref[0]], True))
            o_ref[0] = acc_sc[...] / l_sc[...]


def _sample_attn(mode, pt, q_rows, cache_k, cache_v, k_new, v_new, aux, aux_new, layer, n_pages, dec_seq):
    nb = q_rows.shape[0]
    group = PAGES_PER_STEP
    n_steps = n_pages // group
    kern = functools.partial(_sample_attn_kernel, mode=mode, n_pages=n_pages, n_steps=n_steps,
                             dec_seq=dec_seq, group=group)
    per_b = lambda b, s, pt: (b, 0, 0)
    kv_pages = _page_specs((DKV, PAGE), layer, n_pages, n_steps, group)
    new_page = pl.BlockSpec((1, DKV, PAGE), per_b)
    scratch = [pltpu.VMEM((SROWS, 1), f32), pltpu.VMEM((SROWS, 1), f32), pltpu.VMEM((SROWS, DKV), f32),
               pltpu.VMEM((N_HEADS, 1), f32)]
    if mode == "dsa":
        mid_specs = [pl.BlockSpec((1, n_steps + 1, group, SQ, PAGE), lambda b, s, pt: (b, 0, 0, 0, 0)),
                     new_page, new_page, pl.BlockSpec((SROWS, 1), lambda b, s, pt: (0, 0))]
        mid_args = [aux, k_new, v_new, aux_new]
        scratch += [pltpu.VMEM((SQ, 1), i32), pltpu.VMEM((SQ, 1), f32)]
    else:
        mid_specs = (_page_specs((N_HEADS, PAGE), layer, n_pages, n_steps, group)
                     + [new_page, new_page, pl.BlockSpec((1, N_HEADS, PAGE), per_b)])
        mid_args = [aux] * group + [k_new, v_new, aux_new]
    return pl.pallas_call(
        kern,
        grid_spec=pltpu.PrefetchScalarGridSpec(
            num_scalar_prefetch=1,
            grid=(nb, n_steps + 1),
            in_specs=[pl.BlockSpec((1, SROWS, DKV), per_b)] + kv_pages + kv_pages + mid_specs,
            out_specs=pl.BlockSpec((1, SROWS, DKV), per_b),
            scratch_shapes=scratch,
        ),
        out_shape=jax.ShapeDtypeStruct((nb, SROWS, DKV), f32),
        compiler_params=_cparams(("arbitrary", "arbitrary")),
        name=mode + "_sample_attn",
    )(pt, q_rows, *([cache_k] * group), *([cache_v] * group), *mid_args)


def _outproj_kernel(x_ref, op_ref, os_ref, w_ref, g_ref, b_ref, wr_hi_ref, wr_lo_ref, br_ref,
                    x1_ref, gate_ref, exp_t_ref, rank_t_ref, cnt_ref, run_sc, *, n_prompt_tiles):
    i = pl.program_id(0)
    o = jnp.where(i < n_prompt_tiles, op_ref[...], os_ref[...]).astype(bf16)
    x1 = _layer_norm(ALPHA * x_ref[...] + _dot(o, w_ref[...]), g_ref[...], b_ref[...])
    x1_ref[...] = x1

    hi = x1.astype(bf16)
    lo = (x1 - hi.astype(f32)).astype(bf16)
    lg = _dot(hi, wr_hi_ref[...]) + (_dot(lo, wr_hi_ref[...]) + _dot(hi, wr_lo_ref[...])) + br_ref[...]
    lane = lax.broadcasted_iota(i32, lg.shape, 1)
    out_lane = lax.broadcasted_iota(i32, (ROW_TILE, LANES), 1)
    vals = []
    experts = jnp.zeros((ROW_TILE, LANES), i32)
    for k in range(TOP_E):
        m = jnp.max(lg, axis=1, keepdims=True)
        idx = jnp.min(jnp.where(lg == m, lane, N_EXPERTS), axis=1, keepdims=True)
        vals.append(m)
        experts = jnp.where(out_lane == k, idx, experts)
        lg = jnp.where(lane == idx, -jnp.inf, lg)
    es = [jnp.exp(v - vals[0]) for v in vals]
    tot = es[0] + es[1] + es[2] + es[3]
    gates = jnp.zeros((ROW_TILE, LANES), f32)
    for k in range(TOP_E):
        gates = jnp.where(out_lane == k, es[k] / tot, gates)
    gate_ref[...] = gates

    @pl.when(i == 0)
    def _():
        run_sc[...] = jnp.zeros(run_sc.shape, f32)

    exp_t = experts.T[0:SUBLANES, :]
    e_iota = lax.broadcasted_iota(i32, (N_EXPERTS, ROW_TILE), 0)
    hits = [e_iota == exp_t[k:k + 1, :] for k in range(TOP_E)]
    member = jnp.where(hits[0] | hits[1] | hits[2] | hits[3], 1.0, 0.0)
    pos = run_sc[...] + _dot(member.astype(bf16), _upper_tri(ROW_TILE, strict=True))
    sub = lax.broadcasted_iota(i32, (SUBLANES, ROW_TILE), 0)
    rank_t = jnp.zeros((SUBLANES, ROW_TILE), f32)
    for k in range(TOP_E):
        rank_t = jnp.where(sub == k, jnp.sum(jnp.where(hits[k], pos, 0.0), axis=0, keepdims=True), rank_t)
    run_sc[...] = run_sc[...] + jnp.sum(member, axis=1, keepdims=True)
    exp_t_ref[...] = exp_t
    rank_t_ref[...] = rank_t.astype(i32)
    cnt_ref[...] = jnp.broadcast_to(run_sc[...], cnt_ref.shape).astype(i32)


def _outproj(x, o_prompt, o_sample, w_bf, g, b, wr_hi, wr_lo, br):
    n = x.shape[0]
    npt = o_prompt.shape[0] // ROW_TILE
    nst = o_sample.shape[0] // ROW_TILE
    row = lambda c: pl.BlockSpec((ROW_TILE, c), lambda i: (i, 0))
    col = pl.BlockSpec((SUBLANES, ROW_TILE), lambda i: (0, i))
    full = lambda r, c: pl.BlockSpec((r, c), lambda i: (0, 0))
    kern = functools.partial(_outproj_kernel, n_prompt_tiles=npt)
    return pl.pallas_call(
        kern,
        grid=(n // ROW_TILE,),
        in_specs=[row(D_MODEL),
                  pl.BlockSpec((ROW_TILE, DQ), lambda i: (jnp.minimum(i, npt - 1), 0)),
                  pl.BlockSpec((ROW_TILE, DQ), lambda i: (jnp.clip(i - npt, 0, nst - 1), 0)),
                  full(DQ, D_MODEL), full(1, D_MODEL), full(1, D_MODEL),
                  full(D_MODEL, N_EXPERTS), full(D_MODEL, N_EXPERTS), full(1, N_EXPERTS)],
        out_specs=[row(D_MODEL), row(LANES), col, col, full(N_EXPERTS, LANES)],
        out_shape=[jax.ShapeDtypeStruct((n, D_MODEL), f32),
                   jax.ShapeDtypeStruct((n, LANES), f32),
                   jax.ShapeDtypeStruct((SUBLANES, n), i32), jax.ShapeDtypeStruct((SUBLANES, n), i32),
                   jax.ShapeDtypeStruct((N_EXPERTS, LANES), i32)],
        scratch_shapes=[pltpu.VMEM((N_EXPERTS, 1), f32)],
        compiler_params=_cparams(("arbitrary",)),
        name="outproj_router",
    )(x, o_prompt, o_sample, w_bf, g, b, wr_hi, wr_lo, br)


def _moe_kernel(bexp_ref, used_ref, rows_ref, rows_next_ref, x_hbm, wgu_ref, bgu_ref, wdn_ref, bdn_ref,
                y_ref, wgu_bf, wdn_bf, xbuf, sem):
    i = pl.program_id(0)
    n_used = used_ref[0]
    slot = lax.rem(i, 2)
    e = bexp_ref[i]
    prev = bexp_ref[jnp.maximum(i - 1, 0)]

    def row_copy(tok, slot_, r):
        return pltpu.make_async_copy(x_hbm.at[pl.ds(tok, 1), :], xbuf.at[slot_, pl.ds(r, 1), :], sem.at[slot_])

    def start_block(tok_ref, slot_):
        for r in range(MOE_BLOCK):
            row_copy(tok_ref[0, 0, r], slot_, r).start()

    def wait_block():
        for r in range(MOE_BLOCK):
            row_copy(0, slot, r).wait()

    @pl.when(i == 0)
    def _():
        start_block(rows_ref, 0)

    @pl.when((i == 0) | (e != prev))
    def _():
        wgu_bf[...] = wgu_ref[0, 0].astype(bf16)
        wdn_bf[...] = wdn_ref[0, 0].astype(bf16)

    @pl.when(i < n_used)
    def _():
        wait_block()
        start_block(rows_next_ref, 1 - slot)
        h = _dot(xbuf[slot].astype(bf16), wgu_bf[...]) + bgu_ref[0, 0]
        gate = jnp.minimum(h[:, 0:D_FF], SWIGLU_LIMIT)
        up = jnp.clip(h[:, D_FF:2 * D_FF], -SWIGLU_LIMIT, SWIGLU_LIMIT)
        glu = gate * jax.nn.sigmoid(SWIGLU_ALPHA * gate)
        act = ((up + 1.0) * glu).astype(bf16)
        y_ref[...] = _dot(act, wdn_bf[...]) + bdn_ref[0, 0]

    @pl.when(i == n_used)
    def _():
        wait_block()

    @pl.when(i >= n_used)
    def _():
        y_ref[...] = jnp.zeros(y_ref.shape, f32)


def _moe(block_exp, n_used, row_tok, x, w_gu, b_gu, w_dn, b_dn, layer):
    n_blocks = block_exp.shape[0]
    rows3 = row_tok.reshape(n_blocks, 1, MOE_BLOCK)
    wspec = lambda r, c: pl.BlockSpec((1, 1, r, c), lambda i, be, nu: (layer, be[i], 0, 0))
    rows_spec = lambda ahead: pl.BlockSpec(
        (1, 1, MOE_BLOCK), lambda i, be, nu: (jnp.minimum(i + ahead, n_blocks - 1), 0, 0),
        memory_space=pltpu.SMEM)
    return pl.pallas_call(
        _moe_kernel,
        grid_spec=pltpu.PrefetchScalarGridSpec(
            num_scalar_prefetch=2,
            grid=(n_blocks,),
            in_specs=[rows_spec(0), rows_spec(1), pl.BlockSpec(memory_space=pl.ANY),
                      wspec(D_MODEL, 2 * D_FF), wspec(1, 2 * D_FF),
                      wspec(D_FF, D_MODEL), wspec(1, D_MODEL)],
            out_specs=pl.BlockSpec((MOE_BLOCK, D_MODEL), lambda i, be, nu: (i, 0)),
            scratch_shapes=[pltpu.VMEM((D_MODEL, 2 * D_FF), bf16), pltpu.VMEM((D_FF, D_MODEL), bf16),
                            pltpu.VMEM((2, MOE_BLOCK, D_MODEL), f32), pltpu.SemaphoreType.DMA((2,))],
        ),
        out_shape=jax.ShapeDtypeStruct((n_blocks * MOE_BLOCK, D_MODEL), f32),
        compiler_params=_cparams(("arbitrary",)),
        name="moe_experts",
    )(block_exp, n_used, rows3, rows3, x, w_gu, b_gu, w_dn, b_dn)


def _combine_kernel(dest_ref, dest_next_ref, x_ref, y_hbm, gate_ref, g_ref, b_ref, o_ref, ybuf, sem):
    i = pl.program_id(0)
    slot = lax.rem(i, 2)

    def row_copy(src_row, slot_, j):
        return pltpu.make_async_copy(y_hbm.at[pl.ds(src_row, 1), :], ybuf.at[slot_, pl.ds(j, 1), :], sem.at[slot_])

    def start_tile(ids_ref, slot_):
        for j in range(TOP_E * ROW_TILE):
            row_copy(ids_ref[0, 0, j], slot_, j).start(priority=j % 2)

    @pl.when(i == 0)
    def _():
        start_tile(dest_ref, 0)

    @pl.when(i + 1 < pl.num_programs(0))
    def _():
        start_tile(dest_next_ref, 1 - slot)

    for j in range(TOP_E * ROW_TILE):
        row_copy(0, slot, j).wait()
    gates = gate_ref[...]
    mix = gates[:, 0:1] * ybuf[slot, 0:ROW_TILE, :]
    for k in range(1, TOP_E):
        mix = mix + gates[:, k:k + 1] * ybuf[slot, k * ROW_TILE:(k + 1) * ROW_TILE, :]
    o_ref[...] = _layer_norm(ALPHA * x_ref[...] + mix, g_ref[...], b_ref[...])


def _combine(x1, y_rows, dest, gates, g, b):
    n = x1.shape[0]
    n_tiles = n // ROW_TILE
    ids = dest.reshape(TOP_E, n_tiles, ROW_TILE).transpose(1, 0, 2).reshape(n_tiles, 1, TOP_E * ROW_TILE)
    row = lambda c: pl.BlockSpec((ROW_TILE, c), lambda i: (i, 0))
    full = lambda r, c: pl.BlockSpec((r, c), lambda i: (0, 0))
    ids_spec = lambda ahead: pl.BlockSpec(
        (1, 1, TOP_E * ROW_TILE), lambda i: (jnp.minimum(i + ahead, n_tiles - 1), 0, 0),
        memory_space=pltpu.SMEM)
    return pl.pallas_call(
        _combine_kernel,
        grid=(n_tiles,),
        in_specs=[ids_spec(0), ids_spec(1), row(D_MODEL), pl.BlockSpec(memory_space=pl.ANY),
                  row(LANES), full(1, D_MODEL), full(1, D_MODEL)],
        out_specs=row(D_MODEL),
        out_shape=jax.ShapeDtypeStruct((n, D_MODEL), f32),
        scratch_shapes=[pltpu.VMEM((2, TOP_E * ROW_TILE, D_MODEL), f32), pltpu.SemaphoreType.DMA((2,))],
        compiler_params=_cparams(("arbitrary",)),
        name="combine_norm",
    )(ids, ids, x1, y_rows, gates, g, b)


def _route(exp_t, rank_t, counts, n_tok):
    padded = (counts + MOE_BLOCK - 1) // MOE_BLOCK * MOE_BLOCK
    pad_end = jnp.cumsum(padded)
    pad_start = pad_end - padded
    hit = exp_t[:, :, None] == jnp.arange(N_EXPERTS, dtype=i32)
    dest = (jnp.sum(jnp.where(hit, pad_start, 0), axis=-1) + rank_t).reshape(-1)
    n_blocks = -(-n_tok * TOP_E // MOE_BLOCK) + N_EXPERTS + 1
    tok = jnp.tile(jnp.arange(n_tok, dtype=i32), TOP_E)
    row_tok = jnp.zeros((n_blocks * MOE_BLOCK,), i32).at[dest].set(tok)
    block_start = jnp.arange(n_blocks, dtype=i32) * MOE_BLOCK
    block_exp = jnp.minimum(jnp.sum(pad_end[None, :] <= block_start[:, None], axis=1), N_EXPERTS - 1).astype(i32)
    n_used = (pad_end[-1:] // MOE_BLOCK).astype(i32)
    return row_tok, block_exp, n_used, dest


def _pad_cols(w):
    cols = w.shape[1]
    padded = -(-cols // LANES) * LANES
    return jnp.pad(w, ((0, 0), (0, padded - cols))).astype(bf16)


def _diag_heads(o_full, nb, dec_seq):
    o6 = o_full.reshape(nb, dec_seq, N_KV, GROUP, N_KV, HEAD_DIM)
    own = jnp.eye(N_KV, dtype=jnp.bool_).reshape(1, 1, N_KV, 1, N_KV, 1)
    return jnp.sum(jnp.where(own, o6, 0.0), axis=4).reshape(nb * dec_seq, DQ)


def _block_diag_q(q_s, nb, dec_seq):
    q5 = q_s.reshape(nb, dec_seq, N_KV, GROUP, 1, HEAD_DIM)
    eye = jnp.eye(N_KV, dtype=q_s.dtype).reshape(1, 1, N_KV, 1, N_KV, 1)
    return (q5 * eye).reshape(nb, SROWS, DKV)


def _pad_page(a, nb, dec_seq):
    a3 = a.reshape(nb, dec_seq, a.shape[-1])
    return jnp.pad(a3, ((0, 0), (0, PAGE - dec_seq), (0, 0)))


def kernel(x_prompt, x_sample, cache_dsa_k, cache_dsa_v, cache_dsa_ik, cache_fox_k, cache_fox_v,
           cache_fox_logf, page_table, w_in_dsa, w_out_dsa, w_in_fox, b_forget, w_out_fox,
           ln_gain, ln_bias, w_router, b_router, w_gate_up, b_gate_up, w_down, b_down):
    batch, seq, _ = x_prompt.shape
    nb, dec_seq, _ = x_sample.shape
    n_pages = page_table.shape[1]
    pool = cache_dsa_k.shape[1]
    n_p = batch * seq
    n_s = nb * dec_seq
    n_tok = n_p + n_s
    assert dec_seq * N_HEADS == SROWS and n_s == ROW_TILE and n_p % ROW_TILE == 0
    assert n_pages % PAGES_PER_STEP == 0 and seq % KEY_CHUNK == 0

    x = jnp.concatenate([x_prompt.reshape(n_p, D_MODEL), x_sample.reshape(n_s, D_MODEL)], axis=0)
    pt = page_table.reshape(-1).astype(i32)
    slopes = 2.0 ** (-8.0 * jnp.arange(1, N_HEADS + 1, dtype=f32) / N_HEADS)
    slope_rows = jnp.tile(slopes, dec_seq).reshape(SROWS, 1)

    kv_t = lambda c: jnp.transpose(c, (0, 1, 3, 4, 2)).reshape(-1, pool, DKV, PAGE)
    dsa_k, dsa_v, fox_k, fox_v = kv_t(cache_dsa_k), kv_t(cache_dsa_v), kv_t(cache_fox_k), kv_t(cache_fox_v)
    dsa_ik_t = jnp.swapaxes(cache_dsa_ik, 2, 3)
    fox_lf_t = jnp.swapaxes(cache_fox_logf, 2, 3)
    new_t = lambda a: jnp.swapaxes(_pad_page(a, nb, dec_seq), 1, 2)
    b_gu4 = b_gate_up.reshape(DEPTH, N_EXPERTS, 1, 2 * D_FF)
    b_dn4 = b_down.reshape(DEPTH, N_EXPERTS, 1, D_MODEL)

    outs = {name: [] for name in ("dk", "dv", "di", "fk", "fv", "fl")}
    for layer in range(DEPTH):
        j = layer // 2
        if layer % 2 == 0:
            q, k, v, iq, ik, iw = _inproj_dsa(x, _pad_cols(w_in_dsa[j]))
            o_p = _dsa_prompt(q, iq, iw, ik, k, v, batch, seq)
            iq_rows = jnp.pad(iq[n_p:].reshape(nb, dec_seq * IDX_HEADS, IDX_DIM),
                              ((0, 0), (0, (SQ - dec_seq) * IDX_HEADS), (0, 0)))
            w_rows = jnp.pad(iw[n_p:].reshape(nb, dec_seq * IDX_HEADS, 1),
                             ((0, 0), (0, (SQ - dec_seq) * IDX_HEADS), (0, 0)))
            keys = _dsa_sample_scores(pt, iq_rows, w_rows, dsa_ik_t, new_t(ik[n_p:]),
                                      j, n_pages, dec_seq)
            o_s = _sample_attn("dsa", pt, _block_diag_q(q[n_p:], nb, dec_seq), dsa_k, dsa_v,
                               new_t(k[n_p:]), new_t(v[n_p:]),
                               keys, slope_rows, j, n_pages, dec_seq)
            w_out = w_out_dsa[j]
            outs["dk"].append(k); outs["dv"].append(v); outs["di"].append(ik)
        else:
            q, k, v, lf = _inproj_fox(x, _pad_cols(w_in_fox[j]), b_forget[j])
            cum = _fox_cum(lf, batch, seq)
            o_p = _fox_prompt(q, cum, k, v, batch, seq)
            o_s = _sample_attn("fox", pt, _block_diag_q(q[n_p:], nb, dec_seq), fox_k, fox_v,
                               new_t(k[n_p:]), new_t(v[n_p:]),
                               fox_lf_t, new_t(lf[n_p:]), j, n_pages, dec_seq)
            w_out = w_out_fox[j]
            outs["fk"].append(k); outs["fv"].append(v); outs["fl"].append(lf)

        wr = w_router[layer]
        wr_hi = wr.astype(bf16)
        wr_lo = (wr - wr_hi.astype(f32)).astype(bf16)
        x1, gates, exp_t, rank_t, cnt = _outproj(
            x, o_p, _diag_heads(o_s, nb, dec_seq), w_out.astype(bf16),
            ln_gain[layer, 0].reshape(1, D_MODEL), ln_bias[layer, 0].reshape(1, D_MODEL),
            wr_hi, wr_lo, b_router[layer].reshape(1, N_EXPERTS))

        row_tok, block_exp, n_used, dest = _route(exp_t[:TOP_E], rank_t[:TOP_E], cnt[:, 0], n_tok)
        y_rows = _moe(block_exp, n_used, row_tok, x1, w_gate_up, b_gu4, w_down, b_dn4, layer)
        x = _combine(x1, y_rows, dest, gates, ln_gain[layer, 1].reshape(1, D_MODEL),
                     ln_bias[layer, 1].reshape(1, D_MODEL))

    def stack(name, shape_p, shape_s):
        a = jnp.stack(outs[name])
        return a[:, :n_p].reshape((-1,) + shape_p), a[:, n_p:].reshape((-1,) + shape_s)

    kv_p, kv_s = (batch, seq, N_KV, HEAD_DIM), (nb, dec_seq, N_KV, HEAD_DIM)
    dk_p, dk_s = stack("dk", kv_p, kv_s)
    dv_p, dv_s = stack("dv", kv_p, kv_s)
    di_p, di_s = stack("di", (batch, seq, IDX_DIM), (nb, dec_seq, IDX_DIM))
    fk_p, fk_s = stack("fk", kv_p, kv_s)
    fv_p, fv_s = stack("fv", kv_p, kv_s)
    fl_p, fl_s = stack("fl", (batch, seq, N_HEADS), (nb, dec_seq, N_HEADS))
    return (x[:n_p].reshape(batch, seq, D_MODEL), x[n_p:].reshape(nb, dec_seq, D_MODEL),
            dk_p, dv_p, di_p, fk_p, fv_p, fl_p, dk_s, dv_s, di_s, fk_s, fv_s, fl_s)
```

```python
import functools

import jax
import jax.numpy as jnp
from jax import lax
from jax.experimental import pallas as pl
from jax.experimental.pallas import tpu as pltpu

f32 = jnp.float32
bf16 = jnp.bfloat16
i32 = jnp.int32

D_MODEL = 1024
DEPTH = 4
PAGE = 128
N_HEADS = 16
HEAD_DIM = 64
N_KV = 4
GROUP = N_HEADS // N_KV
DQ = N_HEADS * HEAD_DIM
DKV = N_KV * HEAD_DIM
IDX_HEADS = 8
IDX_DIM = 64
TOPK = 256
N_EXPERTS = 32
TOP_E = 4
D_FF = D_MODEL
SWIGLU_LIMIT = 7.0
SWIGLU_ALPHA = 1.702
MOE_BLOCK = 256
LN_EPS = 1e-5
ALPHA = (2 * DEPTH) ** 0.25
IDX_SCALE = IDX_HEADS ** -0.5 * IDX_DIM ** -0.5
QK_SCALE = HEAD_DIM ** -0.5
LOG2E = 1.4426950408889634

LANES = 128
SUBLANES = 8
Q_TILE = 128
KEY_CHUNK = 256
ROW_TILE = 128
INPROJ_TILE = 384
PAGES_PER_STEP = 16
INT_MIN = -2 ** 31
NEG = -1e30
VMEM_LIMIT = 56 * 1024 * 1024

NT = (((1,), (1,)), ((), ()))


def _cparams(sem):
    return pltpu.CompilerParams(dimension_semantics=sem, vmem_limit_bytes=VMEM_LIMIT)


def _split3(x):
    hi = x.astype(bf16)
    r1 = x - hi.astype(f32)
    mid = r1.astype(bf16)
    lo = (r1 - mid.astype(f32)).astype(bf16)
    return hi, mid, lo


def _dot(a, b):
    return jnp.dot(a, b, preferred_element_type=f32)


def _dot_nt(a, b):
    return lax.dot_general(a, b, NT, preferred_element_type=f32)


def _dot3(x, w_bf):
    hi, mid, lo = _split3(x)
    return _dot(hi, w_bf) + _dot(mid, w_bf) + _dot(lo, w_bf)


def _dot3_tn(x, w_bf):
    tn = (((0,), (0,)), ((), ()))
    return sum(lax.dot_general(part, w_bf, tn, preferred_element_type=f32) for part in _split3(x))


def _layer_norm(y, g, b):
    mu = jnp.mean(y, axis=-1, keepdims=True)
    d = y - mu
    var = jnp.mean(d * d, axis=-1, keepdims=True)
    return d * lax.rsqrt(var + LN_EPS) * g + b


def _order_key(s):
    s = jnp.where(s == 0.0, 0.0, s)
    bits = pltpu.bitcast(s, i32)
    return bits ^ ((bits >> 31) & 0x7FFFFFFF)


def _upper_tri(n, strict=False):
    r = lax.broadcasted_iota(i32, (n, n), 0)
    c = lax.broadcasted_iota(i32, (n, n), 1)
    return jnp.where((r < c) if strict else (r <= c), 1.0, 0.0).astype(bf16)


def _inproj_dsa_kernel(x_ref, w_ref, q_ref, k_ref, v_ref, iq_ref, ik_ref, iw_ref):
    xb = x_ref[...].astype(bf16)
    q_ref[...] = _dot(xb, w_ref[:, 0:DQ])
    k_ref[...] = _dot(xb, w_ref[:, DQ:DQ + DKV])
    v_ref[...] = _dot(xb, w_ref[:, DQ + DKV:DQ + 2 * DKV])
    o = DQ + 2 * DKV
    iq_ref[...] = _dot(xb, w_ref[:, o:o + IDX_HEADS * IDX_DIM])
    o += IDX_HEADS * IDX_DIM
    tail = _dot(xb, w_ref[:, o:o + LANES])
    ik_ref[...] = tail[:, 0:IDX_DIM]
    iw_ref[...] = tail[:, IDX_DIM:IDX_DIM + IDX_HEADS]


def _inproj_dsa(x, w_pad):
    n = x.shape[0]
    cols = w_pad.shape[1]
    row = lambda c: pl.BlockSpec((INPROJ_TILE, c), lambda i: (i, 0))
    widths = (DQ, DKV, DKV, IDX_HEADS * IDX_DIM, IDX_DIM, IDX_HEADS)
    return pl.pallas_call(
        _inproj_dsa_kernel,
        grid=(n // INPROJ_TILE,),
        in_specs=[row(D_MODEL), pl.BlockSpec((D_MODEL, cols), lambda i: (0, 0))],
        out_specs=[row(c) for c in widths],
        out_shape=[jax.ShapeDtypeStruct((n, c), f32) for c in widths],
        compiler_params=_cparams(("arbitrary",)),
        name="inproj_dsa",
    )(x, w_pad)


def _inproj_fox_kernel(x_ref, w_ref, bf_ref, q_ref, k_ref, v_ref, lf_ref):
    xb = x_ref[...].astype(bf16)
    q_ref[...] = _dot(xb, w_ref[:, 0:DQ])
    k_ref[...] = _dot(xb, w_ref[:, DQ:DQ + DKV])
    v_ref[...] = _dot(xb, w_ref[:, DQ + DKV:DQ + 2 * DKV])
    o = DQ + 2 * DKV
    fg = _dot(xb, w_ref[:, o:o + LANES])[:, 0:N_HEADS] + bf_ref[...]
    lf_ref[...] = jax.nn.log_sigmoid(fg)


def _inproj_fox(x, w_pad, b_f):
    n = x.shape[0]
    cols = w_pad.shape[1]
    row = lambda c: pl.BlockSpec((INPROJ_TILE, c), lambda i: (i, 0))
    widths = (DQ, DKV, DKV, N_HEADS)
    return pl.pallas_call(
        _inproj_fox_kernel,
        grid=(n // INPROJ_TILE,),
        in_specs=[row(D_MODEL), pl.BlockSpec((D_MODEL, cols), lambda i: (0, 0)),
                  pl.BlockSpec((1, N_HEADS), lambda i: (0, 0))],
        out_specs=[row(c) for c in widths],
        out_shape=[jax.ShapeDtypeStruct((n, c), f32) for c in widths],
        compiler_params=_cparams(("arbitrary",)),
        name="inproj_fox",
    )(x, w_pad, b_f.reshape(1, N_HEADS))


QROWS = GROUP * Q_TILE


def _flash_block(q_ref, kbf, vbf, o_ref, qs, m_sc, acc_sc, n_chunks, bias_fn):
    for h in range(N_HEADS):
        g = h % GROUP
        qs[h // GROUP, g * Q_TILE:(g + 1) * Q_TILE, :] = (
            q_ref[:, h * HEAD_DIM:(h + 1) * HEAD_DIM] * (QK_SCALE * LOG2E)).astype(bf16)
    m_sc[...] = jnp.full(m_sc.shape, NEG, f32)
    acc_sc[...] = jnp.zeros(acc_sc.shape, f32)

    def body(c, _):
        off = pl.multiple_of(c * KEY_CHUNK, KEY_CHUNK)
        for kh in range(N_KV):
            kc = kbf[kh, pl.ds(off, KEY_CHUNK), :]
            vc = vbf[kh, pl.ds(off, KEY_CHUNK), :]
            lg = _dot_nt(qs[kh], kc) + bias_fn(c, kh)
            m = m_sc[kh]
            part = lg[:, 0:LANES]
            for j in range(1, KEY_CHUNK // LANES):
                part = jnp.maximum(part, lg[:, j * LANES:(j + 1) * LANES])
            m_new = jnp.maximum(m, jnp.max(part, axis=1, keepdims=True))
            p = jnp.exp2(lg - jnp.concatenate([m_new] * (KEY_CHUNK // LANES), axis=1))
            acc_sc[kh] = jnp.exp2(m - m_new) * acc_sc[kh] + _dot(p.astype(bf16), vc)
            m_sc[kh] = m_new
        return 0

    lax.fori_loop(0, n_chunks, body, 0)
    for h in range(N_HEADS):
        kh, g = h // GROUP, h % GROUP
        acc = acc_sc[kh, g * Q_TILE:(g + 1) * Q_TILE, :]
        o_ref[:, h * HEAD_DIM:(h + 1) * HEAD_DIM] = acc[:, 0:HEAD_DIM] / acc[:, HEAD_DIM:2 * HEAD_DIM]


_FLASH_SCRATCH = [pltpu.VMEM((N_KV, QROWS, HEAD_DIM), bf16),
                  pltpu.VMEM((N_KV, QROWS, LANES), f32),
                  pltpu.VMEM((N_KV, QROWS, 2 * HEAD_DIM), f32)]


def _stage_kv(k_ref, v_ref, kbf, vbf):
    for h in range(N_KV):
        kbf[h] = k_ref[:, h * HEAD_DIM:(h + 1) * HEAD_DIM].astype(bf16)
        vbf[h, :, 0:HEAD_DIM] = v_ref[:, h * HEAD_DIM:(h + 1) * HEAD_DIM].astype(bf16)
        vbf[h, :, HEAD_DIM:2 * HEAD_DIM] = jnp.ones((v_ref.shape[0], HEAD_DIM), bf16)


def _dsa_prompt_kernel(q_ref, iq_ref, iw_ref, ik_ref, k_ref, v_ref, o_ref,
                       ikbf, kbf, vbf, key3, mb3, dist3, iqs, qs, m_sc, acc_sc):
    i = pl.program_id(1)

    @pl.when(i == 0)
    def _():
        ikbf[...] = ik_ref[...].astype(bf16)
        _stage_kv(k_ref, v_ref, kbf, vbf)

    n_chunks = (i * Q_TILE + Q_TILE + KEY_CHUNK - 1) // KEY_CHUNK
    row = lax.broadcasted_iota(i32, (Q_TILE, KEY_CHUNK), 0)
    col = lax.broadcasted_iota(i32, (Q_TILE, KEY_CHUNK), 1)
    qpos = i * Q_TILE + row
    w = iw_ref[...] * IDX_SCALE
    w_rep = [jnp.broadcast_to(w[:, h:h + 1], (Q_TILE, KEY_CHUNK)) for h in range(IDX_HEADS)]
    for h in range(IDX_HEADS):
        iqs[h * Q_TILE:(h + 1) * Q_TILE, :] = iq_ref[:, h * IDX_DIM:(h + 1) * IDX_DIM].astype(bf16)

    def score_body(c, _):
        ikc = ikbf[pl.ds(pl.multiple_of(c * KEY_CHUNK, KEY_CHUNK), KEY_CHUNK), :]
        rel = _dot_nt(iqs[...], ikc)
        s = jnp.zeros((Q_TILE, KEY_CHUNK), f32)
        for h in range(IDX_HEADS):
            s = s + w_rep[h] * jnp.maximum(rel[h * Q_TILE:(h + 1) * Q_TILE, :], 0.0)
        d = qpos - (c * KEY_CHUNK + col)
        key3[c] = jnp.where(d >= 0, _order_key(s), INT_MIN)
        dist3[c] = d.astype(f32)
        return 0

    lax.fori_loop(0, n_chunks, score_body, 0)

    def count(pred):
        def body(c, acc):
            return acc + jnp.where(pred(key3[c]), 1.0, 0.0)
        acc = lax.fori_loop(0, n_chunks, body, jnp.zeros((Q_TILE, KEY_CHUNK), f32))
        return jnp.sum(acc, axis=1, keepdims=True)

    def bit_body(it, t):
        cand = t ^ lax.shift_left(jnp.int32(1), 31 - it)
        return jnp.where(count(lambda kk: kk >= cand) >= TOPK, cand, t)

    thr = lax.fori_loop(0, 32, bit_body, jnp.full((Q_TILE, 1), INT_MIN, i32))
    thr = jnp.maximum(thr, INT_MIN + 1)
    need = TOPK - count(lambda kk: kk > thr)
    n_tie = count(lambda kk: kk == thr)
    has_excess = jnp.max(n_tie - need) > 0.0

    @pl.when(jnp.logical_not(has_excess))
    def _():
        def body(c, _):
            mb3[c] = jnp.where(key3[c] >= thr, 0.0, NEG)
            return 0
        lax.fori_loop(0, n_chunks, body, 0)

    @pl.when(has_excess)
    def _():
        tri = _upper_tri(KEY_CHUNK)

        def body(c, before):
            kk = key3[c]
            tie = jnp.where(kk == thr, 1.0, 0.0)
            rank = before + _dot(tie.astype(bf16), tri)
            sel = (kk > thr) | ((kk == thr) & (rank <= need))
            mb3[c] = jnp.where(sel, 0.0, NEG)
            return before + jnp.sum(tie, axis=1, keepdims=True)
        lax.fori_loop(0, n_chunks, body, jnp.zeros((Q_TILE, 1), f32))

    def bias(c, kh):
        mb = mb3[c]
        d = dist3[c]
        slabs = [mb - float(LOG2E * 2.0 ** (-8.0 * (kh * GROUP + g + 1) / N_HEADS)) * d for g in range(GROUP)]
        return jnp.concatenate(slabs, axis=0)

    _flash_block(q_ref, kbf, vbf, o_ref, qs, m_sc, acc_sc, n_chunks, bias)


def _dsa_prompt(q, iq, iw, ik, k, v, batch, seq):
    nq = seq // Q_TILE
    nc = seq // KEY_CHUNK
    qspec = lambda c: pl.BlockSpec((Q_TILE, c), lambda b, i: (b * nq + i, 0))
    bspec = lambda c: pl.BlockSpec((seq, c), lambda b, i: (b, 0))
    return pl.pallas_call(
        _dsa_prompt_kernel,
        grid=(batch, nq),
        in_specs=[qspec(DQ), qspec(IDX_HEADS * IDX_DIM), qspec(IDX_HEADS),
                  bspec(IDX_DIM), bspec(DKV), bspec(DKV)],
        out_specs=qspec(DQ),
        out_shape=jax.ShapeDtypeStruct((batch * seq, DQ), f32),
        scratch_shapes=[pltpu.VMEM((seq, IDX_DIM), bf16),
                        pltpu.VMEM((N_KV, seq, HEAD_DIM), bf16),
                        pltpu.VMEM((N_KV, seq, 2 * HEAD_DIM), bf16),
                        pltpu.VMEM((nc, Q_TILE, KEY_CHUNK), i32),
                        pltpu.VMEM((nc, Q_TILE, KEY_CHUNK), f32),
                        pltpu.VMEM((nc, Q_TILE, KEY_CHUNK), f32),
                        pltpu.VMEM((IDX_HEADS * Q_TILE, IDX_DIM), bf16)] + _FLASH_SCRATCH,
        compiler_params=_cparams(("arbitrary", "arbitrary")),
        name="dsa_prompt",
    )(q, iq, iw, ik, k, v)


def _cum_kernel(lf_ref, o_ref):
    tri = _upper_tri(KEY_CHUNK)
    carry = jnp.zeros((N_HEADS, 1), f32)
    for c in range(lf_ref.shape[0] // KEY_CHUNK):
        cum = _dot3_tn(lf_ref[c * KEY_CHUNK:(c + 1) * KEY_CHUNK, :], tri) + carry
        o_ref[0, c] = cum
        carry = cum[:, KEY_CHUNK - 1:KEY_CHUNK]


def _fox_cum(lf, batch, seq):
    nc = seq // KEY_CHUNK
    return pl.pallas_call(
        _cum_kernel,
        grid=(batch,),
        in_specs=[pl.BlockSpec((seq, N_HEADS), lambda b: (b, 0))],
        out_specs=pl.BlockSpec((1, nc, N_HEADS, KEY_CHUNK), lambda b: (b, 0, 0, 0)),
        out_shape=jax.ShapeDtypeStruct((batch, nc, N_HEADS, KEY_CHUNK), f32),
        compiler_params=_cparams(("arbitrary",)),
        name="fox_cum",
    )(lf)


def _fox_prompt_kernel(q_ref, cum_ref, k_ref, v_ref, o_ref, kbf, vbf, qs, m_sc, acc_sc):
    i = pl.program_id(1)

    @pl.when(i == 0)
    def _():
        _stage_kv(k_ref, v_ref, kbf, vbf)

    n_chunks = (i * Q_TILE + Q_TILE + KEY_CHUNK - 1) // KEY_CHUNK
    row = lax.broadcasted_iota(i32, (Q_TILE, KEY_CHUNK), 0)
    col = lax.broadcasted_iota(i32, (Q_TILE, KEY_CHUNK), 1)
    rel = i * Q_TILE + row - col

    def bias(c, kh):
        ck = cum_ref[0, c] * (-LOG2E)
        causal = rel - c * KEY_CHUNK >= 0
        slabs = [jnp.where(causal, ck[kh * GROUP + g:kh * GROUP + g + 1, :], NEG) for g in range(GROUP)]
        return jnp.concatenate(slabs, axis=0)

    _flash_block(q_ref, kbf, vbf, o_ref, qs, m_sc, acc_sc, n_chunks, bias)


def _fox_prompt(q, cum, k, v, batch, seq):
    nq = seq // Q_TILE
    nc = seq // KEY_CHUNK
    qspec = lambda c: pl.BlockSpec((Q_TILE, c), lambda b, i: (b * nq + i, 0))
    bspec = lambda c: pl.BlockSpec((seq, c), lambda b, i: (b, 0))
    return pl.pallas_call(
        _fox_prompt_kernel,
        grid=(batch, nq),
        in_specs=[qspec(DQ), pl.BlockSpec((1, nc, N_HEADS, KEY_CHUNK), lambda b, i: (b, 0, 0, 0)),
                  bspec(DKV), bspec(DKV)],
        out_specs=qspec(DQ),
        out_shape=jax.ShapeDtypeStruct((batch * seq, DQ), f32),
        scratch_shapes=[pltpu.VMEM((N_KV, seq, HEAD_DIM), bf16),
                        pltpu.VMEM((N_KV, seq, 2 * HEAD_DIM), bf16)] + _FLASH_SCRATCH,
        compiler_params=_cparams(("arbitrary", "arbitrary")),
        name="fox_prompt",
    )(q, cum, k, v)


SROWS = 64
SQ = SUBLANES


def _page_specs(shape, layer, n_pages, n_steps, group):
    def spec(j):
        def index(b, s, pt):
            return (layer, pt[b * n_pages + jnp.minimum(s, n_steps - 1) * group + j], 0, 0)
        return pl.BlockSpec((1, 1) + shape, index)
    return [spec(j) for j in range(group)]


def _dsa_sample_score_kernel(pt_ref, iq_ref, w_ref, *rest, n_steps, dec_seq, group):
    ikc_refs, ikn_ref, key_ref = rest[:group], rest[group], rest[group + 1]
    s_id = pl.program_id(1)
    iqb = iq_ref[0].astype(bf16)
    wcol = w_ref[0] * IDX_SCALE
    t = lax.broadcasted_iota(i32, (SQ, PAGE), 0)
    j = lax.broadcasted_iota(i32, (SQ, PAGE), 1)

    def score(ik_t):
        rel = _dot(iqb, ik_t.astype(bf16))
        return jnp.sum((jnp.maximum(rel, 0.0) * wcol).reshape(SQ, IDX_HEADS, PAGE), axis=1)

    @pl.when(s_id < n_steps)
    def _():
        for g in range(group):
            key_ref[0, 0, g] = jnp.where(t < dec_seq, _order_key(score(ikc_refs[g][0, 0])), INT_MIN)

    @pl.when(s_id == n_steps)
    def _():
        ok = (t < dec_seq) & (j <= t)
        key_ref[0, 0, 0] = jnp.where(ok, _order_key(score(ikn_ref[0])), INT_MIN)
        for g in range(1, group):
            key_ref[0, 0, g] = jnp.full((SQ, PAGE), INT_MIN, i32)


def _dsa_sample_scores(pt, iq_rows, w_rows, cache_ik, ik_new, layer, n_pages, dec_seq):
    nb = iq_rows.shape[0]
    group = PAGES_PER_STEP
    n_steps = n_pages // group
    kern = functools.partial(_dsa_sample_score_kernel, n_steps=n_steps, dec_seq=dec_seq, group=group)
    per_b = lambda b, s, pt: (b, 0, 0)
    return pl.pallas_call(
        kern,
        grid_spec=pltpu.PrefetchScalarGridSpec(
            num_scalar_prefetch=1,
            grid=(nb, n_steps + 1),
            in_specs=[pl.BlockSpec((1, SQ * IDX_HEADS, IDX_DIM), per_b),
                      pl.BlockSpec((1, SQ * IDX_HEADS, 1), per_b)]
                     + _page_specs((IDX_DIM, PAGE), layer, n_pages, n_steps, group)
                     + [pl.BlockSpec((1, IDX_DIM, PAGE), per_b)],
            out_specs=pl.BlockSpec((1, 1, group, SQ, PAGE), lambda b, s, pt: (b, s, 0, 0, 0)),
        ),
        out_shape=jax.ShapeDtypeStruct((nb, n_steps + 1, group, SQ, PAGE), i32),
        compiler_params=_cparams(("arbitrary", "arbitrary")),
        name="dsa_sample_scores",
    )(pt, iq_rows, w_rows, *([cache_ik] * group), ik_new)


def _sample_attn_kernel(pt_ref, q_ref, *rest, mode, n_pages, n_steps, dec_seq, group):
    kc_refs, vc_refs = rest[:group], rest[group:2 * group]
    rest = rest[2 * group:]
    if mode == "dsa":
        keys_ref, kn_ref, vn_ref, slope_ref, o_ref = rest[:5]
        m_sc, l_sc, acc_sc, carry_sc, thr_sc, need_sc = rest[5:]
    else:
        lf_refs, (kn_ref, vn_ref, lfn_ref, o_ref) = rest[:group], rest[group:group + 4]
        m_sc, l_sc, acc_sc, carry_sc = rest[group + 4:]
    s_id = pl.program_id(1)
    row = lax.broadcasted_iota(i32, (SROWS, PAGE), 0)
    col = lax.broadcasted_iota(i32, (SROWS, PAGE), 1)
    tok = row // N_HEADS
    qb = (q_ref[0] * QK_SCALE).astype(bf16)

    @pl.when(s_id == 0)
    def _():
        m_sc[...] = jnp.full(m_sc.shape, NEG, f32)
        l_sc[...] = jnp.zeros(l_sc.shape, f32)
        acc_sc[...] = jnp.zeros(acc_sc.shape, f32)
        carry_sc[...] = jnp.zeros(carry_sc.shape, f32)

    def update(ks, vs, biases):
        lg = jnp.concatenate([_dot(qb, k.astype(bf16)) + b for k, b in zip(ks, biases)], axis=1)
        m = m_sc[...]
        m_new = jnp.maximum(m, jnp.max(lg, axis=1, keepdims=True))
        a = jnp.exp(m - m_new)
        p = jnp.exp(lg - m_new)
        l_sc[...] = a * l_sc[...] + jnp.sum(p, axis=1, keepdims=True)
        pr = p.astype(bf16)
        pv = _dot_nt(pr[:, 0:PAGE], vs[0].astype(bf16))
        for g in range(1, len(vs)):
            pv = pv + _dot_nt(pr[:, g * PAGE:(g + 1) * PAGE], vs[g].astype(bf16))
        acc_sc[...] = a * acc_sc[...] + pv
        m_sc[...] = m_new

    if mode == "dsa":
        tri = _upper_tri(PAGE)

        @pl.when(s_id == 0)
        def _():
            kk = keys_ref[0].reshape((n_steps + 1) * group, SQ, PAGE)

            def count(pred):
                return jnp.sum(jnp.sum(jnp.where(pred(kk), 1.0, 0.0), axis=0), axis=1, keepdims=True)

            def bit_body(it, t):
                cand = t ^ lax.shift_left(jnp.int32(1), 31 - it)
                return jnp.where(count(lambda a: a >= cand[None]) >= TOPK, cand, t)

            thr = lax.fori_loop(0, 32, bit_body, jnp.full((SQ, 1), INT_MIN, i32))
            thr = jnp.maximum(thr, INT_MIN + 1)
            thr_sc[...] = thr
            need_sc[...] = TOPK - count(lambda a: a > thr[None])

        def dsa_biases(kps, first_page):
            thr, need = thr_sc[...], need_sc[...]
            before = carry_sc[0:SQ, :]
            out = []
            prefix = [_dot(jnp.where(kp == thr, 1.0, 0.0).astype(bf16), tri) for kp in kps]
            for g, kp in enumerate(kps):
                rank = before + prefix[g]
                sel = jnp.where((kp > thr) | ((kp == thr) & (rank <= need)), 1.0, 0.0)
                before = before + prefix[g][:, PAGE - 1:PAGE]
                sel_rows = jnp.concatenate(
                    [jnp.broadcast_to(sel[t:t + 1, :], (N_HEADS, PAGE)) for t in range(dec_seq)], axis=0)
                dist = (n_pages * PAGE + tok - ((first_page + g) * PAGE + col)).astype(f32)
                out.append(jnp.where(sel_rows > 0.5, -slope_ref[...] * dist, NEG))
            carry_sc[0:SQ, :] = before
            return out

        @pl.when(s_id < n_steps)
        def _():
            kps = [keys_ref[0, s_id, g] for g in range(group)]
            update([r[0, 0] for r in kc_refs], [r[0, 0] for r in vc_refs], dsa_biases(kps, s_id * group))

        @pl.when(s_id == n_steps)
        def _():
            update([kn_ref[0]], [vn_ref[0]], dsa_biases([keys_ref[0, n_steps, 0]], n_pages))
            o_ref[0] = acc_sc[...] / l_sc[...]
    else:
        tri = _upper_tri(PAGE)

        def fox_biases(lfs, mask_new):
            before = carry_sc[...]
            out = []
            prefix = [_dot3(lf_t, tri) for lf_t in lfs]
            for pre in prefix:
                cum = before + pre
                before = before + pre[:, PAGE - 1:PAGE]
                ck = jnp.concatenate([cum] * dec_seq, axis=0)
                out.append(jnp.where(col <= tok, -ck, NEG) if mask_new else -ck)
            carry_sc[...] = before
            return out

        @pl.when(s_id < n_steps)
        def _():
            update([r[0, 0] for r in kc_refs], [r[0, 0] for r in vc_refs],
                   fox_biases([r[0, 0] for r in lf_refs], False))

        @pl.when(s_id == n_steps)
        def _():
            update([kn_ref[0]], [vn_ref[0]], fox_biases([lfn_ref[0]], True))
            o_ref[0] = acc_sc[...] / l_sc[...]


def _sample_attn(mode, pt, q_rows, cache_k, cache_v, k_new, v_new, aux, aux_new, layer, n_pages, dec_seq):
    nb = q_rows.shape[0]
    group = PAGES_PER_STEP
    n_steps = n_pages // group
    kern = functools.partial(_sample_attn_kernel, mode=mode, n_pages=n_pages, n_steps=n_steps,
                             dec_seq=dec_seq, group=group)
    per_b = lambda b, s, pt: (b, 0, 0)
    kv_pages = _page_specs((DKV, PAGE), layer, n_pages, n_steps, group)
    new_page = pl.BlockSpec((1, DKV, PAGE), per_b)
    scratch = [pltpu.VMEM((SROWS, 1), f32), pltpu.VMEM((SROWS, 1), f32), pltpu.VMEM((SROWS, DKV), f32),
               pltpu.VMEM((N_HEADS, 1), f32)]
    if mode == "dsa":
        mid_specs = [pl.BlockSpec((1, n_steps + 1, group, SQ, PAGE), lambda b, s, pt: (b, 0, 0, 0, 0)),
                     new_page, new_page, pl.BlockSpec((SROWS, 1), lambda b, s, pt: (0, 0))]
        mid_args = [aux, k_new, v_new, aux_new]
        scratch += [pltpu.VMEM((SQ, 1), i32), pltpu.VMEM((SQ, 1), f32)]
    else:
        mid_specs = (_page_specs((N_HEADS, PAGE), layer, n_pages, n_steps, group)
                     + [new_page, new_page, pl.BlockSpec((1, N_HEADS, PAGE), per_b)])
        mid_args = [aux] * group + [k_new, v_new, aux_new]
    return pl.pallas_call(
        kern,
        grid_spec=pltpu.PrefetchScalarGridSpec(
            num_scalar_prefetch=1,
            grid=(nb, n_steps + 1),
            in_specs=[pl.BlockSpec((1, SROWS, DKV), per_b)] + kv_pages + kv_pages + mid_specs,
            out_specs=pl.BlockSpec((1, SROWS, DKV), per_b),
            scratch_shapes=scratch,
        ),
        out_shape=jax.ShapeDtypeStruct((nb, SROWS, DKV), f32),
        compiler_params=_cparams(("arbitrary", "arbitrary")),
        name=mode + "_sample_attn",
    )(pt, q_rows, *([cache_k] * group), *([cache_v] * group), *mid_args)


def _outproj_kernel(x_ref, op_ref, os_ref, w_ref, g_ref, b_ref, wr_hi_ref, wr_lo_ref, br_ref,
                    x1_ref, gate_ref, exp_t_ref, rank_t_ref, cnt_ref, run_sc, *, n_prompt_tiles):
    i = pl.program_id(0)
    o = jnp.where(i < n_prompt_tiles, op_ref[...], os_ref[...]).astype(bf16)
    x1 = _layer_norm(ALPHA * x_ref[...] + _dot(o, w_ref[...]), g_ref[...], b_ref[...])
    x1_ref[...] = x1

    hi = x1.astype(bf16)
    lo = (x1 - hi.astype(f32)).astype(bf16)
    lg = _dot(hi, wr_hi_ref[...]) + (_dot(lo, wr_hi_ref[...]) + _dot(hi, wr_lo_ref[...])) + br_ref[...]
    lane = lax.broadcasted_iota(i32, lg.shape, 1)
    out_lane = lax.broadcasted_iota(i32, (ROW_TILE, LANES), 1)
    vals = []
    experts = jnp.zeros((ROW_TILE, LANES), i32)
    for k in range(TOP_E):
        m = jnp.max(lg, axis=1, keepdims=True)
        idx = jnp.min(jnp.where(lg == m, lane, N_EXPERTS), axis=1, keepdims=True)
        vals.append(m)
        experts = jnp.where(out_lane == k, idx, experts)
        lg = jnp.where(lane == idx, -jnp.inf, lg)
    es = [jnp.exp(v - vals[0]) for v in vals]
    tot = es[0] + es[1] + es[2] + es[3]
    gates = jnp.zeros((ROW_TILE, LANES), f32)
    for k in range(TOP_E):
        gates = jnp.where(out_lane == k, es[k] / tot, gates)
    gate_ref[...] = gates

    @pl.when(i == 0)
    def _():
        run_sc[...] = jnp.zeros(run_sc.shape, f32)

    exp_t = experts.T[0:SUBLANES, :]
    e_iota = lax.broadcasted_iota(i32, (N_EXPERTS, ROW_TILE), 0)
    hits = [e_iota == exp_t[k:k + 1, :] for k in range(TOP_E)]
    member = jnp.where(hits[0] | hits[1] | hits[2] | hits[3], 1.0, 0.0)
    pos = run_sc[...] + _dot(member.astype(bf16), _upper_tri(ROW_TILE, strict=True))
    sub = lax.broadcasted_iota(i32, (SUBLANES, ROW_TILE), 0)
    rank_t = jnp.zeros((SUBLANES, ROW_TILE), f32)
    for k in range(TOP_E):
        rank_t = jnp.where(sub == k, jnp.sum(jnp.where(hits[k], pos, 0.0), axis=0, keepdims=True), rank_t)
    run_sc[...] = run_sc[...] + jnp.sum(member, axis=1, keepdims=True)
    exp_t_ref[...] = exp_t
    rank_t_ref[...] = rank_t.astype(i32)
    cnt_ref[...] = jnp.broadcast_to(run_sc[...], cnt_ref.shape).astype(i32)


def _outproj(x, o_prompt, o_sample, w_bf, g, b, wr_hi, wr_lo, br):
    n = x.shape[0]
    npt = o_prompt.shape[0] // ROW_TILE
    nst = o_sample.shape[0] // ROW_TILE
    row = lambda c: pl.BlockSpec((ROW_TILE, c), lambda i: (i, 0))
    col = pl.BlockSpec((SUBLANES, ROW_TILE), lambda i: (0, i))
    full = lambda r, c: pl.BlockSpec((r, c), lambda i: (0, 0))
    kern = functools.partial(_outproj_kernel, n_prompt_tiles=npt)
    return pl.pallas_call(
        kern,
        grid=(n // ROW_TILE,),
        in_specs=[row(D_MODEL),
                  pl.BlockSpec((ROW_TILE, DQ), lambda i: (jnp.minimum(i, npt - 1), 0)),
                  pl.BlockSpec((ROW_TILE, DQ), lambda i: (jnp.clip(i - npt, 0, nst - 1), 0)),
                  full(DQ, D_MODEL), full(1, D_MODEL), full(1, D_MODEL),
                  full(D_MODEL, N_EXPERTS), full(D_MODEL, N_EXPERTS), full(1, N_EXPERTS)],
        out_specs=[row(D_MODEL), row(LANES), col, col, full(N_EXPERTS, LANES)],
        out_shape=[jax.ShapeDtypeStruct((n, D_MODEL), f32),
                   jax.ShapeDtypeStruct((n, LANES), f32),
                   jax.ShapeDtypeStruct((SUBLANES, n), i32), jax.ShapeDtypeStruct((SUBLANES, n), i32),
                   jax.ShapeDtypeStruct((N_EXPERTS, LANES), i32)],
        scratch_shapes=[pltpu.VMEM((N_EXPERTS, 1), f32)],
        compiler_params=_cparams(("arbitrary",)),
        name="outproj_router",
    )(x, o_prompt, o_sample, w_bf, g, b, wr_hi, wr_lo, br)


def _moe_kernel(bexp_ref, used_ref, rows_ref, rows_next_ref, x_hbm, wgu_ref, bgu_ref, wdn_ref, bdn_ref,
                y_ref, wgu_bf, wdn_bf, xbuf, sem):
    i = pl.program_id(0)
    n_used = used_ref[0]
    slot = lax.rem(i, 2)
    e = bexp_ref[i]
    prev = bexp_ref[jnp.maximum(i - 1, 0)]

    def row_copy(tok, slot_, r):
        return pltpu.make_async_copy(x_hbm.at[pl.ds(tok, 1), :], xbuf.at[slot_, pl.ds(r, 1), :], sem.at[slot_])

    def start_block(tok_ref, slot_):
        for r in range(MOE_BLOCK):
            row_copy(tok_ref[0, 0, r], slot_, r).start()

    def wait_block():
        for r in range(MOE_BLOCK):
            row_copy(0, slot, r).wait()

    @pl.when(i == 0)
    def _():
        start_block(rows_ref, 0)

    @pl.when((i == 0) | (e != prev))
    def _():
        wgu_bf[...] = wgu_ref[0, 0].astype(bf16)
        wdn_bf[...] = wdn_ref[0, 0].astype(bf16)

    @pl.when(i < n_used)
    def _():
        wait_block()
        start_block(rows_next_ref, 1 - slot)
        h = _dot(xbuf[slot].astype(bf16), wgu_bf[...]) + bgu_ref[0, 0]
        gate = jnp.minimum(h[:, 0:D_FF], SWIGLU_LIMIT)
        up = jnp.clip(h[:, D_FF:2 * D_FF], -SWIGLU_LIMIT, SWIGLU_LIMIT)
        glu = gate * jax.nn.sigmoid(SWIGLU_ALPHA * gate)
        act = ((up + 1.0) * glu).astype(bf16)
        y_ref[...] = _dot(act, wdn_bf[...]) + bdn_ref[0, 0]

    @pl.when(i == n_used)
    def _():
        wait_block()

    @pl.when(i >= n_used)
    def _():
        y_ref[...] = jnp.zeros(y_ref.shape, f32)


def _moe(block_exp, n_used, row_tok, x, w_gu, b_gu, w_dn, b_dn, layer):
    n_blocks = block_exp.shape[0]
    rows3 = row_tok.reshape(n_blocks, 1, MOE_BLOCK)
    wspec = lambda r, c: pl.BlockSpec((1, 1, r, c), lambda i, be, nu: (layer, be[i], 0, 0))
    rows_spec = lambda ahead: pl.BlockSpec(
        (1, 1, MOE_BLOCK), lambda i, be, nu: (jnp.minimum(i + ahead, n_blocks - 1), 0, 0),
        memory_space=pltpu.SMEM)
    return pl.pallas_call(
        _moe_kernel,
        grid_spec=pltpu.PrefetchScalarGridSpec(
            num_scalar_prefetch=2,
            grid=(n_blocks,),
            in_specs=[rows_spec(0), rows_spec(1), pl.BlockSpec(memory_space=pl.ANY),
                      wspec(D_MODEL, 2 * D_FF), wspec(1, 2 * D_FF),
                      wspec(D_FF, D_MODEL), wspec(1, D_MODEL)],
            out_specs=pl.BlockSpec((MOE_BLOCK, D_MODEL), lambda i, be, nu: (i, 0)),
            scratch_shapes=[pltpu.VMEM((D_MODEL, 2 * D_FF), bf16), pltpu.VMEM((D_FF, D_MODEL), bf16),
                            pltpu.VMEM((2, MOE_BLOCK, D_MODEL), f32), pltpu.SemaphoreType.DMA((2,))],
        ),
        out_shape=jax.ShapeDtypeStruct((n_blocks * MOE_BLOCK, D_MODEL), f32),
        compiler_params=_cparams(("arbitrary",)),
        name="moe_experts",
    )(block_exp, n_used, rows3, rows3, x, w_gu, b_gu, w_dn, b_dn)


def _combine_kernel(dest_ref, dest_next_ref, x_ref, y_hbm, gate_ref, g_ref, b_ref, o_ref, ybuf, sem):
    i = pl.program_id(0)
    slot = lax.rem(i, 2)

    def row_copy(src_row, slot_, j):
        return pltpu.make_async_copy(y_hbm.at[pl.ds(src_row, 1), :], ybuf.at[slot_, pl.ds(j, 1), :], sem.at[slot_])

    def start_tile(ids_ref, slot_):
        for j in range(TOP_E * ROW_TILE):
            row_copy(ids_ref[0, 0, j], slot_, j).start(priority=j % 2)

    @pl.when(i == 0)
    def _():
        start_tile(dest_ref, 0)

    @pl.when(i + 1 < pl.num_programs(0))
    def _():
        start_tile(dest_next_ref, 1 - slot)

    for j in range(TOP_E * ROW_TILE):
        row_copy(0, slot, j).wait()
    gates = gate_ref[...]
    mix = gates[:, 0:1] * ybuf[slot, 0:ROW_TILE, :]
    for k in range(1, TOP_E):
        mix = mix + gates[:, k:k + 1] * ybuf[slot, k * ROW_TILE:(k + 1) * ROW_TILE, :]
    o_ref[...] = _layer_norm(ALPHA * x_ref[...] + mix, g_ref[...], b_ref[...])


def _combine(x1, y_rows, dest, gates, g, b):
    n = x1.shape[0]
    n_tiles = n // ROW_TILE
    ids = dest.reshape(TOP_E, n_tiles, ROW_TILE).transpose(1, 0, 2).reshape(n_tiles, 1, TOP_E * ROW_TILE)
    row = lambda c: pl.BlockSpec((ROW_TILE, c), lambda i: (i, 0))
    full = lambda r, c: pl.BlockSpec((r, c), lambda i: (0, 0))
    ids_spec = lambda ahead: pl.BlockSpec(
        (1, 1, TOP_E * ROW_TILE), lambda i: (jnp.minimum(i + ahead, n_tiles - 1), 0, 0),
        memory_space=pltpu.SMEM)
    return pl.pallas_call(
        _combine_kernel,
        grid=(n_tiles,),
        in_specs=[ids_spec(0), ids_spec(1), row(D_MODEL), pl.BlockSpec(memory_space=pl.ANY),
                  row(LANES), full(1, D_MODEL), full(1, D_MODEL)],
        out_specs=row(D_MODEL),
        out_shape=jax.ShapeDtypeStruct((n, D_MODEL), f32),
        scratch_shapes=[pltpu.VMEM((2, TOP_E * ROW_TILE, D_MODEL), f32), pltpu.SemaphoreType.DMA((2,))],
        compiler_params=_cparams(("arbitrary",)),
        name="combine_norm",
    )(ids, ids, x1, y_rows, gates, g, b)


def _route(exp_t, rank_t, counts, n_tok):
    padded = (counts + MOE_BLOCK - 1) // MOE_BLOCK * MOE_BLOCK
    pad_end = jnp.cumsum(padded)
    pad_start = pad_end - padded
    hit = exp_t[:, :, None] == jnp.arange(N_EXPERTS, dtype=i32)
    dest = (jnp.sum(jnp.where(hit, pad_start, 0), axis=-1) + rank_t).reshape(-1)
    n_blocks = -(-n_tok * TOP_E // MOE_BLOCK) + N_EXPERTS + 1
    tok = jnp.tile(jnp.arange(n_tok, dtype=i32), TOP_E)
    row_tok = jnp.zeros((n_blocks * MOE_BLOCK,), i32).at[dest].set(tok)
    block_start = jnp.arange(n_blocks, dtype=i32) * MOE_BLOCK
    block_exp = jnp.minimum(jnp.sum(pad_end[None, :] <= block_start[:, None], axis=1), N_EXPERTS - 1).astype(i32)
    n_used = (pad_end[-1:] // MOE_BLOCK).astype(i32)
    return row_tok, block_exp, n_used, dest


def _pad_cols(w):
    cols = w.shape[1]
    padded = -(-cols // LANES) * LANES
    return jnp.pad(w, ((0, 0), (0, padded - cols))).astype(bf16)


def _diag_heads(o_full, nb, dec_seq):
    o6 = o_full.reshape(nb, dec_seq, N_KV, GROUP, N_KV, HEAD_DIM)
    own = jnp.eye(N_KV, dtype=jnp.bool_).reshape(1, 1, N_KV, 1, N_KV, 1)
    return jnp.sum(jnp.where(own, o6, 0.0), axis=4).reshape(nb * dec_seq, DQ)


def _block_diag_q(q_s, nb, dec_seq):
    q5 = q_s.reshape(nb, dec_seq, N_KV, GROUP, 1, HEAD_DIM)
    eye = jnp.eye(N_KV, dtype=q_s.dtype).reshape(1, 1, N_KV, 1, N_KV, 1)
    return (q5 * eye).reshape(nb, SROWS, DKV)


def _pad_page(a, nb, dec_seq):
    a3 = a.reshape(nb, dec_seq, a.shape[-1])
    return jnp.pad(a3, ((0, 0), (0, PAGE - dec_seq), (0, 0)))


def kernel(x_prompt, x_sample, cache_dsa_k, cache_dsa_v, cache_dsa_ik, cache_fox_k, cache_fox_v,
           cache_fox_logf, page_table, w_in_dsa, w_out_dsa, w_in_fox, b_forget, w_out_fox,
           ln_gain, ln_bias, w_router, b_router, w_gate_up, b_gate_up, w_down, b_down):
    batch, seq, _ = x_prompt.shape
    nb, dec_seq, _ = x_sample.shape
    n_pages = page_table.shape[1]
    pool = cache_dsa_k.shape[1]
    n_p = batch * seq
    n_s = nb * dec_seq
    n_tok = n_p + n_s
    assert dec_seq * N_HEADS == SROWS and n_s == ROW_TILE and n_p % ROW_TILE == 0 and n_tok % INPROJ_TILE == 0
    assert n_pages % PAGES_PER_STEP == 0 and seq % KEY_CHUNK == 0

    x = jnp.concatenate([x_prompt.reshape(n_p, D_MODEL), x_sample.reshape(n_s, D_MODEL)], axis=0)
    pt = page_table.reshape(-1).astype(i32)
    slopes = 2.0 ** (-8.0 * jnp.arange(1, N_HEADS + 1, dtype=f32) / N_HEADS)
    slope_rows = jnp.tile(slopes, dec_seq).reshape(SROWS, 1)

    kv_t = lambda c: jnp.transpose(c, (0, 1, 3, 4, 2)).reshape(-1, pool, DKV, PAGE)
    dsa_k, dsa_v, fox_k, fox_v = kv_t(cache_dsa_k), kv_t(cache_dsa_v), kv_t(cache_fox_k), kv_t(cache_fox_v)
    dsa_ik_t = jnp.swapaxes(cache_dsa_ik, 2, 3)
    fox_lf_t = jnp.swapaxes(cache_fox_logf, 2, 3)
    new_t = lambda a: jnp.swapaxes(_pad_page(a, nb, dec_seq), 1, 2)
    b_gu4 = b_gate_up.reshape(DEPTH, N_EXPERTS, 1, 2 * D_FF)
    b_dn4 = b_down.reshape(DEPTH, N_EXPERTS, 1, D_MODEL)

    outs = {name: [] for name in ("dk", "dv", "di", "fk", "fv", "fl")}
    for layer in range(DEPTH):
        j = layer // 2
        if layer % 2 == 0:
            q, k, v, iq, ik, iw = _inproj_dsa(x, _pad_cols(w_in_dsa[j]))
            o_p = _dsa_prompt(q, iq, iw, ik, k, v, batch, seq)
            iq_rows = jnp.pad(iq[n_p:].reshape(nb, dec_seq * IDX_HEADS, IDX_DIM),
                              ((0, 0), (0, (SQ - dec_seq) * IDX_HEADS), (0, 0)))
            w_rows = jnp.pad(iw[n_p:].reshape(nb, dec_seq * IDX_HEADS, 1),
                             ((0, 0), (0, (SQ - dec_seq) * IDX_HEADS), (0, 0)))
            keys = _dsa_sample_scores(pt, iq_rows, w_rows, dsa_ik_t, new_t(ik[n_p:]),
                                      j, n_pages, dec_seq)
            o_s = _sample_attn("dsa", pt, _block_diag_q(q[n_p:], nb, dec_seq), dsa_k, dsa_v,
                               new_t(k[n_p:]), new_t(v[n_p:]),
                               keys, slope_rows, j, n_pages, dec_seq)
            w_out = w_out_dsa[j]
            outs["dk"].append(k); outs["dv"].append(v); outs["di"].append(ik)
        else:
            q, k, v, lf = _inproj_fox(x, _pad_cols(w_in_fox[j]), b_forget[j])
            cum = _fox_cum(lf, batch, seq)
            o_p = _fox_prompt(q, cum, k, v, batch, seq)
            o_s = _sample_attn("fox", pt, _block_diag_q(q[n_p:], nb, dec_seq), fox_k, fox_v,
                               new_t(k[n_p:]), new_t(v[n_p:]),
                               fox_lf_t, new_t(lf[n_p:]), j, n_pages, dec_seq)
            w_out = w_out_fox[j]
            outs["fk"].append(k); outs["fv"].append(v); outs["fl"].append(lf)

        wr = w_router[layer]
        wr_hi = wr.astype(bf16)
        wr_lo = (wr - wr_hi.astype(f32)).astype(bf16)
        x1, gates, exp_t, rank_t, cnt = _outproj(
            x, o_p, _diag_heads(o_s, nb, dec_seq), w_out.astype(bf16),
            ln_gain[layer, 0].reshape(1, D_MODEL), ln_bias[layer, 0].reshape(1, D_MODEL),
            wr_hi, wr_lo, b_router[layer].reshape(1, N_EXPERTS))

        row_tok, block_exp, n_used, dest = _route(exp_t[:TOP_E], rank_t[:TOP_E], cnt[:, 0], n_tok)
        y_rows = _moe(block_exp, n_used, row_tok, x1, w_gate_up, b_gu4, w_down, b_dn4, layer)
        x = _combine(x1, y_rows, dest, gates, ln_gain[layer, 1].reshape(1, D_MODEL),
                     ln_bias[layer, 1].reshape(1, D_MODEL))

    def stack(name, shape_p, shape_s):
        a = jnp.stack(outs[name])
        return a[:, :n_p].reshape((-1,) + shape_p), a[:, n_p:].reshape((-1,) + shape_s)

    kv_p, kv_s = (batch, seq, N_KV, HEAD_DIM), (nb, dec_seq, N_KV, HEAD_DIM)
    dk_p, dk_s = stack("dk", kv_p, kv_s)
    dv_p, dv_s = stack("dv", kv_p, kv_s)
    di_p, di_s = stack("di", (batch, seq, IDX_DIM), (nb, dec_seq, IDX_DIM))
    fk_p, fk_s = stack("fk", kv_p, kv_s)
    fv_p, fv_s = stack("fv", kv_p, kv_s)
    fl_p, fl_s = stack("fl", (batch, seq, N_HEADS), (nb, dec_seq, N_HEADS))
    return (x[:n_p].reshape(batch, seq, D_MODEL), x[n_p:].reshape(nb, dec_seq, D_MODEL),
            dk_p, dv_p, di_p, fk_p, fv_p, fl_p, dk_s, dv_s, di_s, fk_s, fv_s, fl_s)
```

```python
import functools

import jax
import jax.numpy as jnp
from jax import lax
from jax.experimental import pallas as pl
from jax.experimental.pallas import tpu as pltpu

f32 = jnp.float32
bf16 = jnp.bfloat16
i32 = jnp.int32
i16 = jnp.int16

D_MODEL = 1024
DEPTH = 4
PAGE = 128
N_HEADS = 16
HEAD_DIM = 64
N_KV = 4
GROUP = N_HEADS // N_KV
DQ = N_HEADS * HEAD_DIM
DKV = N_KV * HEAD_DIM
IDX_HEADS = 8
IDX_DIM = 64
TOPK = 256
N_EXPERTS = 32
TOP_E = 4
D_FF = D_MODEL
SWIGLU_LIMIT = 7.0
SWIGLU_ALPHA = 1.702
MOE_BLOCK = 256
LN_EPS = 1e-5
ALPHA = (2 * DEPTH) ** 0.25
IDX_SCALE = IDX_HEADS ** -0.5 * IDX_DIM ** -0.5
QK_SCALE = HEAD_DIM ** -0.5
LOG2E = 1.4426950408889634

LANES = 128
SUBLANES = 8
Q_TILE = 128
KEY_CHUNK = 256
ROW_TILE = 128
INPROJ_TILE = 384
PAGES_PER_STEP = 16
INT_MIN = -2 ** 31
HALF16 = 2 ** 15
NEG = -1e30
VMEM_LIMIT = 56 * 1024 * 1024

NT = (((1,), (1,)), ((), ()))


def _cparams(sem):
    return pltpu.CompilerParams(dimension_semantics=sem, vmem_limit_bytes=VMEM_LIMIT)


def _split3(x):
    hi = x.astype(bf16)
    r1 = x - hi.astype(f32)
    mid = r1.astype(bf16)
    lo = (r1 - mid.astype(f32)).astype(bf16)
    return hi, mid, lo


def _dot(a, b):
    return jnp.dot(a, b, preferred_element_type=f32)


def _dot_nt(a, b):
    return lax.dot_general(a, b, NT, preferred_element_type=f32)


def _dot3(x, w_bf):
    hi, mid, lo = _split3(x)
    return _dot(hi, w_bf) + _dot(mid, w_bf) + _dot(lo, w_bf)


def _dot3_tn(x, w_bf):
    tn = (((0,), (0,)), ((), ()))
    return sum(lax.dot_general(part, w_bf, tn, preferred_element_type=f32) for part in _split3(x))


def _layer_norm(y, g, b):
    mu = jnp.mean(y, axis=-1, keepdims=True)
    d = y - mu
    var = jnp.mean(d * d, axis=-1, keepdims=True)
    return d * lax.rsqrt(var + LN_EPS) * g + b


def _order_key(s):
    s = jnp.where(s == 0.0, 0.0, s)
    bits = pltpu.bitcast(s, i32)
    return bits ^ ((bits >> 31) & 0x7FFFFFFF)


def _upper_tri(n, strict=False):
    r = lax.broadcasted_iota(i32, (n, n), 0)
    c = lax.broadcasted_iota(i32, (n, n), 1)
    return jnp.where((r < c) if strict else (r <= c), 1.0, 0.0).astype(bf16)


def _inproj_dsa_kernel(x_ref, w_ref, q_ref, k_ref, v_ref, iq_ref, ik_ref, iw_ref):
    xb = x_ref[...].astype(bf16)
    q_ref[...] = _dot(xb, w_ref[:, 0:DQ])
    k_ref[...] = _dot(xb, w_ref[:, DQ:DQ + DKV])
    v_ref[...] = _dot(xb, w_ref[:, DQ + DKV:DQ + 2 * DKV])
    o = DQ + 2 * DKV
    iq_ref[...] = _dot(xb, w_ref[:, o:o + IDX_HEADS * IDX_DIM])
    o += IDX_HEADS * IDX_DIM
    tail = _dot(xb, w_ref[:, o:o + LANES])
    ik_ref[...] = tail[:, 0:IDX_DIM]
    iw_ref[...] = tail[:, IDX_DIM:IDX_DIM + IDX_HEADS]


def _inproj_dsa(x, w_pad):
    n = x.shape[0]
    cols = w_pad.shape[1]
    row = lambda c: pl.BlockSpec((INPROJ_TILE, c), lambda i: (i, 0))
    widths = (DQ, DKV, DKV, IDX_HEADS * IDX_DIM, IDX_DIM, IDX_HEADS)
    return pl.pallas_call(
        _inproj_dsa_kernel,
        grid=(n // INPROJ_TILE,),
        in_specs=[row(D_MODEL), pl.BlockSpec((D_MODEL, cols), lambda i: (0, 0))],
        out_specs=[row(c) for c in widths],
        out_shape=[jax.ShapeDtypeStruct((n, c), f32) for c in widths],
        compiler_params=_cparams(("arbitrary",)),
        name="inproj_dsa",
    )(x, w_pad)


def _inproj_fox_kernel(x_ref, w_ref, bf_ref, q_ref, k_ref, v_ref, lf_ref):
    xb = x_ref[...].astype(bf16)
    q_ref[...] = _dot(xb, w_ref[:, 0:DQ])
    k_ref[...] = _dot(xb, w_ref[:, DQ:DQ + DKV])
    v_ref[...] = _dot(xb, w_ref[:, DQ + DKV:DQ + 2 * DKV])
    o = DQ + 2 * DKV
    fg = _dot(xb, w_ref[:, o:o + LANES])[:, 0:N_HEADS] + bf_ref[...]
    lf_ref[...] = jax.nn.log_sigmoid(fg)


def _inproj_fox(x, w_pad, b_f):
    n = x.shape[0]
    cols = w_pad.shape[1]
    row = lambda c: pl.BlockSpec((INPROJ_TILE, c), lambda i: (i, 0))
    widths = (DQ, DKV, DKV, N_HEADS)
    return pl.pallas_call(
        _inproj_fox_kernel,
        grid=(n // INPROJ_TILE,),
        in_specs=[row(D_MODEL), pl.BlockSpec((D_MODEL, cols), lambda i: (0, 0)),
                  pl.BlockSpec((1, N_HEADS), lambda i: (0, 0))],
        out_specs=[row(c) for c in widths],
        out_shape=[jax.ShapeDtypeStruct((n, c), f32) for c in widths],
        compiler_params=_cparams(("arbitrary",)),
        name="inproj_fox",
    )(x, w_pad, b_f.reshape(1, N_HEADS))


QROWS = GROUP * Q_TILE


def _flash_block(q_ref, kbf, vbf, o_ref, qs, m_sc, acc_sc, n_chunks, bias_fn):
    for h in range(N_HEADS):
        g = h % GROUP
        qs[h // GROUP, g * Q_TILE:(g + 1) * Q_TILE, :] = (
            q_ref[:, h * HEAD_DIM:(h + 1) * HEAD_DIM] * (QK_SCALE * LOG2E)).astype(bf16)
    m_sc[...] = jnp.full(m_sc.shape, NEG, f32)
    acc_sc[...] = jnp.zeros(acc_sc.shape, f32)

    def body(c, _):
        off = pl.multiple_of(c * KEY_CHUNK, KEY_CHUNK)
        for kh in range(N_KV):
            kc = kbf[kh, pl.ds(off, KEY_CHUNK), :]
            vc = vbf[kh, pl.ds(off, KEY_CHUNK), :]
            lg = _dot_nt(qs[kh], kc) + bias_fn(c, kh)
            m = m_sc[kh]
            part = lg[:, 0:LANES]
            for j in range(1, KEY_CHUNK // LANES):
                part = jnp.maximum(part, lg[:, j * LANES:(j + 1) * LANES])
            m_new = jnp.maximum(m, jnp.max(part, axis=1, keepdims=True))
            p = jnp.exp2(lg - jnp.concatenate([m_new] * (KEY_CHUNK // LANES), axis=1))
            acc_sc[kh] = jnp.exp2(m - m_new) * acc_sc[kh] + _dot(p.astype(bf16), vc)
            m_sc[kh] = m_new
        return 0

    lax.fori_loop(0, n_chunks, body, 0)
    for h in range(N_HEADS):
        kh, g = h // GROUP, h % GROUP
        acc = acc_sc[kh, g * Q_TILE:(g + 1) * Q_TILE, :]
        o_ref[:, h * HEAD_DIM:(h + 1) * HEAD_DIM] = acc[:, 0:HEAD_DIM] / acc[:, HEAD_DIM:2 * HEAD_DIM]


_FLASH_SCRATCH = [pltpu.VMEM((N_KV, QROWS, HEAD_DIM), bf16),
                  pltpu.VMEM((N_KV, QROWS, LANES), f32),
                  pltpu.VMEM((N_KV, QROWS, 2 * HEAD_DIM), f32)]


def _stage_kv(k_ref, v_ref, kbf, vbf):
    for h in range(N_KV):
        kbf[h] = k_ref[:, h * HEAD_DIM:(h + 1) * HEAD_DIM].astype(bf16)
        vbf[h, :, 0:HEAD_DIM] = v_ref[:, h * HEAD_DIM:(h + 1) * HEAD_DIM].astype(bf16)
        vbf[h, :, HEAD_DIM:2 * HEAD_DIM] = jnp.ones((v_ref.shape[0], HEAD_DIM), bf16)


def _dsa_prompt_kernel(q_ref, iq_ref, iw_ref, ik_ref, k_ref, v_ref, o_ref,
                       ikbf, kbf, vbf, key3, mb3, dist3, iqs, hi3, lo3, qs, m_sc, acc_sc):
    i = pl.program_id(1)

    @pl.when(i == 0)
    def _():
        ikbf[...] = ik_ref[...].astype(bf16)
        _stage_kv(k_ref, v_ref, kbf, vbf)

    n_chunks = (i * Q_TILE + Q_TILE + KEY_CHUNK - 1) // KEY_CHUNK
    row = lax.broadcasted_iota(i32, (Q_TILE, KEY_CHUNK), 0)
    col = lax.broadcasted_iota(i32, (Q_TILE, KEY_CHUNK), 1)
    qpos = i * Q_TILE + row
    w = iw_ref[...] * IDX_SCALE
    w_rep = [jnp.broadcast_to(w[:, h:h + 1], (Q_TILE, KEY_CHUNK)) for h in range(IDX_HEADS)]
    for h in range(IDX_HEADS):
        iqs[h * Q_TILE:(h + 1) * Q_TILE, :] = iq_ref[:, h * IDX_DIM:(h + 1) * IDX_DIM].astype(bf16)

    def score_body(c, _):
        ikc = ikbf[pl.ds(pl.multiple_of(c * KEY_CHUNK, KEY_CHUNK), KEY_CHUNK), :]
        rel = _dot_nt(iqs[...], ikc)
        s = jnp.zeros((Q_TILE, KEY_CHUNK), f32)
        for h in range(IDX_HEADS):
            s = s + w_rep[h] * jnp.maximum(rel[h * Q_TILE:(h + 1) * Q_TILE, :], 0.0)
        d = qpos - (c * KEY_CHUNK + col)
        key = jnp.where(d >= 0, _order_key(s), INT_MIN)
        key3[c] = key
        hi3[c] = (key >> 16).astype(i16)
        lo3[c] = ((key & 0xFFFF) - HALF16).astype(i16)
        dist3[c] = d.astype(f32)
        return 0

    lax.fori_loop(0, n_chunks, score_body, 0)

    def count(pred):
        def body(c, acc):
            return acc + jnp.where(pred(key3[c]), 1.0, 0.0)
        acc = lax.fori_loop(0, n_chunks, body, jnp.zeros((Q_TILE, KEY_CHUNK), f32))
        return jnp.sum(acc, axis=1, keepdims=True)

    def spread16(v):
        return jnp.broadcast_to(v, (Q_TILE, KEY_CHUNK)).astype(i16)

    def count16(ref3, cand, strict=False):
        cb = spread16(cand)
        one, zero = jnp.ones((Q_TILE, KEY_CHUNK), i16), jnp.zeros((Q_TILE, KEY_CHUNK), i16)

        def body(c, acc):
            v = ref3[c]
            return acc + jnp.where((v > cb) if strict else (v >= cb), one, zero)
        acc = lax.fori_loop(0, n_chunks, body, zero)
        return jnp.sum(acc.astype(f32), axis=1, keepdims=True)

    def search16(ref3, target):
        def bit_body(it, u):
            cand_u = u | lax.shift_left(jnp.int32(1), 15 - it)
            return jnp.where(count16(ref3, cand_u - HALF16) >= target, cand_u, u)
        return lax.fori_loop(0, 16, bit_body, jnp.zeros((Q_TILE, 1), i32)) - HALF16

    hi_thr = search16(hi3, TOPK)
    above = count16(hi3, hi_thr, strict=True)
    hi_b = spread16(hi_thr)

    def keep_equal(c, _):
        lo3[c] = jnp.where(hi3[c] == hi_b, lo3[c], jnp.full((Q_TILE, KEY_CHUNK), -HALF16, i16))
        return 0

    lax.fori_loop(0, n_chunks, keep_equal, 0)
    lo_thr = search16(lo3, TOPK - above)
    thr = lax.shift_left(hi_thr, 16) | (lo_thr + HALF16)
    thr = jnp.maximum(thr, INT_MIN + 1)
    need = TOPK - count(lambda kk: kk > thr)
    n_tie = count(lambda kk: kk == thr)
    has_excess = jnp.max(n_tie - need) > 0.0

    @pl.when(jnp.logical_not(has_excess))
    def _():
        def body(c, _):
            mb3[c] = jnp.where(key3[c] >= thr, 0.0, NEG)
            return 0
        lax.fori_loop(0, n_chunks, body, 0)

    @pl.when(has_excess)
    def _():
        tri = _upper_tri(KEY_CHUNK)

        def body(c, before):
            kk = key3[c]
            tie = jnp.where(kk == thr, 1.0, 0.0)
            rank = before + _dot(tie.astype(bf16), tri)
            sel = (kk > thr) | ((kk == thr) & (rank <= need))
            mb3[c] = jnp.where(sel, 0.0, NEG)
            return before + jnp.sum(tie, axis=1, keepdims=True)
        lax.fori_loop(0, n_chunks, body, jnp.zeros((Q_TILE, 1), f32))

    def bias(c, kh):
        mb = mb3[c]
        d = dist3[c]
        slabs = [mb - float(LOG2E * 2.0 ** (-8.0 * (kh * GROUP + g + 1) / N_HEADS)) * d for g in range(GROUP)]
        return jnp.concatenate(slabs, axis=0)

    _flash_block(q_ref, kbf, vbf, o_ref, qs, m_sc, acc_sc, n_chunks, bias)


def _dsa_prompt(q, iq, iw, ik, k, v, batch, seq):
    nq = seq // Q_TILE
    nc = seq // KEY_CHUNK
    qspec = lambda c: pl.BlockSpec((Q_TILE, c), lambda b, i: (b * nq + i, 0))
    bspec = lambda c: pl.BlockSpec((seq, c), lambda b, i: (b, 0))
    return pl.pallas_call(
        _dsa_prompt_kernel,
        grid=(batch, nq),
        in_specs=[qspec(DQ), qspec(IDX_HEADS * IDX_DIM), qspec(IDX_HEADS),
                  bspec(IDX_DIM), bspec(DKV), bspec(DKV)],
        out_specs=qspec(DQ),
        out_shape=jax.ShapeDtypeStruct((batch * seq, DQ), f32),
        scratch_shapes=[pltpu.VMEM((seq, IDX_DIM), bf16),
                        pltpu.VMEM((N_KV, seq, HEAD_DIM), bf16),
                        pltpu.VMEM((N_KV, seq, 2 * HEAD_DIM), bf16),
                        pltpu.VMEM((nc, Q_TILE, KEY_CHUNK), i32),
                        pltpu.VMEM((nc, Q_TILE, KEY_CHUNK), f32),
                        pltpu.VMEM((nc, Q_TILE, KEY_CHUNK), f32),
                        pltpu.VMEM((IDX_HEADS * Q_TILE, IDX_DIM), bf16),
                        pltpu.VMEM((nc, Q_TILE, KEY_CHUNK), i16),
                        pltpu.VMEM((nc, Q_TILE, KEY_CHUNK), i16)] + _FLASH_SCRATCH,
        compiler_params=_cparams(("arbitrary", "arbitrary")),
        name="dsa_prompt",
    )(q, iq, iw, ik, k, v)


def _cum_kernel(lf_ref, o_ref):
    tri = _upper_tri(KEY_CHUNK)
    carry = jnp.zeros((N_HEADS, 1), f32)
    for c in range(lf_ref.shape[0] // KEY_CHUNK):
        cum = _dot3_tn(lf_ref[c * KEY_CHUNK:(c + 1) * KEY_CHUNK, :], tri) + carry
        o_ref[0, c] = cum
        carry = cum[:, KEY_CHUNK - 1:KEY_CHUNK]


def _fox_cum(lf, batch, seq):
    nc = seq // KEY_CHUNK
    return pl.pallas_call(
        _cum_kernel,
        grid=(batch,),
        in_specs=[pl.BlockSpec((seq, N_HEADS), lambda b: (b, 0))],
        out_specs=pl.BlockSpec((1, nc, N_HEADS, KEY_CHUNK), lambda b: (b, 0, 0, 0)),
        out_shape=jax.ShapeDtypeStruct((batch, nc, N_HEADS, KEY_CHUNK), f32),
        compiler_params=_cparams(("arbitrary",)),
        name="fox_cum",
    )(lf)


def _fox_prompt_kernel(q_ref, cum_ref, k_ref, v_ref, o_ref, kbf, vbf, qs, m_sc, acc_sc):
    i = pl.program_id(1)

    @pl.when(i == 0)
    def _():
        _stage_kv(k_ref, v_ref, kbf, vbf)

    n_chunks = (i * Q_TILE + Q_TILE + KEY_CHUNK - 1) // KEY_CHUNK
    row = lax.broadcasted_iota(i32, (Q_TILE, KEY_CHUNK), 0)
    col = lax.broadcasted_iota(i32, (Q_TILE, KEY_CHUNK), 1)
    rel = i * Q_TILE + row - col

    def bias(c, kh):
        ck = cum_ref[0, c] * (-LOG2E)
        causal = rel - c * KEY_CHUNK >= 0
        slabs = [jnp.where(causal, ck[kh * GROUP + g:kh * GROUP + g + 1, :], NEG) for g in range(GROUP)]
        return jnp.concatenate(slabs, axis=0)

    _flash_block(q_ref, kbf, vbf, o_ref, qs, m_sc, acc_sc, n_chunks, bias)


def _fox_prompt(q, cum, k, v, batch, seq):
    nq = seq // Q_TILE
    nc = seq // KEY_CHUNK
    qspec = lambda c: pl.BlockSpec((Q_TILE, c), lambda b, i: (b * nq + i, 0))
    bspec = lambda c: pl.BlockSpec((seq, c), lambda b, i: (b, 0))
    return pl.pallas_call(
        _fox_prompt_kernel,
        grid=(batch, nq),
        in_specs=[qspec(DQ), pl.BlockSpec((1, nc, N_HEADS, KEY_CHUNK), lambda b, i: (b, 0, 0, 0)),
                  bspec(DKV), bspec(DKV)],
        out_specs=qspec(DQ),
        out_shape=jax.ShapeDtypeStruct((batch * seq, DQ), f32),
        scratch_shapes=[pltpu.VMEM((N_KV, seq, HEAD_DIM), bf16),
                        pltpu.VMEM((N_KV, seq, 2 * HEAD_DIM), bf16)] + _FLASH_SCRATCH,
        compiler_params=_cparams(("arbitrary", "arbitrary")),
        name="fox_prompt",
    )(q, cum, k, v)


SROWS = 64
SQ = SUBLANES


def _page_specs(shape, layer, n_pages, n_steps, group):
    def spec(j):
        def index(b, s, pt):
            return (layer, pt[b * n_pages + jnp.minimum(s, n_steps - 1) * group + j], 0, 0)
        return pl.BlockSpec((1, 1) + shape, index)
    return [spec(j) for j in range(group)]


def _dsa_sample_score_kernel(pt_ref, iq_ref, w_ref, *rest, n_steps, dec_seq, group):
    ikc_refs, ikn_ref, key_ref = rest[:group], rest[group], rest[group + 1]
    s_id = pl.program_id(1)
    iqb = iq_ref[0].astype(bf16)
    wcol = w_ref[0] * IDX_SCALE
    t = lax.broadcasted_iota(i32, (SQ, PAGE), 0)
    j = lax.broadcasted_iota(i32, (SQ, PAGE), 1)

    def score(ik_t):
        rel = _dot(iqb, ik_t.astype(bf16))
        return jnp.sum((jnp.maximum(rel, 0.0) * wcol).reshape(SQ, IDX_HEADS, PAGE), axis=1)

    @pl.when(s_id < n_steps)
    def _():
        for g in range(group):
            key_ref[0, 0, g] = jnp.where(t < dec_seq, _order_key(score(ikc_refs[g][0, 0])), INT_MIN)

    @pl.when(s_id == n_steps)
    def _():
        ok = (t < dec_seq) & (j <= t)
        key_ref[0, 0, 0] = jnp.where(ok, _order_key(score(ikn_ref[0])), INT_MIN)
        for g in range(1, group):
            key_ref[0, 0, g] = jnp.full((SQ, PAGE), INT_MIN, i32)


def _dsa_sample_scores(pt, iq_rows, w_rows, cache_ik, ik_new, layer, n_pages, dec_seq):
    nb = iq_rows.shape[0]
    group = PAGES_PER_STEP
    n_steps = n_pages // group
    kern = functools.partial(_dsa_sample_score_kernel, n_steps=n_steps, dec_seq=dec_seq, group=group)
    per_b = lambda b, s, pt: (b, 0, 0)
    return pl.pallas_call(
        kern,
        grid_spec=pltpu.PrefetchScalarGridSpec(
            num_scalar_prefetch=1,
            grid=(nb, n_steps + 1),
            in_specs=[pl.BlockSpec((1, SQ * IDX_HEADS, IDX_DIM), per_b),
                      pl.BlockSpec((1, SQ * IDX_HEADS, 1), per_b)]
                     + _page_specs((IDX_DIM, PAGE), layer, n_pages, n_steps, group)
                     + [pl.BlockSpec((1, IDX_DIM, PAGE), per_b)],
            out_specs=pl.BlockSpec((1, 1, group, SQ, PAGE), lambda b, s, pt: (b, s, 0, 0, 0)),
        ),
        out_shape=jax.ShapeDtypeStruct((nb, n_steps + 1, group, SQ, PAGE), i32),
        compiler_params=_cparams(("arbitrary", "arbitrary")),
        name="dsa_sample_scores",
    )(pt, iq_rows, w_rows, *([cache_ik] * group), ik_new)


def _sample_attn_kernel(pt_ref, q_ref, *rest, mode, n_pages, n_steps, dec_seq, group):
    kc_refs, vc_refs = rest[:group], rest[group:2 * group]
    rest = rest[2 * group:]
    if mode == "dsa":
        keys_ref, kn_ref, vn_ref, slope_ref, o_ref = rest[:5]
        m_sc, l_sc, acc_sc, carry_sc, thr_sc, need_sc = rest[5:]
    else:
        lf_refs, (kn_ref, vn_ref, lfn_ref, o_ref) = rest[:group], rest[group:group + 4]
        m_sc, l_sc, acc_sc, carry_sc = rest[group + 4:]
    s_id = pl.program_id(1)
    row = lax.broadcasted_iota(i32, (SROWS, PAGE), 0)
    col = lax.broadcasted_iota(i32, (SROWS, PAGE), 1)
    tok = row // N_HEADS
    qb = (q_ref[0] * QK_SCALE).astype(bf16)

    @pl.when(s_id == 0)
    def _():
        m_sc[...] = jnp.full(m_sc.shape, NEG, f32)
        l_sc[...] = jnp.zeros(l_sc.shape, f32)
        acc_sc[...] = jnp.zeros(acc_sc.shape, f32)
        carry_sc[...] = jnp.zeros(carry_sc.shape, f32)

    def update(ks, vs, biases):
        lg = jnp.concatenate([_dot(qb, k.astype(bf16)) + b for k, b in zip(ks, biases)], axis=1)
        m = m_sc[...]
        m_new = jnp.maximum(m, jnp.max(lg, axis=1, keepdims=True))
        a = jnp.exp(m - m_new)
        p = jnp.exp(lg - m_new)
        l_sc[...] = a * l_sc[...] + jnp.sum(p, axis=1, keepdims=True)
        pr = p.astype(bf16)
        pv = _dot_nt(pr[:, 0:PAGE], vs[0].astype(bf16))
        for g in range(1, len(vs)):
            pv = pv + _dot_nt(pr[:, g * PAGE:(g + 1) * PAGE], vs[g].astype(bf16))
        acc_sc[...] = a * acc_sc[...] + pv
        m_sc[...] = m_new

    if mode == "dsa":
        tri = _upper_tri(PAGE)

        @pl.when(s_id == 0)
        def _():
            kk = keys_ref[0].reshape((n_steps + 1) * group, SQ, PAGE)

            def count(pred):
                return jnp.sum(jnp.sum(jnp.where(pred(kk), 1.0, 0.0), axis=0), axis=1, keepdims=True)

            def bit_body(it, t):
                cand = t ^ lax.shift_left(jnp.int32(1), 31 - it)
                return jnp.where(count(lambda a: a >= cand[None]) >= TOPK, cand, t)

            thr = lax.fori_loop(0, 32, bit_body, jnp.full((SQ, 1), INT_MIN, i32))
            thr = jnp.maximum(thr, INT_MIN + 1)
            thr_sc[...] = thr
            need_sc[...] = TOPK - count(lambda a: a > thr[None])

        def dsa_biases(kps, first_page):
            thr, need = thr_sc[...], need_sc[...]
            before = carry_sc[0:SQ, :]
            out = []
            prefix = [_dot(jnp.where(kp == thr, 1.0, 0.0).astype(bf16), tri) for kp in kps]
            for g, kp in enumerate(kps):
                rank = before + prefix[g]
                sel = jnp.where((kp > thr) | ((kp == thr) & (rank <= need)), 1.0, 0.0)
                before = before + prefix[g][:, PAGE - 1:PAGE]
                sel_rows = jnp.concatenate(
                    [jnp.broadcast_to(sel[t:t + 1, :], (N_HEADS, PAGE)) for t in range(dec_seq)], axis=0)
                dist = (n_pages * PAGE + tok - ((first_page + g) * PAGE + col)).astype(f32)
                out.append(jnp.where(sel_rows > 0.5, -slope_ref[...] * dist, NEG))
            carry_sc[0:SQ, :] = before
            return out

        @pl.when(s_id < n_steps)
        def _():
            kps = [keys_ref[0, s_id, g] for g in range(group)]
            update([r[0, 0] for r in kc_refs], [r[0, 0] for r in vc_refs], dsa_biases(kps, s_id * group))

        @pl.when(s_id == n_steps)
        def _():
            update([kn_ref[0]], [vn_ref[0]], dsa_biases([keys_ref[0, n_steps, 0]], n_pages))
            o_ref[0] = acc_sc[...] / l_sc[...]
    else:
        tri = _upper_tri(PAGE)

        def fox_biases(lfs, mask_new):
            before = carry_sc[...]
            out = []
            prefix = [_dot3(lf_t, tri) for lf_t in lfs]
            for pre in prefix:
                cum = before + pre
                before = before + pre[:, PAGE - 1:PAGE]
                ck = jnp.concatenate([cum] * dec_seq, axis=0)
                out.append(jnp.where(col <= tok, -ck, NEG) if mask_new else -ck)
            carry_sc[...] = before
            return out

        @pl.when(s_id < n_steps)
        def _():
            update([r[0, 0] for r in kc_refs], [r[0, 0] for r in vc_refs],
                   fox_biases([r[0, 0] for r in lf_refs], False))

        @pl.when(s_id == n_steps)
        def _():
            update([kn_ref[0]], [vn_ref[0]], fox_biases([lfn_ref[0]], True))
            o_ref[0] = acc_sc[...] / l_sc[...]


def _sample_attn(mode, pt, q_rows, cache_k, cache_v, k_new, v_new, aux, aux_new, layer, n_pages, dec_seq):
    nb = q_rows.shape[0]
    group = PAGES_PER_STEP
    n_steps = n_pages // group
    kern = functools.partial(_sample_attn_kernel, mode=mode, n_pages=n_pages, n_steps=n_steps,
                             dec_seq=dec_seq, group=group)
    per_b = lambda b, s, pt: (b, 0, 0)
    kv_pages = _page_specs((DKV, PAGE), layer, n_pages, n_steps, group)
    new_page = pl.BlockSpec((1, DKV, PAGE), per_b)
    scratch = [pltpu.VMEM((SROWS, 1), f32), pltpu.VMEM((SROWS, 1), f32), pltpu.VMEM((SROWS, DKV), f32),
               pltpu.VMEM((N_HEADS, 1), f32)]
    if mode == "dsa":
        mid_specs = [pl.BlockSpec((1, n_steps + 1, group, SQ, PAGE), lambda b, s, pt: (b, 0, 0, 0, 0)),
                     new_page, new_page, pl.BlockSpec((SROWS, 1), lambda b, s, pt: (0, 0))]
        mid_args = [aux, k_new, v_new, aux_new]
        scratch += [pltpu.VMEM((SQ, 1), i32), pltpu.VMEM((SQ, 1), f32)]
    else:
        mid_specs = (_page_specs((N_HEADS, PAGE), layer, n_pages, n_steps, group)
                     + [new_page, new_page, pl.BlockSpec((1, N_HEADS, PAGE), per_b)])
        mid_args = [aux] * group + [k_new, v_new, aux_new]
    return pl.pallas_call(
        kern,
        grid_spec=pltpu.PrefetchScalarGridSpec(
            num_scalar_prefetch=1,
            grid=(nb, n_steps + 1),
            in_specs=[pl.BlockSpec((1, SROWS, DKV), per_b)] + kv_pages + kv_pages + mid_specs,
            out_specs=pl.BlockSpec((1, SROWS, DKV), per_b),
            scratch_shapes=scratch,
        ),
        out_shape=jax.ShapeDtypeStruct((nb, SROWS, DKV), f32),
        compiler_params=_cparams(("arbitrary", "arbitrary")),
        name=mode + "_sample_attn",
    )(pt, q_rows, *([cache_k] * group), *([cache_v] * group), *mid_args)


def _outproj_kernel(x_ref, op_ref, os_ref, w_ref, g_ref, b_ref, wr_hi_ref, wr_lo_ref, br_ref,
                    x1_ref, gate_ref, exp_t_ref, rank_t_ref, cnt_ref, run_sc, *, n_prompt_tiles):
    i = pl.program_id(0)
    o = jnp.where(i < n_prompt_tiles, op_ref[...], os_ref[...]).astype(bf16)
    x1 = _layer_norm(ALPHA * x_ref[...] + _dot(o, w_ref[...]), g_ref[...], b_ref[...])
    x1_ref[...] = x1

    hi = x1.astype(bf16)
    lo = (x1 - hi.astype(f32)).astype(bf16)
    lg = _dot(hi, wr_hi_ref[...]) + (_dot(lo, wr_hi_ref[...]) + _dot(hi, wr_lo_ref[...])) + br_ref[...]
    lane = lax.broadcasted_iota(i32, lg.shape, 1)
    out_lane = lax.broadcasted_iota(i32, (ROW_TILE, LANES), 1)
    vals = []
    experts = jnp.zeros((ROW_TILE, LANES), i32)
    for k in range(TOP_E):
        m = jnp.max(lg, axis=1, keepdims=True)
        idx = jnp.min(jnp.where(lg == m, lane, N_EXPERTS), axis=1, keepdims=True)
        vals.append(m)
        experts = jnp.where(out_lane == k, idx, experts)
        lg = jnp.where(lane == idx, -jnp.inf, lg)
    es = [jnp.exp(v - vals[0]) for v in vals]
    tot = es[0] + es[1] + es[2] + es[3]
    gates = jnp.zeros((ROW_TILE, LANES), f32)
    for k in range(TOP_E):
        gates = jnp.where(out_lane == k, es[k] / tot, gates)
    gate_ref[...] = gates

    @pl.when(i == 0)
    def _():
        run_sc[...] = jnp.zeros(run_sc.shape, f32)

    exp_t = experts.T[0:SUBLANES, :]
    e_iota = lax.broadcasted_iota(i32, (N_EXPERTS, ROW_TILE), 0)
    hits = [e_iota == exp_t[k:k + 1, :] for k in range(TOP_E)]
    member = jnp.where(hits[0] | hits[1] | hits[2] | hits[3], 1.0, 0.0)
    pos = run_sc[...] + _dot(member.astype(bf16), _upper_tri(ROW_TILE, strict=True))
    sub = lax.broadcasted_iota(i32, (SUBLANES, ROW_TILE), 0)
    rank_t = jnp.zeros((SUBLANES, ROW_TILE), f32)
    for k in range(TOP_E):
        rank_t = jnp.where(sub == k, jnp.sum(jnp.where(hits[k], pos, 0.0), axis=0, keepdims=True), rank_t)
    run_sc[...] = run_sc[...] + jnp.sum(member, axis=1, keepdims=True)
    exp_t_ref[...] = exp_t
    rank_t_ref[...] = rank_t.astype(i32)
    cnt_ref[...] = jnp.broadcast_to(run_sc[...], cnt_ref.shape).astype(i32)


def _outproj(x, o_prompt, o_sample, w_bf, g, b, wr_hi, wr_lo, br):
    n = x.shape[0]
    npt = o_prompt.shape[0] // ROW_TILE
    nst = o_sample.shape[0] // ROW_TILE
    row = lambda c: pl.BlockSpec((ROW_TILE, c), lambda i: (i, 0))
    col = pl.BlockSpec((SUBLANES, ROW_TILE), lambda i: (0, i))
    full = lambda r, c: pl.BlockSpec((r, c), lambda i: (0, 0))
    kern = functools.partial(_outproj_kernel, n_prompt_tiles=npt)
    return pl.pallas_call(
        kern,
        grid=(n // ROW_TILE,),
        in_specs=[row(D_MODEL),
                  pl.BlockSpec((ROW_TILE, DQ), lambda i: (jnp.minimum(i, npt - 1), 0)),
                  pl.BlockSpec((ROW_TILE, DQ), lambda i: (jnp.clip(i - npt, 0, nst - 1), 0)),
                  full(DQ, D_MODEL), full(1, D_MODEL), full(1, D_MODEL),
                  full(D_MODEL, N_EXPERTS), full(D_MODEL, N_EXPERTS), full(1, N_EXPERTS)],
        out_specs=[row(D_MODEL), row(LANES), col, col, full(N_EXPERTS, LANES)],
        out_shape=[jax.ShapeDtypeStruct((n, D_MODEL), f32),
                   jax.ShapeDtypeStruct((n, LANES), f32),
                   jax.ShapeDtypeStruct((SUBLANES, n), i32), jax.ShapeDtypeStruct((SUBLANES, n), i32),
                   jax.ShapeDtypeStruct((N_EXPERTS, LANES), i32)],
        scratch_shapes=[pltpu.VMEM((N_EXPERTS, 1), f32)],
        compiler_params=_cparams(("arbitrary",)),
        name="outproj_router",
    )(x, o_prompt, o_sample, w_bf, g, b, wr_hi, wr_lo, br)


def _moe_kernel(bexp_ref, used_ref, rows_ref, rows_next_ref, x_hbm, wgu_ref, bgu_ref, wdn_ref, bdn_ref,
                y_ref, wgu_bf, wdn_bf, xbuf, sem):
    i = pl.program_id(0)
    n_used = used_ref[0]
    slot = lax.rem(i, 2)
    e = bexp_ref[i]
    prev = bexp_ref[jnp.maximum(i - 1, 0)]

    def row_copy(tok, slot_, r):
        return pltpu.make_async_copy(x_hbm.at[pl.ds(tok, 1), :], xbuf.at[slot_, pl.ds(r, 1), :], sem.at[slot_])

    def start_block(tok_ref, slot_):
        for r in range(MOE_BLOCK):
            row_copy(tok_ref[0, 0, r], slot_, r).start()

    def wait_block():
        for r in range(MOE_BLOCK):
            row_copy(0, slot, r).wait()

    @pl.when(i == 0)
    def _():
        start_block(rows_ref, 0)

    @pl.when((i == 0) | (e != prev))
    def _():
        wgu_bf[...] = wgu_ref[0, 0].astype(bf16)
        wdn_bf[...] = wdn_ref[0, 0].astype(bf16)

    @pl.when(i < n_used)
    def _():
        wait_block()
        start_block(rows_next_ref, 1 - slot)
        h = _dot(xbuf[slot].astype(bf16), wgu_bf[...]) + bgu_ref[0, 0]
        gate = jnp.minimum(h[:, 0:D_FF], SWIGLU_LIMIT)
        up = jnp.clip(h[:, D_FF:2 * D_FF], -SWIGLU_LIMIT, SWIGLU_LIMIT)
        glu = gate * jax.nn.sigmoid(SWIGLU_ALPHA * gate)
        act = ((up + 1.0) * glu).astype(bf16)
        y_ref[...] = _dot(act, wdn_bf[...]) + bdn_ref[0, 0]

    @pl.when(i == n_used)
    def _():
        wait_block()

    @pl.when(i >= n_used)
    def _():
        y_ref[...] = jnp.zeros(y_ref.shape, f32)


def _moe(block_exp, n_used, row_tok, x, w_gu, b_gu, w_dn, b_dn, layer):
    n_blocks = block_exp.shape[0]
    rows3 = row_tok.reshape(n_blocks, 1, MOE_BLOCK)
    wspec = lambda r, c: pl.BlockSpec((1, 1, r, c), lambda i, be, nu: (layer, be[i], 0, 0))
    rows_spec = lambda ahead: pl.BlockSpec(
        (1, 1, MOE_BLOCK), lambda i, be, nu: (jnp.minimum(i + ahead, n_blocks - 1), 0, 0),
        memory_space=pltpu.SMEM)
    return pl.pallas_call(
        _moe_kernel,
        grid_spec=pltpu.PrefetchScalarGridSpec(
            num_scalar_prefetch=2,
            grid=(n_blocks,),
            in_specs=[rows_spec(0), rows_spec(1), pl.BlockSpec(memory_space=pl.ANY),
                      wspec(D_MODEL, 2 * D_FF), wspec(1, 2 * D_FF),
                      wspec(D_FF, D_MODEL), wspec(1, D_MODEL)],
            out_specs=pl.BlockSpec((MOE_BLOCK, D_MODEL), lambda i, be, nu: (i, 0)),
            scratch_shapes=[pltpu.VMEM((D_MODEL, 2 * D_FF), bf16), pltpu.VMEM((D_FF, D_MODEL), bf16),
                            pltpu.VMEM((2, MOE_BLOCK, D_MODEL), f32), pltpu.SemaphoreType.DMA((2,))],
        ),
        out_shape=jax.ShapeDtypeStruct((n_blocks * MOE_BLOCK, D_MODEL), f32),
        compiler_params=_cparams(("arbitrary",)),
        name="moe_experts",
    )(block_exp, n_used, rows3, rows3, x, w_gu, b_gu, w_dn, b_dn)


def _combine_kernel(dest_ref, dest_next_ref, x_ref, y_hbm, gate_ref, g_ref, b_ref, o_ref, ybuf, sem):
    i = pl.program_id(0)
    slot = lax.rem(i, 2)

    def row_copy(src_row, slot_, j):
        return pltpu.make_async_copy(y_hbm.at[pl.ds(src_row, 1), :], ybuf.at[slot_, pl.ds(j, 1), :], sem.at[slot_])

    def start_tile(ids_ref, slot_):
        for j in range(TOP_E * ROW_TILE):
            row_copy(ids_ref[0, 0, j], slot_, j).start(priority=j % 2)

    @pl.when(i == 0)
    def _():
        start_tile(dest_ref, 0)

    @pl.when(i + 1 < pl.num_programs(0))
    def _():
        start_tile(dest_next_ref, 1 - slot)

    for j in range(TOP_E * ROW_TILE):
        row_copy(0, slot, j).wait()
    gates = gate_ref[...]
    mix = gates[:, 0:1] * ybuf[slot, 0:ROW_TILE, :]
    for k in range(1, TOP_E):
        mix = mix + gates[:, k:k + 1] * ybuf[slot, k * ROW_TILE:(k + 1) * ROW_TILE, :]
    o_ref[...] = _layer_norm(ALPHA * x_ref[...] + mix, g_ref[...], b_ref[...])


def _combine(x1, y_rows, dest, gates, g, b):
    n = x1.shape[0]
    n_tiles = n // ROW_TILE
    ids = dest.reshape(TOP_E, n_tiles, ROW_TILE).transpose(1, 0, 2).reshape(n_tiles, 1, TOP_E * ROW_TILE)
    row = lambda c: pl.BlockSpec((ROW_TILE, c), lambda i: (i, 0))
    full = lambda r, c: pl.BlockSpec((r, c), lambda i: (0, 0))
    ids_spec = lambda ahead: pl.BlockSpec(
        (1, 1, TOP_E * ROW_TILE), lambda i: (jnp.minimum(i + ahead, n_tiles - 1), 0, 0),
        memory_space=pltpu.SMEM)
    return pl.pallas_call(
        _combine_kernel,
        grid=(n_tiles,),
        in_specs=[ids_spec(0), ids_spec(1), row(D_MODEL), pl.BlockSpec(memory_space=pl.ANY),
                  row(LANES), full(1, D_MODEL), full(1, D_MODEL)],
        out_specs=row(D_MODEL),
        out_shape=jax.ShapeDtypeStruct((n, D_MODEL), f32),
        scratch_shapes=[pltpu.VMEM((2, TOP_E * ROW_TILE, D_MODEL), f32), pltpu.SemaphoreType.DMA((2,))],
        compiler_params=_cparams(("arbitrary",)),
        name="combine_norm",
    )(ids, ids, x1, y_rows, gates, g, b)


def _route(exp_t, rank_t, counts, n_tok):
    padded = (counts + MOE_BLOCK - 1) // MOE_BLOCK * MOE_BLOCK
    pad_end = jnp.cumsum(padded)
    pad_start = pad_end - padded
    hit = exp_t[:, :, None] == jnp.arange(N_EXPERTS, dtype=i32)
    dest = (jnp.sum(jnp.where(hit, pad_start, 0), axis=-1) + rank_t).reshape(-1)
    n_blocks = -(-n_tok * TOP_E // MOE_BLOCK) + N_EXPERTS + 1
    tok = jnp.tile(jnp.arange(n_tok, dtype=i32), TOP_E)
    row_tok = jnp.zeros((n_blocks * MOE_BLOCK,), i32).at[dest].set(tok)
    block_start = jnp.arange(n_blocks, dtype=i32) * MOE_BLOCK
    block_exp = jnp.minimum(jnp.sum(pad_end[None, :] <= block_start[:, None], axis=1), N_EXPERTS - 1).astype(i32)
    n_used = (pad_end[-1:] // MOE_BLOCK).astype(i32)
    return row_tok, block_exp, n_used, dest


def _pad_cols(w):
    cols = w.shape[1]
    padded = -(-cols // LANES) * LANES
    return jnp.pad(w, ((0, 0), (0, padded - cols))).astype(bf16)


def _diag_heads(o_full, nb, dec_seq):
    o6 = o_full.reshape(nb, dec_seq, N_KV, GROUP, N_KV, HEAD_DIM)
    own = jnp.eye(N_KV, dtype=jnp.bool_).reshape(1, 1, N_KV, 1, N_KV, 1)
    return jnp.sum(jnp.where(own, o6, 0.0), axis=4).reshape(nb * dec_seq, DQ)


def _block_diag_q(q_s, nb, dec_seq):
    q5 = q_s.reshape(nb, dec_seq, N_KV, GROUP, 1, HEAD_DIM)
    eye = jnp.eye(N_KV, dtype=q_s.dtype).reshape(1, 1, N_KV, 1, N_KV, 1)
    return (q5 * eye).reshape(nb, SROWS, DKV)


def _pad_page(a, nb, dec_seq):
    a3 = a.reshape(nb, dec_seq, a.shape[-1])
    return jnp.pad(a3, ((0, 0), (0, PAGE - dec_seq), (0, 0)))


def kernel(x_prompt, x_sample, cache_dsa_k, cache_dsa_v, cache_dsa_ik, cache_fox_k, cache_fox_v,
           cache_fox_logf, page_table, w_in_dsa, w_out_dsa, w_in_fox, b_forget, w_out_fox,
           ln_gain, ln_bias, w_router, b_router, w_gate_up, b_gate_up, w_down, b_down):
    batch, seq, _ = x_prompt.shape
    nb, dec_seq, _ = x_sample.shape
    n_pages = page_table.shape[1]
    pool = cache_dsa_k.shape[1]
    n_p = batch * seq
    n_s = nb * dec_seq
    n_tok = n_p + n_s
    assert dec_seq * N_HEADS == SROWS and n_s == ROW_TILE and n_p % ROW_TILE == 0 and n_tok % INPROJ_TILE == 0
    assert n_pages % PAGES_PER_STEP == 0 and seq % KEY_CHUNK == 0

    x = jnp.concatenate([x_prompt.reshape(n_p, D_MODEL), x_sample.reshape(n_s, D_MODEL)], axis=0)
    pt = page_table.reshape(-1).astype(i32)
    slopes = 2.0 ** (-8.0 * jnp.arange(1, N_HEADS + 1, dtype=f32) / N_HEADS)
    slope_rows = jnp.tile(slopes, dec_seq).reshape(SROWS, 1)

    kv_t = lambda c: jnp.transpose(c, (0, 1, 3, 4, 2)).reshape(-1, pool, DKV, PAGE)
    dsa_k, dsa_v, fox_k, fox_v = kv_t(cache_dsa_k), kv_t(cache_dsa_v), kv_t(cache_fox_k), kv_t(cache_fox_v)
    dsa_ik_t = jnp.swapaxes(cache_dsa_ik, 2, 3)
    fox_lf_t = jnp.swapaxes(cache_fox_logf, 2, 3)
    new_t = lambda a: jnp.swapaxes(_pad_page(a, nb, dec_seq), 1, 2)
    b_gu4 = b_gate_up.reshape(DEPTH, N_EXPERTS, 1, 2 * D_FF)
    b_dn4 = b_down.reshape(DEPTH, N_EXPERTS, 1, D_MODEL)

    outs = {name: [] for name in ("dk", "dv", "di", "fk", "fv", "fl")}
    for layer in range(DEPTH):
        j = layer // 2
        if layer % 2 == 0:
            q, k, v, iq, ik, iw = _inproj_dsa(x, _pad_cols(w_in_dsa[j]))
            o_p = _dsa_prompt(q, iq, iw, ik, k, v, batch, seq)
            iq_rows = jnp.pad(iq[n_p:].reshape(nb, dec_seq * IDX_HEADS, IDX_DIM),
                              ((0, 0), (0, (SQ - dec_seq) * IDX_HEADS), (0, 0)))
            w_rows = jnp.pad(iw[n_p:].reshape(nb, dec_seq * IDX_HEADS, 1),
                             ((0, 0), (0, (SQ - dec_seq) * IDX_HEADS), (0, 0)))
            keys = _dsa_sample_scores(pt, iq_rows, w_rows, dsa_ik_t, new_t(ik[n_p:]),
                                      j, n_pages, dec_seq)
            o_s = _sample_attn("dsa", pt, _block_diag_q(q[n_p:], nb, dec_seq), dsa_k, dsa_v,
                               new_t(k[n_p:]), new_t(v[n_p:]),
                               keys, slope_rows, j, n_pages, dec_seq)
            w_out = w_out_dsa[j]
            outs["dk"].append(k); outs["dv"].append(v); outs["di"].append(ik)
        else:
            q, k, v, lf = _inproj_fox(x, _pad_cols(w_in_fox[j]), b_forget[j])
            cum = _fox_cum(lf, batch, seq)
            o_p = _fox_prompt(q, cum, k, v, batch, seq)
            o_s = _sample_attn("fox", pt, _block_diag_q(q[n_p:], nb, dec_seq), fox_k, fox_v,
                               new_t(k[n_p:]), new_t(v[n_p:]),
                               fox_lf_t, new_t(lf[n_p:]), j, n_pages, dec_seq)
            w_out = w_out_fox[j]
            outs["fk"].append(k); outs["fv"].append(v); outs["fl"].append(lf)

        wr = w_router[layer]
        wr_hi = wr.astype(bf16)
        wr_lo = (wr - wr_hi.astype(f32)).astype(bf16)
        x1, gates, exp_t, rank_t, cnt = _outproj(
            x, o_p, _diag_heads(o_s, nb, dec_seq), w_out.astype(bf16),
            ln_gain[layer, 0].reshape(1, D_MODEL), ln_bias[layer, 0].reshape(1, D_MODEL),
            wr_hi, wr_lo, b_router[layer].reshape(1, N_EXPERTS))

        row_tok, block_exp, n_used, dest = _route(exp_t[:TOP_E], rank_t[:TOP_E], cnt[:, 0], n_tok)
        y_rows = _moe(block_exp, n_used, row_tok, x1, w_gate_up, b_gu4, w_down, b_dn4, layer)
        x = _combine(x1, y_rows, dest, gates, ln_gain[layer, 1].reshape(1, D_MODEL),
                     ln_bias[layer, 1].reshape(1, D_MODEL))

    def stack(name, shape_p, shape_s):
        a = jnp.stack(outs[name])
        return a[:, :n_p].reshape((-1,) + shape_p), a[:, n_p:].reshape((-1,) + shape_s)

    kv_p, kv_s = (batch, seq, N_KV, HEAD_DIM), (nb, dec_seq, N_KV, HEAD_DIM)
    dk_p, dk_s = stack("dk", kv_p, kv_s)
    dv_p, dv_s = stack("dv", kv_p, kv_s)
    di_p, di_s = stack("di", (batch, seq, IDX_DIM), (nb, dec_seq, IDX_DIM))
    fk_p, fk_s = stack("fk", kv_p, kv_s)
    fv_p, fv_s = stack("fv", kv_p, kv_s)
    fl_p, fl_s = stack("fl", (batch, seq, N_HEADS), (nb, dec_seq, N_HEADS))
    return (x[:n_p].reshape(batch, seq, D_MODEL), x[n_p:].reshape(nb, dec_seq, D_MODEL),
            dk_p, dv_p, di_p, fk_p, fv_p, fl_p, dk_s, dv_s, di_s, fk_s, fv_s, fl_s)
```

```python
import functools

import jax
import jax.numpy as jnp
from jax import lax
from jax.experimental import pallas as pl
from jax.experimental.pallas import tpu as pltpu

f32 = jnp.float32
bf16 = jnp.bfloat16
i32 = jnp.int32
i16 = jnp.int16

D_MODEL = 1024
DEPTH = 4
PAGE = 128
N_HEADS = 16
HEAD_DIM = 64
N_KV = 4
GROUP = N_HEADS // N_KV
DQ = N_HEADS * HEAD_DIM
DKV = N_KV * HEAD_DIM
IDX_HEADS = 8
IDX_DIM = 64
TOPK = 256
N_EXPERTS = 32
TOP_E = 4
D_FF = D_MODEL
SWIGLU_LIMIT = 7.0
SWIGLU_ALPHA = 1.702
MOE_BLOCK = 256
LN_EPS = 1e-5
ALPHA = (2 * DEPTH) ** 0.25
IDX_SCALE = IDX_HEADS ** -0.5 * IDX_DIM ** -0.5
QK_SCALE = HEAD_DIM ** -0.5
LOG2E = 1.4426950408889634

LANES = 128
SUBLANES = 8
Q_TILE = 128
KEY_CHUNK = 256
ROW_TILE = 128
INPROJ_TILE = 384
PAGES_PER_STEP = 16
INT_MIN = -2 ** 31
HALF16 = 2 ** 15
NEG = -1e30
VMEM_LIMIT = 56 * 1024 * 1024

NT = (((1,), (1,)), ((), ()))


def _cparams(sem):
    return pltpu.CompilerParams(dimension_semantics=sem, vmem_limit_bytes=VMEM_LIMIT)


def _split3(x):
    hi = x.astype(bf16)
    r1 = x - hi.astype(f32)
    mid = r1.astype(bf16)
    lo = (r1 - mid.astype(f32)).astype(bf16)
    return hi, mid, lo


def _dot(a, b):
    return jnp.dot(a, b, preferred_element_type=f32)


def _dot_nt(a, b):
    return lax.dot_general(a, b, NT, preferred_element_type=f32)


def _dot3(x, w_bf):
    hi, mid, lo = _split3(x)
    return _dot(hi, w_bf) + _dot(mid, w_bf) + _dot(lo, w_bf)


def _dot3_tn(x, w_bf):
    tn = (((0,), (0,)), ((), ()))
    return sum(lax.dot_general(part, w_bf, tn, preferred_element_type=f32) for part in _split3(x))


def _layer_norm(y, g, b):
    mu = jnp.mean(y, axis=-1, keepdims=True)
    d = y - mu
    var = jnp.mean(d * d, axis=-1, keepdims=True)
    return d * lax.rsqrt(var + LN_EPS) * g + b


def _order_key(s):
    s = jnp.where(s == 0.0, 0.0, s)
    bits = pltpu.bitcast(s, i32)
    return bits ^ ((bits >> 31) & 0x7FFFFFFF)


def _upper_tri(n, strict=False):
    r = lax.broadcasted_iota(i32, (n, n), 0)
    c = lax.broadcasted_iota(i32, (n, n), 1)
    return jnp.where((r < c) if strict else (r <= c), 1.0, 0.0).astype(bf16)


def _inproj_dsa_kernel(x_ref, w_ref, q_ref, k_ref, v_ref, iq_ref, ik_ref, iw_ref):
    xb = x_ref[...].astype(bf16)
    q_ref[...] = _dot(xb, w_ref[:, 0:DQ])
    k_ref[...] = _dot(xb, w_ref[:, DQ:DQ + DKV])
    v_ref[...] = _dot(xb, w_ref[:, DQ + DKV:DQ + 2 * DKV])
    o = DQ + 2 * DKV
    iq_ref[...] = _dot(xb, w_ref[:, o:o + IDX_HEADS * IDX_DIM])
    o += IDX_HEADS * IDX_DIM
    tail = _dot(xb, w_ref[:, o:o + LANES])
    ik_ref[...] = tail[:, 0:IDX_DIM]
    iw_ref[...] = tail[:, IDX_DIM:IDX_DIM + IDX_HEADS]


def _inproj_dsa(x, w_pad):
    n = x.shape[0]
    cols = w_pad.shape[1]
    row = lambda c: pl.BlockSpec((INPROJ_TILE, c), lambda i: (i, 0))
    widths = (DQ, DKV, DKV, IDX_HEADS * IDX_DIM, IDX_DIM, IDX_HEADS)
    return pl.pallas_call(
        _inproj_dsa_kernel,
        grid=(n // INPROJ_TILE,),
        in_specs=[row(D_MODEL), pl.BlockSpec((D_MODEL, cols), lambda i: (0, 0))],
        out_specs=[row(c) for c in widths],
        out_shape=[jax.ShapeDtypeStruct((n, c), f32) for c in widths],
        compiler_params=_cparams(("arbitrary",)),
        name="inproj_dsa",
    )(x, w_pad)


def _inproj_fox_kernel(x_ref, w_ref, bf_ref, q_ref, k_ref, v_ref, lf_ref):
    xb = x_ref[...].astype(bf16)
    q_ref[...] = _dot(xb, w_ref[:, 0:DQ])
    k_ref[...] = _dot(xb, w_ref[:, DQ:DQ + DKV])
    v_ref[...] = _dot(xb, w_ref[:, DQ + DKV:DQ + 2 * DKV])
    o = DQ + 2 * DKV
    fg = _dot(xb, w_ref[:, o:o + LANES])[:, 0:N_HEADS] + bf_ref[...]
    lf_ref[...] = jax.nn.log_sigmoid(fg)


def _inproj_fox(x, w_pad, b_f):
    n = x.shape[0]
    cols = w_pad.shape[1]
    row = lambda c: pl.BlockSpec((INPROJ_TILE, c), lambda i: (i, 0))
    widths = (DQ, DKV, DKV, N_HEADS)
    return pl.pallas_call(
        _inproj_fox_kernel,
        grid=(n // INPROJ_TILE,),
        in_specs=[row(D_MODEL), pl.BlockSpec((D_MODEL, cols), lambda i: (0, 0)),
                  pl.BlockSpec((1, N_HEADS), lambda i: (0, 0))],
        out_specs=[row(c) for c in widths],
        out_shape=[jax.ShapeDtypeStruct((n, c), f32) for c in widths],
        compiler_params=_cparams(("arbitrary",)),
        name="inproj_fox",
    )(x, w_pad, b_f.reshape(1, N_HEADS))


QROWS = GROUP * Q_TILE


def _flash_block(q_ref, kbf, vbf, o_ref, qs, m_sc, acc_sc, n_chunks, bias_fn):
    for h in range(N_HEADS):
        g = h % GROUP
        qs[h // GROUP, g * Q_TILE:(g + 1) * Q_TILE, :] = (
            q_ref[:, h * HEAD_DIM:(h + 1) * HEAD_DIM] * (QK_SCALE * LOG2E)).astype(bf16)
    m_sc[...] = jnp.full(m_sc.shape, NEG, f32)
    acc_sc[...] = jnp.zeros(acc_sc.shape, f32)

    def body(c, _):
        off = pl.multiple_of(c * KEY_CHUNK, KEY_CHUNK)
        for kh in range(N_KV):
            kc = kbf[kh, pl.ds(off, KEY_CHUNK), :]
            vc = vbf[kh, pl.ds(off, KEY_CHUNK), :]
            lg = _dot_nt(qs[kh], kc) + bias_fn(c, kh)
            m = m_sc[kh]
            part = lg[:, 0:LANES]
            for j in range(1, KEY_CHUNK // LANES):
                part = jnp.maximum(part, lg[:, j * LANES:(j + 1) * LANES])
            m_new = jnp.maximum(m, jnp.max(part, axis=1, keepdims=True))
            p = jnp.exp2(lg - jnp.concatenate([m_new] * (KEY_CHUNK // LANES), axis=1))
            acc_sc[kh] = jnp.exp2(m - m_new) * acc_sc[kh] + _dot(p.astype(bf16), vc)
            m_sc[kh] = m_new
        return 0

    lax.fori_loop(0, n_chunks, body, 0)
    for h in range(N_HEADS):
        kh, g = h // GROUP, h % GROUP
        acc = acc_sc[kh, g * Q_TILE:(g + 1) * Q_TILE, :]
        o_ref[:, h * HEAD_DIM:(h + 1) * HEAD_DIM] = acc[:, 0:HEAD_DIM] / acc[:, HEAD_DIM:2 * HEAD_DIM]


_FLASH_SCRATCH = [pltpu.VMEM((N_KV, QROWS, HEAD_DIM), bf16),
                  pltpu.VMEM((N_KV, QROWS, LANES), f32),
                  pltpu.VMEM((N_KV, QROWS, 2 * HEAD_DIM), f32)]


def _stage_kv(k_ref, v_ref, kbf, vbf):
    for h in range(N_KV):
        kbf[h] = k_ref[:, h * HEAD_DIM:(h + 1) * HEAD_DIM].astype(bf16)
        vbf[h, :, 0:HEAD_DIM] = v_ref[:, h * HEAD_DIM:(h + 1) * HEAD_DIM].astype(bf16)
        vbf[h, :, HEAD_DIM:2 * HEAD_DIM] = jnp.ones((v_ref.shape[0], HEAD_DIM), bf16)


def _dsa_prompt_kernel(q_ref, iq_ref, iw_ref, ik_ref, k_ref, v_ref, o_ref,
                       ikbf, kbf, vbf, key3, mb3, dist3, iqs, hi3, lo3, qs, m_sc, acc_sc):
    i = pl.program_id(1)

    @pl.when(i == 0)
    def _():
        ikbf[...] = ik_ref[...].astype(bf16)
        _stage_kv(k_ref, v_ref, kbf, vbf)

    n_chunks = (i * Q_TILE + Q_TILE + KEY_CHUNK - 1) // KEY_CHUNK
    row = lax.broadcasted_iota(i32, (Q_TILE, KEY_CHUNK), 0)
    col = lax.broadcasted_iota(i32, (Q_TILE, KEY_CHUNK), 1)
    qpos = i * Q_TILE + row
    w = iw_ref[...] * IDX_SCALE
    w_rep = [jnp.broadcast_to(w[:, h:h + 1], (Q_TILE, KEY_CHUNK)) for h in range(IDX_HEADS)]
    for h in range(IDX_HEADS):
        iqs[h * Q_TILE:(h + 1) * Q_TILE, :] = iq_ref[:, h * IDX_DIM:(h + 1) * IDX_DIM].astype(bf16)

    def score_body(c, _):
        ikc = ikbf[pl.ds(pl.multiple_of(c * KEY_CHUNK, KEY_CHUNK), KEY_CHUNK), :]
        rel = _dot_nt(iqs[...], ikc)
        s = jnp.zeros((Q_TILE, KEY_CHUNK), f32)
        for h in range(IDX_HEADS):
            s = s + w_rep[h] * jnp.maximum(rel[h * Q_TILE:(h + 1) * Q_TILE, :], 0.0)
        d = qpos - (c * KEY_CHUNK + col)
        key = jnp.where(d >= 0, _order_key(s), INT_MIN)
        key3[c] = key
        hi3[c] = (key >> 16).astype(i16)
        lo3[c] = ((key & 0xFFFF) - HALF16).astype(i16)
        dist3[c] = d.astype(f32)
        return 0

    lax.fori_loop(0, n_chunks, score_body, 0)

    def count(pred):
        def body(c, acc):
            return acc + jnp.where(pred(key3[c]), 1.0, 0.0)
        acc = lax.fori_loop(0, n_chunks, body, jnp.zeros((Q_TILE, KEY_CHUNK), f32))
        return jnp.sum(acc, axis=1, keepdims=True)

    def spread16(v):
        return jnp.broadcast_to(v, (Q_TILE, KEY_CHUNK)).astype(i16)

    def count16(ref3, cand, strict=False):
        cb = spread16(cand)
        one, zero = jnp.ones((Q_TILE, KEY_CHUNK), i16), jnp.zeros((Q_TILE, KEY_CHUNK), i16)

        def body(c, acc):
            v = ref3[c]
            return acc + jnp.where((v > cb) if strict else (v >= cb), one, zero)
        acc = lax.fori_loop(0, n_chunks, body, zero)
        return jnp.sum(acc.astype(f32), axis=1, keepdims=True)

    def search16(ref3, target):
        def bit_body(it, u):
            cand_u = u | lax.shift_left(jnp.int32(1), 15 - it)
            return jnp.where(count16(ref3, cand_u - HALF16) >= target, cand_u, u)
        return lax.fori_loop(0, 16, bit_body, jnp.zeros((Q_TILE, 1), i32)) - HALF16

    hi_thr = search16(hi3, TOPK)
    above = count16(hi3, hi_thr, strict=True)
    hi_b = spread16(hi_thr)

    def keep_equal(c, _):
        lo3[c] = jnp.where(hi3[c] == hi_b, lo3[c], jnp.full((Q_TILE, KEY_CHUNK), -HALF16, i16))
        return 0

    lax.fori_loop(0, n_chunks, keep_equal, 0)
    lo_thr = search16(lo3, TOPK - above)
    thr = lax.shift_left(hi_thr, 16) | (lo_thr + HALF16)
    thr = jnp.maximum(thr, INT_MIN + 1)
    need = TOPK - count(lambda kk: kk > thr)
    n_tie = count(lambda kk: kk == thr)
    has_excess = jnp.max(n_tie - need) > 0.0

    @pl.when(jnp.logical_not(has_excess))
    def _():
        def body(c, _):
            mb3[c] = jnp.where(key3[c] >= thr, 0.0, NEG)
            return 0
        lax.fori_loop(0, n_chunks, body, 0)

    @pl.when(has_excess)
    def _():
        tri = _upper_tri(KEY_CHUNK)

        def body(c, before):
            kk = key3[c]
            tie = jnp.where(kk == thr, 1.0, 0.0)
            rank = before + _dot(tie.astype(bf16), tri)
            sel = (kk > thr) | ((kk == thr) & (rank <= need))
            mb3[c] = jnp.where(sel, 0.0, NEG)
            return before + jnp.sum(tie, axis=1, keepdims=True)
        lax.fori_loop(0, n_chunks, body, jnp.zeros((Q_TILE, 1), f32))

    def bias(c, kh):
        mb = mb3[c]
        d = dist3[c]
        slabs = [mb - float(LOG2E * 2.0 ** (-8.0 * (kh * GROUP + g + 1) / N_HEADS)) * d for g in range(GROUP)]
        return jnp.concatenate(slabs, axis=0)

    _flash_block(q_ref, kbf, vbf, o_ref, qs, m_sc, acc_sc, n_chunks, bias)


def _dsa_prompt(q, iq, iw, ik, k, v, batch, seq):
    nq = seq // Q_TILE
    nc = seq // KEY_CHUNK
    qspec = lambda c: pl.BlockSpec((Q_TILE, c), lambda b, i: (b * nq + i, 0))
    bspec = lambda c: pl.BlockSpec((seq, c), lambda b, i: (b, 0))
    return pl.pallas_call(
        _dsa_prompt_kernel,
        grid=(batch, nq),
        in_specs=[qspec(DQ), qspec(IDX_HEADS * IDX_DIM), qspec(IDX_HEADS),
                  bspec(IDX_DIM), bspec(DKV), bspec(DKV)],
        out_specs=qspec(DQ),
        out_shape=jax.ShapeDtypeStruct((batch * seq, DQ), f32),
        scratch_shapes=[pltpu.VMEM((seq, IDX_DIM), bf16),
                        pltpu.VMEM((N_KV, seq, HEAD_DIM), bf16),
                        pltpu.VMEM((N_KV, seq, 2 * HEAD_DIM), bf16),
                        pltpu.VMEM((nc, Q_TILE, KEY_CHUNK), i32),
                        pltpu.VMEM((nc, Q_TILE, KEY_CHUNK), f32),
                        pltpu.VMEM((nc, Q_TILE, KEY_CHUNK), f32),
                        pltpu.VMEM((IDX_HEADS * Q_TILE, IDX_DIM), bf16),
                        pltpu.VMEM((nc, Q_TILE, KEY_CHUNK), i16),
                        pltpu.VMEM((nc, Q_TILE, KEY_CHUNK), i16)] + _FLASH_SCRATCH,
        compiler_params=_cparams(("arbitrary", "arbitrary")),
        name="dsa_prompt",
    )(q, iq, iw, ik, k, v)


def _cum_kernel(lf_ref, o_ref):
    tri = _upper_tri(KEY_CHUNK)
    carry = jnp.zeros((N_HEADS, 1), f32)
    for c in range(lf_ref.shape[0] // KEY_CHUNK):
        cum = _dot3_tn(lf_ref[c * KEY_CHUNK:(c + 1) * KEY_CHUNK, :], tri) + carry
        o_ref[0, c] = cum
        carry = cum[:, KEY_CHUNK - 1:KEY_CHUNK]


def _fox_cum(lf, batch, seq):
    nc = seq // KEY_CHUNK
    return pl.pallas_call(
        _cum_kernel,
        grid=(batch,),
        in_specs=[pl.BlockSpec((seq, N_HEADS), lambda b: (b, 0))],
        out_specs=pl.BlockSpec((1, nc, N_HEADS, KEY_CHUNK), lambda b: (b, 0, 0, 0)),
        out_shape=jax.ShapeDtypeStruct((batch, nc, N_HEADS, KEY_CHUNK), f32),
        compiler_params=_cparams(("arbitrary",)),
        name="fox_cum",
    )(lf)


def _fox_prompt_kernel(q_ref, cum_ref, k_ref, v_ref, o_ref, kbf, vbf, qs, m_sc, acc_sc):
    i = pl.program_id(1)

    @pl.when(i == 0)
    def _():
        _stage_kv(k_ref, v_ref, kbf, vbf)

    n_chunks = (i * Q_TILE + Q_TILE + KEY_CHUNK - 1) // KEY_CHUNK
    row = lax.broadcasted_iota(i32, (Q_TILE, KEY_CHUNK), 0)
    col = lax.broadcasted_iota(i32, (Q_TILE, KEY_CHUNK), 1)
    rel = i * Q_TILE + row - col

    def bias(c, kh):
        ck = cum_ref[0, c] * (-LOG2E)
        causal = rel - c * KEY_CHUNK >= 0
        slabs = [jnp.where(causal, ck[kh * GROUP + g:kh * GROUP + g + 1, :], NEG) for g in range(GROUP)]
        return jnp.concatenate(slabs, axis=0)

    _flash_block(q_ref, kbf, vbf, o_ref, qs, m_sc, acc_sc, n_chunks, bias)


def _fox_prompt(q, cum, k, v, batch, seq):
    nq = seq // Q_TILE
    nc = seq // KEY_CHUNK
    qspec = lambda c: pl.BlockSpec((Q_TILE, c), lambda b, i: (b * nq + i, 0))
    bspec = lambda c: pl.BlockSpec((seq, c), lambda b, i: (b, 0))
    return pl.pallas_call(
        _fox_prompt_kernel,
        grid=(batch, nq),
        in_specs=[qspec(DQ), pl.BlockSpec((1, nc, N_HEADS, KEY_CHUNK), lambda b, i: (b, 0, 0, 0)),
                  bspec(DKV), bspec(DKV)],
        out_specs=qspec(DQ),
        out_shape=jax.ShapeDtypeStruct((batch * seq, DQ), f32),
        scratch_shapes=[pltpu.VMEM((N_KV, seq, HEAD_DIM), bf16),
                        pltpu.VMEM((N_KV, seq, 2 * HEAD_DIM), bf16)] + _FLASH_SCRATCH,
        compiler_params=_cparams(("arbitrary", "arbitrary")),
        name="fox_prompt",
    )(q, cum, k, v)


SROWS = 64
SQ = SUBLANES


def _page_specs(shape, layer, n_pages, n_steps, group):
    def spec(j):
        def index(b, s, pt):
            return (layer, pt[b * n_pages + jnp.minimum(s, n_steps - 1) * group + j], 0, 0)
        return pl.BlockSpec((1, 1) + shape, index)
    return [spec(j) for j in range(group)]


def _dsa_sample_score_kernel(pt_ref, iq_ref, w_ref, *rest, n_steps, dec_seq, group):
    ikc_refs, ikn_ref, key_ref = rest[:group], rest[group], rest[group + 1]
    s_id = pl.program_id(1)
    iqb = iq_ref[0].astype(bf16)
    wcol = w_ref[0] * IDX_SCALE
    t = lax.broadcasted_iota(i32, (SQ, PAGE), 0)
    j = lax.broadcasted_iota(i32, (SQ, PAGE), 1)

    def score(ik_t):
        rel = _dot(iqb, ik_t.astype(bf16))
        return jnp.sum((jnp.maximum(rel, 0.0) * wcol).reshape(SQ, IDX_HEADS, PAGE), axis=1)

    @pl.when(s_id < n_steps)
    def _():
        for g in range(group):
            key_ref[0, 0, g] = jnp.where(t < dec_seq, _order_key(score(ikc_refs[g][0, 0])), INT_MIN)

    @pl.when(s_id == n_steps)
    def _():
        ok = (t < dec_seq) & (j <= t)
        key_ref[0, 0, 0] = jnp.where(ok, _order_key(score(ikn_ref[0])), INT_MIN)
        for g in range(1, group):
            key_ref[0, 0, g] = jnp.full((SQ, PAGE), INT_MIN, i32)


def _dsa_sample_scores(pt, iq_rows, w_rows, cache_ik, ik_new, layer, n_pages, dec_seq):
    nb = iq_rows.shape[0]
    group = PAGES_PER_STEP
    n_steps = n_pages // group
    kern = functools.partial(_dsa_sample_score_kernel, n_steps=n_steps, dec_seq=dec_seq, group=group)
    per_b = lambda b, s, pt: (b, 0, 0)
    return pl.pallas_call(
        kern,
        grid_spec=pltpu.PrefetchScalarGridSpec(
            num_scalar_prefetch=1,
            grid=(nb, n_steps + 1),
            in_specs=[pl.BlockSpec((1, SQ * IDX_HEADS, IDX_DIM), per_b),
                      pl.BlockSpec((1, SQ * IDX_HEADS, 1), per_b)]
                     + _page_specs((IDX_DIM, PAGE), layer, n_pages, n_steps, group)
                     + [pl.BlockSpec((1, IDX_DIM, PAGE), per_b)],
            out_specs=pl.BlockSpec((1, 1, group, SQ, PAGE), lambda b, s, pt: (b, s, 0, 0, 0)),
        ),
        out_shape=jax.ShapeDtypeStruct((nb, n_steps + 1, group, SQ, PAGE), i32),
        compiler_params=_cparams(("arbitrary", "arbitrary")),
        name="dsa_sample_scores",
    )(pt, iq_rows, w_rows, *([cache_ik] * group), ik_new)


def _sample_attn_kernel(pt_ref, q_ref, *rest, mode, n_pages, n_steps, dec_seq, group):
    kc_refs, vc_refs = rest[:group], rest[group:2 * group]
    rest = rest[2 * group:]
    if mode == "dsa":
        keys_ref, kn_ref, vn_ref, slope_ref, o_ref = rest[:5]
        m_sc, l_sc, acc_sc, carry_sc, thr_sc, need_sc = rest[5:]
    else:
        lf_refs, (kn_ref, vn_ref, lfn_ref, o_ref) = rest[:group], rest[group:group + 4]
        m_sc, l_sc, acc_sc, carry_sc = rest[group + 4:]
    s_id = pl.program_id(1)
    row = lax.broadcasted_iota(i32, (SROWS, PAGE), 0)
    col = lax.broadcasted_iota(i32, (SROWS, PAGE), 1)
    tok = row // N_HEADS
    qb = (q_ref[0] * QK_SCALE).astype(bf16)

    @pl.when(s_id == 0)
    def _():
        m_sc[...] = jnp.full(m_sc.shape, NEG, f32)
        l_sc[...] = jnp.zeros(l_sc.shape, f32)
        acc_sc[...] = jnp.zeros(acc_sc.shape, f32)
        carry_sc[...] = jnp.zeros(carry_sc.shape, f32)

    def update(ks, vs, biases):
        lg = jnp.concatenate([_dot(qb, k.astype(bf16)) + b for k, b in zip(ks, biases)], axis=1)
        m = m_sc[...]
        m_new = jnp.maximum(m, jnp.max(lg, axis=1, keepdims=True))
        a = jnp.exp(m - m_new)
        p = jnp.exp(lg - m_new)
        l_sc[...] = a * l_sc[...] + jnp.sum(p, axis=1, keepdims=True)
        pr = p.astype(bf16)
        pv = _dot_nt(pr[:, 0:PAGE], vs[0].astype(bf16))
        for g in range(1, len(vs)):
            pv = pv + _dot_nt(pr[:, g * PAGE:(g + 1) * PAGE], vs[g].astype(bf16))
        acc_sc[...] = a * acc_sc[...] + pv
        m_sc[...] = m_new

    if mode == "dsa":
        tri = _upper_tri(PAGE)

        @pl.when(s_id == 0)
        def _():
            kk = keys_ref[0].reshape((n_steps + 1) * group, SQ, PAGE)

            def count(pred):
                return jnp.sum(jnp.sum(jnp.where(pred(kk), 1.0, 0.0), axis=0), axis=1, keepdims=True)

            def bit_body(it, t):
                cand = t ^ lax.shift_left(jnp.int32(1), 31 - it)
                return jnp.where(count(lambda a: a >= cand[None]) >= TOPK, cand, t)

            thr = lax.fori_loop(0, 32, bit_body, jnp.full((SQ, 1), INT_MIN, i32))
            thr = jnp.maximum(thr, INT_MIN + 1)
            thr_sc[...] = thr
            need_sc[...] = TOPK - count(lambda a: a > thr[None])

        def dsa_biases(kps, first_page):
            thr, need = thr_sc[...], need_sc[...]
            before = carry_sc[0:SQ, :]
            out = []
            prefix = [_dot(jnp.where(kp == thr, 1.0, 0.0).astype(bf16), tri) for kp in kps]
            for g, kp in enumerate(kps):
                rank = before + prefix[g]
                sel = jnp.where((kp > thr) | ((kp == thr) & (rank <= need)), 1.0, 0.0)
                before = before + prefix[g][:, PAGE - 1:PAGE]
                sel_rows = jnp.concatenate(
                    [jnp.broadcast_to(sel[t:t + 1, :], (N_HEADS, PAGE)) for t in range(dec_seq)], axis=0)
                dist = (n_pages * PAGE + tok - ((first_page + g) * PAGE + col)).astype(f32)
                out.append(jnp.where(sel_rows > 0.5, -slope_ref[...] * dist, NEG))
            carry_sc[0:SQ, :] = before
            return out

        @pl.when(s_id < n_steps)
        def _():
            kps = [keys_ref[0, s_id, g] for g in range(group)]
            update([r[0, 0] for r in kc_refs], [r[0, 0] for r in vc_refs], dsa_biases(kps, s_id * group))

        @pl.when(s_id == n_steps)
        def _():
            update([kn_ref[0]], [vn_ref[0]], dsa_biases([keys_ref[0, n_steps, 0]], n_pages))
            o_ref[0] = acc_sc[...] / l_sc[...]
    else:
        tri = _upper_tri(PAGE)

        def fox_biases(lfs, mask_new):
            before = carry_sc[...]
            out = []
            prefix = [_dot3(lf_t, tri) for lf_t in lfs]
            for pre in prefix:
                cum = before + pre
                before = before + pre[:, PAGE - 1:PAGE]
                ck = jnp.concatenate([cum] * dec_seq, axis=0)
                out.append(jnp.where(col <= tok, -ck, NEG) if mask_new else -ck)
            carry_sc[...] = before
            return out

        @pl.when(s_id < n_steps)
        def _():
            update([r[0, 0] for r in kc_refs], [r[0, 0] for r in vc_refs],
                   fox_biases([r[0, 0] for r in lf_refs], False))

        @pl.when(s_id == n_steps)
        def _():
            update([kn_ref[0]], [vn_ref[0]], fox_biases([lfn_ref[0]], True))
            o_ref[0] = acc_sc[...] / l_sc[...]


def _sample_attn(mode, pt, q_rows, cache_k, cache_v, k_new, v_new, aux, aux_new, layer, n_pages, dec_seq):
    nb = q_rows.shape[0]
    group = PAGES_PER_STEP
    n_steps = n_pages // group
    kern = functools.partial(_sample_attn_kernel, mode=mode, n_pages=n_pages, n_steps=n_steps,
                             dec_seq=dec_seq, group=group)
    per_b = lambda b, s, pt: (b, 0, 0)
    kv_pages = _page_specs((DKV, PAGE), layer, n_pages, n_steps, group)
    new_page = pl.BlockSpec((1, DKV, PAGE), per_b)
    scratch = [pltpu.VMEM((SROWS, 1), f32), pltpu.VMEM((SROWS, 1), f32), pltpu.VMEM((SROWS, DKV), f32),
               pltpu.VMEM((N_HEADS, 1), f32)]
    if mode == "dsa":
        mid_specs = [pl.BlockSpec((1, n_steps + 1, group, SQ, PAGE), lambda b, s, pt: (b, 0, 0, 0, 0)),
                     new_page, new_page, pl.BlockSpec((SROWS, 1), lambda b, s, pt: (0, 0))]
        mid_args = [aux, k_new, v_new, aux_new]
        scratch += [pltpu.VMEM((SQ, 1), i32), pltpu.VMEM((SQ, 1), f32)]
    else:
        mid_specs = (_page_specs((N_HEADS, PAGE), layer, n_pages, n_steps, group)
                     + [new_page, new_page, pl.BlockSpec((1, N_HEADS, PAGE), per_b)])
        mid_args = [aux] * group + [k_new, v_new, aux_new]
    return pl.pallas_call(
        kern,
        grid_spec=pltpu.PrefetchScalarGridSpec(
            num_scalar_prefetch=1,
            grid=(nb, n_steps + 1),
            in_specs=[pl.BlockSpec((1, SROWS, DKV), per_b)] + kv_pages + kv_pages + mid_specs,
            out_specs=pl.BlockSpec((1, SROWS, DKV), per_b),
            scratch_shapes=scratch,
        ),
        out_shape=jax.ShapeDtypeStruct((nb, SROWS, DKV), f32),
        compiler_params=_cparams(("arbitrary", "arbitrary")),
        name=mode + "_sample_attn",
    )(pt, q_rows, *([cache_k] * group), *([cache_v] * group), *mid_args)


def _outproj_kernel(x_ref, op_ref, os_ref, w_ref, g_ref, b_ref, wr_hi_ref, wr_lo_ref, br_ref,
                    x1_ref, gate_ref, exp_t_ref, rank_t_ref, cnt_ref, run_sc, *, n_prompt_tiles):
    i = pl.program_id(0)
    o = jnp.where(i < n_prompt_tiles, op_ref[...], os_ref[...]).astype(bf16)
    x1 = _layer_norm(ALPHA * x_ref[...] + _dot(o, w_ref[...]), g_ref[...], b_ref[...])
    x1_ref[...] = x1

    hi = x1.astype(bf16)
    lo = (x1 - hi.astype(f32)).astype(bf16)
    lg = _dot(hi, wr_hi_ref[...]) + (_dot(lo, wr_hi_ref[...]) + _dot(hi, wr_lo_ref[...])) + br_ref[...]
    lane = lax.broadcasted_iota(i32, lg.shape, 1)
    out_lane = lax.broadcasted_iota(i32, (ROW_TILE, LANES), 1)
    vals = []
    experts = jnp.zeros((ROW_TILE, LANES), i32)
    for k in range(TOP_E):
        m = jnp.max(lg, axis=1, keepdims=True)
        idx = jnp.min(jnp.where(lg == m, lane, N_EXPERTS), axis=1, keepdims=True)
        vals.append(m)
        experts = jnp.where(out_lane == k, idx, experts)
        lg = jnp.where(lane == idx, -jnp.inf, lg)
    es = [jnp.exp(v - vals[0]) for v in vals]
    tot = es[0] + es[1] + es[2] + es[3]
    gates = jnp.zeros((ROW_TILE, LANES), f32)
    for k in range(TOP_E):
        gates = jnp.where(out_lane == k, es[k] / tot, gates)
    gate_ref[...] = gates

    @pl.when(i == 0)
    def _():
        run_sc[...] = jnp.zeros(run_sc.shape, f32)

    exp_t = experts.T[0:SUBLANES, :]
    e_iota = lax.broadcasted_iota(i32, (N_EXPERTS, ROW_TILE), 0)
    hits = [e_iota == exp_t[k:k + 1, :] for k in range(TOP_E)]
    member = jnp.where(hits[0] | hits[1] | hits[2] | hits[3], 1.0, 0.0)
    pos = run_sc[...] + _dot(member.astype(bf16), _upper_tri(ROW_TILE, strict=True))
    sub = lax.broadcasted_iota(i32, (SUBLANES, ROW_TILE), 0)
    rank_t = jnp.zeros((SUBLANES, ROW_TILE), f32)
    for k in range(TOP_E):
        rank_t = jnp.where(sub == k, jnp.sum(jnp.where(hits[k], pos, 0.0), axis=0, keepdims=True), rank_t)
    run_sc[...] = run_sc[...] + jnp.sum(member, axis=1, keepdims=True)
    exp_t_ref[...] = exp_t
    rank_t_ref[...] = rank_t.astype(i32)
    cnt_ref[...] = jnp.broadcast_to(run_sc[...], cnt_ref.shape).astype(i32)


def _outproj(x, o_prompt, o_sample, w_bf, g, b, wr_hi, wr_lo, br):
    n = x.shape[0]
    npt = o_prompt.shape[0] // ROW_TILE
    nst = o_sample.shape[0] // ROW_TILE
    row = lambda c: pl.BlockSpec((ROW_TILE, c), lambda i: (i, 0))
    col = pl.BlockSpec((SUBLANES, ROW_TILE), lambda i: (0, i))
    full = lambda r, c: pl.BlockSpec((r, c), lambda i: (0, 0))
    kern = functools.partial(_outproj_kernel, n_prompt_tiles=npt)
    return pl.pallas_call(
        kern,
        grid=(n // ROW_TILE,),
        in_specs=[row(D_MODEL),
                  pl.BlockSpec((ROW_TILE, DQ), lambda i: (jnp.minimum(i, npt - 1), 0)),
                  pl.BlockSpec((ROW_TILE, DQ), lambda i: (jnp.clip(i - npt, 0, nst - 1), 0)),
                  full(DQ, D_MODEL), full(1, D_MODEL), full(1, D_MODEL),
                  full(D_MODEL, N_EXPERTS), full(D_MODEL, N_EXPERTS), full(1, N_EXPERTS)],
        out_specs=[row(D_MODEL), row(LANES), col, col, full(N_EXPERTS, LANES)],
        out_shape=[jax.ShapeDtypeStruct((n, D_MODEL), f32),
                   jax.ShapeDtypeStruct((n, LANES), f32),
                   jax.ShapeDtypeStruct((SUBLANES, n), i32), jax.ShapeDtypeStruct((SUBLANES, n), i32),
                   jax.ShapeDtypeStruct((N_EXPERTS, LANES), i32)],
        scratch_shapes=[pltpu.VMEM((N_EXPERTS, 1), f32)],
        compiler_params=_cparams(("arbitrary",)),
        name="outproj_router",
    )(x, o_prompt, o_sample, w_bf, g, b, wr_hi, wr_lo, br)


def _moe_kernel(bexp_ref, used_ref, rows_ref, rows_next_ref, x_hbm, wgu_ref, bgu_ref, wdn_ref, bdn_ref,
                y_ref, wgu_bf, wdn_bf, xbuf, sem):
    i = pl.program_id(0)
    n_used = used_ref[0]
    slot = lax.rem(i, 2)
    e = bexp_ref[i]
    prev = bexp_ref[jnp.maximum(i - 1, 0)]

    def row_copy(tok, slot_, r):
        return pltpu.make_async_copy(x_hbm.at[pl.ds(tok, 1), :], xbuf.at[slot_, pl.ds(r, 1), :], sem.at[slot_])

    def start_block(tok_ref, slot_):
        for r in range(MOE_BLOCK):
            row_copy(tok_ref[0, 0, r], slot_, r).start()

    def wait_block():
        for r in range(MOE_BLOCK):
            row_copy(0, slot, r).wait()

    @pl.when(i == 0)
    def _():
        start_block(rows_ref, 0)

    @pl.when((i == 0) | (e != prev))
    def _():
        wgu_bf[...] = wgu_ref[0, 0].astype(bf16)
        wdn_bf[...] = wdn_ref[0, 0].astype(bf16)

    @pl.when(i < n_used)
    def _():
        wait_block()
        start_block(rows_next_ref, 1 - slot)
        h = _dot(xbuf[slot].astype(bf16), wgu_bf[...]) + bgu_ref[0, 0]
        gate = jnp.minimum(h[:, 0:D_FF], SWIGLU_LIMIT)
        up = jnp.clip(h[:, D_FF:2 * D_FF], -SWIGLU_LIMIT, SWIGLU_LIMIT)
        glu = gate * jax.nn.sigmoid(SWIGLU_ALPHA * gate)
        act = ((up + 1.0) * glu).astype(bf16)
        y_ref[...] = _dot(act, wdn_bf[...]) + bdn_ref[0, 0]

    @pl.when(i == n_used)
    def _():
        wait_block()

    @pl.when(i >= n_used)
    def _():
        y_ref[...] = jnp.zeros(y_ref.shape, f32)


def _moe(block_exp, n_used, row_tok, x, w_gu, b_gu, w_dn, b_dn, layer):
    n_blocks = block_exp.shape[0]
    rows3 = row_tok.reshape(n_blocks, 1, MOE_BLOCK)
    wspec = lambda r, c: pl.BlockSpec((1, 1, r, c), lambda i, be, nu: (layer, be[i], 0, 0))
    rows_spec = lambda ahead: pl.BlockSpec(
        (1, 1, MOE_BLOCK), lambda i, be, nu: (jnp.minimum(i + ahead, n_blocks - 1), 0, 0),
        memory_space=pltpu.SMEM)
    return pl.pallas_call(
        _moe_kernel,
        grid_spec=pltpu.PrefetchScalarGridSpec(
            num_scalar_prefetch=2,
            grid=(n_blocks,),
            in_specs=[rows_spec(0), rows_spec(1), pl.BlockSpec(memory_space=pl.ANY),
                      wspec(D_MODEL, 2 * D_FF), wspec(1, 2 * D_FF),
                      wspec(D_FF, D_MODEL), wspec(1, D_MODEL)],
            out_specs=pl.BlockSpec((MOE_BLOCK, D_MODEL), lambda i, be, nu: (i, 0)),
            scratch_shapes=[pltpu.VMEM((D_MODEL, 2 * D_FF), bf16), pltpu.VMEM((D_FF, D_MODEL), bf16),
                            pltpu.VMEM((2, MOE_BLOCK, D_MODEL), f32), pltpu.SemaphoreType.DMA((2,))],
        ),
        out_shape=jax.ShapeDtypeStruct((n_blocks * MOE_BLOCK, D_MODEL), f32),
        compiler_params=_cparams(("arbitrary",)),
        name="moe_experts",
    )(block_exp, n_used, rows3, rows3, x, w_gu, b_gu, w_dn, b_dn)


def _combine_kernel(dest_ref, dest_next_ref, x_ref, y_hbm, gate_ref, g_ref, b_ref, o_ref, ybuf, sem):
    i = pl.program_id(0)
    slot = lax.rem(i, 2)

    def row_copy(src_row, slot_, j):
        return pltpu.make_async_copy(y_hbm.at[pl.ds(src_row, 1), :], ybuf.at[slot_, pl.ds(j, 1), :], sem.at[slot_])

    def start_tile(ids_ref, slot_):
        for j in range(TOP_E * ROW_TILE):
            row_copy(ids_ref[0, 0, j], slot_, j).start(priority=j % 2)

    @pl.when(i == 0)
    def _():
        start_tile(dest_ref, 0)

    @pl.when(i + 1 < pl.num_programs(0))
    def _():
        start_tile(dest_next_ref, 1 - slot)

    for j in range(TOP_E * ROW_TILE):
        row_copy(0, slot, j).wait()
    gates = gate_ref[...]
    mix = gates[:, 0:1] * ybuf[slot, 0:ROW_TILE, :]
    for k in range(1, TOP_E):
        mix = mix + gates[:, k:k + 1] * ybuf[slot, k * ROW_TILE:(k + 1) * ROW_TILE, :]
    o_ref[...] = _layer_norm(ALPHA * x_ref[...] + mix, g_ref[...], b_ref[...])


def _combine(x1, y_rows, dest, gates, g, b):
    n = x1.shape[0]
    n_tiles = n // ROW_TILE
    ids = dest.reshape(TOP_E, n_tiles, ROW_TILE).transpose(1, 0, 2).reshape(n_tiles, 1, TOP_E * ROW_TILE)
    row = lambda c: pl.BlockSpec((ROW_TILE, c), lambda i: (i, 0))
    full = lambda r, c: pl.BlockSpec((r, c), lambda i: (0, 0))
    ids_spec = lambda ahead: pl.BlockSpec(
        (1, 1, TOP_E * ROW_TILE), lambda i: (jnp.minimum(i + ahead, n_tiles - 1), 0, 0),
        memory_space=pltpu.SMEM)
    return pl.pallas_call(
        _combine_kernel,
        grid=(n_tiles,),
        in_specs=[ids_spec(0), ids_spec(1), row(D_MODEL), pl.BlockSpec(memory_space=pl.ANY),
                  row(LANES), full(1, D_MODEL), full(1, D_MODEL)],
        out_specs=row(D_MODEL),
        out_shape=jax.ShapeDtypeStruct((n, D_MODEL), f32),
        scratch_shapes=[pltpu.VMEM((2, TOP_E * ROW_TILE, D_MODEL), f32), pltpu.SemaphoreType.DMA((2,))],
        compiler_params=_cparams(("arbitrary",)),
        name="combine_norm",
    )(ids, ids, x1, y_rows, gates, g, b)


def _route(exp_t, rank_t, counts, n_tok):
    padded = (counts + MOE_BLOCK - 1) // MOE_BLOCK * MOE_BLOCK
    pad_end = jnp.cumsum(padded)
    pad_start = pad_end - padded
    hit = exp_t[:, :, None] == jnp.arange(N_EXPERTS, dtype=i32)
    dest = (jnp.sum(jnp.where(hit, pad_start, 0), axis=-1) + rank_t).reshape(-1)
    n_asg = n_tok * TOP_E
    n_blocks = -(-n_asg // MOE_BLOCK) + N_EXPERTS + 1
    block_start = jnp.arange(n_blocks, dtype=i32) * MOE_BLOCK
    block_exp = jnp.minimum(jnp.sum(pad_end[None, :] <= block_start[:, None], axis=1), N_EXPERTS - 1).astype(i32)
    n_used = (pad_end[-1:] // MOE_BLOCK).astype(i32)
    n_dummy = n_blocks * MOE_BLOCK - n_asg
    pad_cnt = padded - counts
    cum_pad = jnp.cumsum(pad_cnt)
    j = jnp.arange(n_dummy, dtype=i32)
    e_j = jnp.sum(cum_pad[None, :] <= j[:, None], axis=1)
    first_pad = pad_start + counts - (cum_pad - pad_cnt)
    base = jnp.sum(jnp.where(e_j[:, None] == jnp.arange(N_EXPERTS, dtype=i32), first_pad, 0), axis=1)
    pad_row = jnp.where(e_j < N_EXPERTS, base, pad_end[-1] - cum_pad[-1]) + j
    tok = jnp.tile(jnp.arange(n_tok, dtype=i32), TOP_E)
    _, row_tok = lax.sort_key_val(jnp.concatenate([dest, pad_row]),
                                  jnp.concatenate([tok, jnp.zeros((n_dummy,), i32)]))
    return row_tok, block_exp, n_used, dest


def _pad_cols(w):
    cols = w.shape[1]
    padded = -(-cols // LANES) * LANES
    return jnp.pad(w, ((0, 0), (0, padded - cols))).astype(bf16)


def _diag_heads(o_full, nb, dec_seq):
    o6 = o_full.reshape(nb, dec_seq, N_KV, GROUP, N_KV, HEAD_DIM)
    own = jnp.eye(N_KV, dtype=jnp.bool_).reshape(1, 1, N_KV, 1, N_KV, 1)
    return jnp.sum(jnp.where(own, o6, 0.0), axis=4).reshape(nb * dec_seq, DQ)


def _block_diag_q(q_s, nb, dec_seq):
    q5 = q_s.reshape(nb, dec_seq, N_KV, GROUP, 1, HEAD_DIM)
    eye = jnp.eye(N_KV, dtype=q_s.dtype).reshape(1, 1, N_KV, 1, N_KV, 1)
    return (q5 * eye).reshape(nb, SROWS, DKV)


def _pad_page(a, nb, dec_seq):
    a3 = a.reshape(nb, dec_seq, a.shape[-1])
    return jnp.pad(a3, ((0, 0), (0, PAGE - dec_seq), (0, 0)))


def kernel(x_prompt, x_sample, cache_dsa_k, cache_dsa_v, cache_dsa_ik, cache_fox_k, cache_fox_v,
           cache_fox_logf, page_table, w_in_dsa, w_out_dsa, w_in_fox, b_forget, w_out_fox,
           ln_gain, ln_bias, w_router, b_router, w_gate_up, b_gate_up, w_down, b_down):
    batch, seq, _ = x_prompt.shape
    nb, dec_seq, _ = x_sample.shape
    n_pages = page_table.shape[1]
    pool = cache_dsa_k.shape[1]
    n_p = batch * seq
    n_s = nb * dec_seq
    n_tok = n_p + n_s
    assert dec_seq * N_HEADS == SROWS and n_s == ROW_TILE and n_p % ROW_TILE == 0 and n_tok % INPROJ_TILE == 0
    assert n_pages % PAGES_PER_STEP == 0 and seq % KEY_CHUNK == 0

    x = jnp.concatenate([x_prompt.reshape(n_p, D_MODEL), x_sample.reshape(n_s, D_MODEL)], axis=0)
    pt = page_table.reshape(-1).astype(i32)
    slopes = 2.0 ** (-8.0 * jnp.arange(1, N_HEADS + 1, dtype=f32) / N_HEADS)
    slope_rows = jnp.tile(slopes, dec_seq).reshape(SROWS, 1)

    kv_t = lambda c: jnp.transpose(c, (0, 1, 3, 4, 2)).reshape(-1, pool, DKV, PAGE)
    dsa_k, dsa_v, fox_k, fox_v = kv_t(cache_dsa_k), kv_t(cache_dsa_v), kv_t(cache_fox_k), kv_t(cache_fox_v)
    dsa_ik_t = jnp.swapaxes(cache_dsa_ik, 2, 3)
    fox_lf_t = jnp.swapaxes(cache_fox_logf, 2, 3)
    new_t = lambda a: jnp.swapaxes(_pad_page(a, nb, dec_seq), 1, 2)
    b_gu4 = b_gate_up.reshape(DEPTH, N_EXPERTS, 1, 2 * D_FF)
    b_dn4 = b_down.reshape(DEPTH, N_EXPERTS, 1, D_MODEL)

    outs = {name: [] for name in ("dk", "dv", "di", "fk", "fv", "fl")}
    for layer in range(DEPTH):
        j = layer // 2
        if layer % 2 == 0:
            q, k, v, iq, ik, iw = _inproj_dsa(x, _pad_cols(w_in_dsa[j]))
            o_p = _dsa_prompt(q, iq, iw, ik, k, v, batch, seq)
            iq_rows = jnp.pad(iq[n_p:].reshape(nb, dec_seq * IDX_HEADS, IDX_DIM),
                              ((0, 0), (0, (SQ - dec_seq) * IDX_HEADS), (0, 0)))
            w_rows = jnp.pad(iw[n_p:].reshape(nb, dec_seq * IDX_HEADS, 1),
                             ((0, 0), (0, (SQ - dec_seq) * IDX_HEADS), (0, 0)))
            keys = _dsa_sample_scores(pt, iq_rows, w_rows, dsa_ik_t, new_t(ik[n_p:]),
                                      j, n_pages, dec_seq)
            o_s = _sample_attn("dsa", pt, _block_diag_q(q[n_p:], nb, dec_seq), dsa_k, dsa_v,
                               new_t(k[n_p:]), new_t(v[n_p:]),
                               keys, slope_rows, j, n_pages, dec_seq)
            w_out = w_out_dsa[j]
            outs["dk"].append(k); outs["dv"].append(v); outs["di"].append(ik)
        else:
            q, k, v, lf = _inproj_fox(x, _pad_cols(w_in_fox[j]), b_forget[j])
            cum = _fox_cum(lf, batch, seq)
            o_p = _fox_prompt(q, cum, k, v, batch, seq)
            o_s = _sample_attn("fox", pt, _block_diag_q(q[n_p:], nb, dec_seq), fox_k, fox_v,
                               new_t(k[n_p:]), new_t(v[n_p:]),
                               fox_lf_t, new_t(lf[n_p:]), j, n_pages, dec_seq)
            w_out = w_out_fox[j]
            outs["fk"].append(k); outs["fv"].append(v); outs["fl"].append(lf)

        wr = w_router[layer]
        wr_hi = wr.astype(bf16)
        wr_lo = (wr - wr_hi.astype(f32)).astype(bf16)
        x1, gates, exp_t, rank_t, cnt = _outproj(
            x, o_p, _diag_heads(o_s, nb, dec_seq), w_out.astype(bf16),
            ln_gain[layer, 0].reshape(1, D_MODEL), ln_bias[layer, 0].reshape(1, D_MODEL),
            wr_hi, wr_lo, b_router[layer].reshape(1, N_EXPERTS))

        row_tok, block_exp, n_used, dest = _route(exp_t[:TOP_E], rank_t[:TOP_E], cnt[:, 0], n_tok)
        y_rows = _moe(block_exp, n_used, row_tok, x1, w_gate_up, b_gu4, w_down, b_dn4, layer)
        x = _combine(x1, y_rows, dest, gates, ln_gain[layer, 1].reshape(1, D_MODEL),
                     ln_bias[layer, 1].reshape(1, D_MODEL))

    def stack(name, shape_p, shape_s):
        a = jnp.stack(outs[name])
        return a[:, :n_p].reshape((-1,) + shape_p), a[:, n_p:].reshape((-1,) + shape_s)

    kv_p, kv_s = (batch, seq, N_KV, HEAD_DIM), (nb, dec_seq, N_KV, HEAD_DIM)
    dk_p, dk_s = stack("dk", kv_p, kv_s)
    dv_p, dv_s = stack("dv", kv_p, kv_s)
    di_p, di_s = stack("di", (batch, seq, IDX_DIM), (nb, dec_seq, IDX_DIM))
    fk_p, fk_s = stack("fk", kv_p, kv_s)
    fv_p, fv_s = stack("fv", kv_p, kv_s)
    fl_p, fl_s = stack("fl", (batch, seq, N_HEADS), (nb, dec_seq, N_HEADS))
    return (x[:n_p].reshape(batch, seq, D_MODEL), x[n_p:].reshape(nb, dec_seq, D_MODEL),
            dk_p, dv_p, di_p, fk_p, fv_p, fl_p, dk_s, dv_s, di_s, fk_s, fv_s, fl_s)
```
